```python
import jax, jax.numpy as jnp
from jax import lax
import numpy as np

D_MODEL = 1024
BATCH = 16
SEQ = 4096
DEPTH = 4

GRID_W = 64
CTX_LEN = 256
N_MIXERS = 3
NORM_EPS = 1e-6
NEG_INF = -1e30

NA_HEADS = 16
NA_HEAD_DIM = D_MODEL // NA_HEADS
NA_ROWS = 8
NA_COLS = 16

GDN_DK = 128
GDN_DV = 128
GDN_HEADS = D_MODEL // GDN_DK
GDN_QK_W = GDN_HEADS * GDN_DK
GDN_V_W = GDN_HEADS * GDN_DV
GDN_CONV_CH = 2 * GDN_QK_W + GDN_V_W
GDN_IN = GDN_CONV_CH + GDN_V_W + 4 * GDN_HEADS
GDN_CONV_K = 4
GDN_CHUNK = 64

POOL_WINDOWS = (2, 4, 8, 16)
POOL_GROUP = D_MODEL // len(POOL_WINDOWS)

D_FF = 2816
N_EXPERTS = 8
TOP_K = 2
D_FF_EXPERT = 3584
MOE_BLOCK = 512

kernel_name = "hybrid_na_gdn_pool_moe_diffusion_trunk"


def rmsnorm(x, g):
    xf = x.astype(jnp.float32)
    y = xf * lax.rsqrt(jnp.mean(xf * xf, axis=-1, keepdims=True) + NORM_EPS)
    return (y * g.astype(jnp.float32)).astype(x.dtype)


def modulate(h, shift, scale):
    return h * (1 + scale) + shift


def l2norm(x):
    return x * lax.rsqrt(jnp.sum(x * x, axis=-1, keepdims=True) + NORM_EPS)


def dense_attention(q, k, v):
    s = jnp.einsum('bqhd,bkhd->bhqk', q, k, preferred_element_type=jnp.float32) * (q.shape[-1] ** -0.5)
    p = jax.nn.softmax(s, axis=-1).astype(v.dtype)
    return jnp.einsum('bhqk,bkhd->bqhd', p, v)


def neighbourhood_attention(q, k, v, k_ctx, v_ctx, rpb):
    B, N, H, Dh = q.shape
    rows = N // GRID_W
    kr = min(NA_ROWS, rows)
    scale = Dh ** -0.5
    qg = q.reshape(B, rows, GRID_W, H, Dh)
    kg = k.reshape(B, rows, GRID_W, H, Dh)
    vg = v.reshape(B, rows, GRID_W, H, Dh)
    col = jnp.arange(GRID_W)
    c0 = jnp.clip(col - NA_COLS // 2, 0, GRID_W - NA_COLS)
    in_win = (col[None, :] >= c0[:, None]) & (col[None, :] < c0[:, None] + NA_COLS)
    dc = jnp.clip(col[None, :] - col[:, None], -(NA_COLS - 1), NA_COLS - 1) + NA_COLS - 1
    rpb_c = rpb[:, :, dc]

    def one_row(r):
        r0 = jnp.clip(r - kr // 2, 0, rows - kr)
        k_blk = lax.dynamic_slice_in_dim(kg, r0, kr, axis=1)
        v_blk = lax.dynamic_slice_in_dim(vg, r0, kr, axis=1)
        q_r = lax.dynamic_index_in_dim(qg, r, axis=1, keepdims=False)
        dr = r0 + jnp.arange(kr) - r + NA_ROWS - 1
        bias = jnp.transpose(rpb_c[:, dr], (0, 2, 1, 3))
        s_loc = jnp.einsum('bqhd,brkhd->bhqrk', q_r, k_blk,
                           preferred_element_type=jnp.float32) * scale + bias
        s_loc = jnp.where(in_win[:, None, :], s_loc, NEG_INF)
        s_ctx = jnp.einsum('bqhd,blhd->bhql', q_r, k_ctx,
                           preferred_element_type=jnp.float32) * scale
        s = jnp.concatenate([s_loc.reshape(B, H, GRID_W, kr * GRID_W), s_ctx], axis=-1)
        p = jax.nn.softmax(s, axis=-1).astype(v.dtype)
        p_loc = p[..., :kr * GRID_W].reshape(B, H, GRID_W, kr, GRID_W)
        p_ctx = p[..., kr * GRID_W:]
        return (jnp.einsum('bhqrk,brkhd->bqhd', p_loc, v_blk)
                + jnp.einsum('bhql,blhd->bqhd', p_ctx, v_ctx))

    out = lax.map(one_row, jnp.arange(rows))
    return jnp.moveaxis(out, 0, 1).reshape(B, N, H, Dh)


def na_mixer(h_ctx, h_lat, w_qkv, w_o, rpb, need_ctx):
    B, N, D = h_lat.shape
    L = h_ctx.shape[1]
    qkv_c = (h_ctx @ w_qkv).reshape(B, L, 3, NA_HEADS, NA_HEAD_DIM)
    qkv_l = (h_lat @ w_qkv).reshape(B, N, 3, NA_HEADS, NA_HEAD_DIM)
    o_lat = neighbourhood_attention(qkv_l[:, :, 0], qkv_l[:, :, 1], qkv_l[:, :, 2],
                                    qkv_c[:, :, 1], qkv_c[:, :, 2], rpb)
    y_lat = o_lat.reshape(B, N, D) @ w_o
    y_ctx = None
    if need_ctx:
        o_ctx = dense_attention(qkv_c[:, :, 0], qkv_c[:, :, 1], qkv_c[:, :, 2])
        y_ctx = o_ctx.reshape(B, L, D) @ w_o
    return y_ctx, y_lat


def conv_centred(x, w):
    K = w.shape[0]
    left = K // 2
    return lax.conv_general_dilated(x, w[:, None, :], window_strides=(1,),
                                    padding=[(left, K - 1 - left)],
                                    dimension_numbers=('NWC', 'WIO', 'NWC'),
                                    feature_group_count=x.shape[-1])


def gated_delta_chunked(q, k, v, g, beta, s0):
    B, H, T, DK = q.shape
    DV = v.shape[-1]
    C = GDN_CHUNK
    n = T // C
    q = q.reshape(B, H, n, C, DK)
    k = k.reshape(B, H, n, C, DK)
    v = v.reshape(B, H, n, C, DV)
    beta = beta.reshape(B, H, n, C)
    g = jnp.cumsum(g.reshape(B, H, n, C), axis=-1)
    idx = jnp.arange(C)
    incl = idx[:, None] >= idx[None, :]
    strict = idx[:, None] > idx[None, :]
    decay = jnp.exp(jnp.where(incl, g[..., :, None] - g[..., None, :], -jnp.inf))
    kb = k * beta[..., None]
    lower = jnp.where(strict, jnp.einsum('bhnid,bhnjd->bhnij', kb, k) * decay, 0.0)
    tmat = lower + jnp.eye(C, dtype=jnp.float32)
    rhs = jnp.concatenate([v * beta[..., None], kb * jnp.exp(g)[..., None]], axis=-1)
    sol = lax.linalg.triangular_solve(tmat, rhs, left_side=True, lower=True, unit_diagonal=True)
    u, w = sol[..., :DV], sol[..., DV:]
    a_qk = jnp.einsum('bhnid,bhnjd->bhnij', q, k) * decay
    g_last = g[..., -1]
    k_tail = k * jnp.exp(g_last[..., None] - g)[..., None]
    q_dec = q * jnp.exp(g)[..., None]

    def step(s, inp):
        q_c, w_c, u_c, a_c, k_c, gl = inp
        v_new = u_c - jnp.einsum('bhck,bhkv->bhcv', w_c, s)
        o = jnp.einsum('bhck,bhkv->bhcv', q_c, s) + jnp.einsum('bhcj,bhjv->bhcv', a_c, v_new)
        s = s * jnp.exp(gl)[..., None, None] + jnp.einsum('bhck,bhcv->bhkv', k_c, v_new)
        return s, o

    xs = tuple(jnp.moveaxis(t, 2, 0) for t in (q_dec, w, u, a_qk, k_tail, g_last))
    s, o = lax.scan(step, s0, xs)
    o = jnp.moveaxis(o, 0, 2).reshape(B, H, T, DV)
    return o, s


def gdn_features(h, w_in, conv_w, a_log, dt_bias):
    B, T, _ = h.shape
    proj = h @ w_in
    qkv = jax.nn.silu(conv_centred(proj[..., :GDN_CONV_CH], conv_w))
    z = proj[..., GDN_CONV_CH:GDN_CONV_CH + GDN_V_W]
    off = GDN_CONV_CH + GDN_V_W
    a = proj[..., off:off + 2 * GDN_HEADS].astype(jnp.float32).reshape(B, T, 2, GDN_HEADS)
    b = proj[..., off + 2 * GDN_HEADS:].astype(jnp.float32).reshape(B, T, 2, GDN_HEADS)
    qkv = qkv.astype(jnp.float32)
    q = qkv[..., :GDN_QK_W].reshape(B, T, GDN_HEADS, GDN_DK)
    k = qkv[..., GDN_QK_W:2 * GDN_QK_W].reshape(B, T, GDN_HEADS, GDN_DK)
    v = qkv[..., 2 * GDN_QK_W:].reshape(B, T, GDN_HEADS, GDN_DV)
    q = jnp.transpose(l2norm(q) * (GDN_DK ** -0.5), (0, 2, 1, 3))
    k = jnp.transpose(l2norm(k), (0, 2, 1, 3))
    v = jnp.transpose(v, (0, 2, 1, 3))
    g = -jnp.exp(a_log.astype(jnp.float32)) * jax.nn.softplus(a + dt_bias.astype(jnp.float32))
    g = jnp.transpose(g, (2, 0, 3, 1))
    beta = jnp.transpose(jax.nn.sigmoid(b), (2, 0, 3, 1))
    return q, k, v, g, beta, z


def gdn_bidirectional(feats, s0_fwd, s0_bwd):
    q, k, v, g, beta, _ = feats
    flip = lambda t: jnp.flip(t, axis=2)
    o_f, s_f = gated_delta_chunked(q, k, v, g[0], beta[0], s0_fwd)
    o_b, s_b = gated_delta_chunked(flip(q), flip(k), flip(v), flip(g[1]), flip(beta[1]), s0_bwd)
    return o_f + flip(o_b), s_f, s_b


def gdn_output(o, z, norm_g, w_o):
    B, H, T, DV = o.shape
    o = rmsnorm(jnp.transpose(o, (0, 2, 1, 3)), norm_g) * jax.nn.silu(z.reshape(B, T, H, DV))
    return o.reshape(B, T, H * DV).astype(z.dtype) @ w_o


def gdn_mixer(h_ctx, h_lat, w_in, conv_w, a_log, dt_bias, norm_g, w_o, need_ctx):
    B = h_lat.shape[0]
    f_ctx = gdn_features(h_ctx, w_in, conv_w, a_log, dt_bias)
    f_lat = gdn_features(h_lat, w_in, conv_w, a_log, dt_bias)
    s0 = jnp.zeros((B, GDN_HEADS, GDN_DK, GDN_DV), jnp.float32)
    o_ctx, s_cf, s_cb = gdn_bidirectional(f_ctx, s0, s0)
    o_lat, _, _ = gdn_bidirectional(f_lat, s_cf, s_cb)
    y_lat = gdn_output(o_lat, f_lat[5], norm_g, w_o)
    y_ctx = gdn_output(o_ctx, f_ctx[5], norm_g, w_o) if need_ctx else None
    return y_ctx, y_lat


def multiscale_pool(h, w_grp, ls):
    T = h.shape[-2]
    hf = h.astype(jnp.float32)
    t = jnp.arange(T)
    parts = []
    for gi, win in enumerate(POOL_WINDOWS):
        hg = hf[..., gi * POOL_GROUP:(gi + 1) * POOL_GROUP]
        cs = jnp.cumsum(hg, axis=-2)
        cs = jnp.concatenate([jnp.zeros_like(cs[..., :1, :]), cs], axis=-2)
        lo = jnp.clip(t - win // 2, 0, T)
        hi = jnp.clip(t + win // 2, 0, T)
        mean = (jnp.take(cs, hi, axis=-2) - jnp.take(cs, lo, axis=-2)) / (hi - lo).astype(jnp.float32)[:, None]
        parts.append(mean - hg)
    pooled = jnp.stack(parts, axis=-2).astype(h.dtype)
    y = jnp.einsum('...tng,ngf->...tnf', pooled, w_grp)
    return y.reshape(h.shape) * ls


def pool_mixer(h_ctx, h_lat, w_grp, ls, need_ctx):
    B, N, D = h_lat.shape
    rows = N // GRID_W
    y_lat = multiscale_pool(h_lat.reshape(B, rows, GRID_W, D), w_grp, ls).reshape(B, N, D)
    y_ctx = multiscale_pool(h_ctx, w_grp, ls) if need_ctx else None
    return y_ctx, y_lat


def swiglu(h, w1, w3, w2):
    return (jax.nn.silu(h @ w1) * (h @ w3)) @ w2


def moe_swiglu(h, w_router, w1, w3, w2):
    M, D = h.shape
    probs = jax.nn.softmax((h @ w_router).astype(jnp.float32), axis=-1)
    top_p, top_e = lax.top_k(probs, TOP_K)
    top_p = top_p / jnp.sum(top_p, axis=-1, keepdims=True)
    e_flat = top_e.reshape(-1)
    onehot = jax.nn.one_hot(e_flat, N_EXPERTS, dtype=jnp.int32)
    rank = jnp.sum(jnp.cumsum(onehot, axis=0) * onehot, axis=-1) - 1
    counts = jnp.sum(onehot, axis=0)
    padded = ((counts + MOE_BLOCK - 1) // MOE_BLOCK) * MOE_BLOCK
    ends = jnp.cumsum(padded)
    starts = ends - padded
    pos = (starts[e_flat] + rank).reshape(M, TOP_K)
    n_blocks = -(-(M * TOP_K) // MOE_BLOCK) + N_EXPERTS
    cap = n_blocks * MOE_BLOCK
    buf = jnp.zeros((cap, D), h.dtype)
    for j in range(TOP_K):
        buf = buf.at[pos[:, j]].set(h)
    block_e = jnp.minimum(jnp.searchsorted(ends, jnp.arange(n_blocks) * MOE_BLOCK, side='right'),
                          N_EXPERTS - 1)

    def run(args):
        xb, e = args
        return (jax.nn.silu(xb @ w1[e]) * (xb @ w3[e])) @ w2[e]

    out_buf = lax.map(run, (buf.reshape(n_blocks, MOE_BLOCK, D), block_e)).reshape(cap, D)
    return jnp.einsum('mkd,mk->md', out_buf[pos], top_p.astype(h.dtype))


def moe_streams(h_ctx, h_lat, w_router, w1, w3, w2):
    B, N, D = h_lat.shape
    if h_ctx is None:
        return None, moe_swiglu(h_lat.reshape(B * N, D), w_router, w1, w3, w2).reshape(B, N, D)
    L = h_ctx.shape[1]
    flat = jnp.concatenate([h_ctx.reshape(B * L, D), h_lat.reshape(B * N, D)], axis=0)
    y = moe_swiglu(flat, w_router, w1, w3, w2)
    return y[:B * L].reshape(B, L, D), y[B * L:].reshape(B, N, D)


def setup_inputs(seed: int = 0) -> dict:
    key = jax.random.key(seed)
    ks = iter(jax.random.split(key, 40))
    D = D_MODEL
    n_a = len(range(0, DEPTH, N_MIXERS))
    n_b = len(range(1, DEPTH, N_MIXERS))
    n_c = len(range(2, DEPTH, N_MIXERS))
    n_dense = len(range(0, DEPTH, 2))
    n_moe = len(range(1, DEPTH, 2))

    def nrm(shape, scale):
        return jax.random.normal(next(ks), shape, jnp.float32) * scale

    x = nrm((BATCH, SEQ, D), 1.0)
    c = nrm((BATCH, D), 1.0)
    ctx = nrm((BATCH, CTX_LEN, D), 1.0)
    c_ctx = nrm((D,), 1.0)
    ada_w = nrm((DEPTH, D, 6 * D), 0.5 * D ** -0.5)
    ada_b = nrm((DEPTH, 6 * D), 0.01)
    norm_g = 1.0 + nrm((DEPTH, 2, D), 0.05)
    final_g = 1.0 + nrm((D,), 0.05)
    na_w_qkv = nrm((n_a, D, 3 * D), D ** -0.5)
    na_w_o = nrm((n_a, D, D), D ** -0.5)
    na_rpb = nrm((n_a, NA_HEADS, 2 * NA_ROWS - 1, 2 * NA_COLS - 1), 0.2)
    gdn_w_in = nrm((n_b, D, GDN_IN), D ** -0.5)
    gdn_conv = nrm((n_b, GDN_CONV_K, GDN_CONV_CH), GDN_CONV_K ** -0.5)
    gdn_a_log = jnp.log(jax.random.uniform(next(ks), (n_b, 2, GDN_HEADS), jnp.float32, 1.0, 16.0))
    dt = jnp.exp(jax.random.uniform(next(ks), (n_b, 2, GDN_HEADS), jnp.float32,
                                    float(np.log(1e-3)), float(np.log(1e-1))))
    gdn_dt_bias = jnp.log(jnp.expm1(dt))
    gdn_norm_g = 1.0 + nrm((n_b, GDN_DV), 0.05)
    gdn_w_o = nrm((n_b, GDN_V_W, D), GDN_V_W ** -0.5)
    pool_w = nrm((n_c, len(POOL_WINDOWS), POOL_GROUP, POOL_GROUP), POOL_GROUP ** -0.5)
    pool_scale = 1.0 + nrm((n_c, D), 0.1)
    ffn_w1 = nrm((n_dense, D, D_FF), D ** -0.5)
    ffn_w3 = nrm((n_dense, D, D_FF), D ** -0.5)
    ffn_w2 = nrm((n_dense, D_FF, D), D_FF ** -0.5)
    moe_router = nrm((n_moe, D, N_EXPERTS), D ** -0.5)
    moe_w1 = nrm((n_moe, N_EXPERTS, D, D_FF_EXPERT), D ** -0.5)
    moe_w3 = nrm((n_moe, N_EXPERTS, D, D_FF_EXPERT), D ** -0.5)
    moe_w2 = nrm((n_moe, N_EXPERTS, D_FF_EXPERT, D), D_FF_EXPERT ** -0.5)
    return {"x": x, "c": c, "ctx": ctx, "c_ctx": c_ctx,
            "ada_w": ada_w, "ada_b": ada_b, "norm_g": norm_g, "final_g": final_g,
            "na_w_qkv": na_w_qkv, "na_w_o": na_w_o, "na_rpb": na_rpb,
            "gdn_w_in": gdn_w_in, "gdn_conv": gdn_conv, "gdn_a_log": gdn_a_log,
            "gdn_dt_bias": gdn_dt_bias, "gdn_norm_g": gdn_norm_g, "gdn_w_o": gdn_w_o,
            "pool_w": pool_w, "pool_scale": pool_scale,
            "ffn_w1": ffn_w1, "ffn_w3": ffn_w3, "ffn_w2": ffn_w2,
            "moe_router": moe_router, "moe_w1": moe_w1, "moe_w3": moe_w3, "moe_w2": moe_w2}


def reference(x, c, ctx, c_ctx, ada_w, ada_b, norm_g, final_g, na_w_qkv, na_w_o, na_rpb,
              gdn_w_in, gdn_conv, gdn_a_log, gdn_dt_bias, gdn_norm_g, gdn_w_o,
              pool_w, pool_scale, ffn_w1, ffn_w3, ffn_w2, moe_router, moe_w1, moe_w3, moe_w2):
    B, N, D = x.shape
    for i in range(DEPTH):
        last = i == DEPTH - 1
        m_lat = (jax.nn.silu(c) @ ada_w[i] + ada_b[i]).reshape(B, 1, 6, D)
        m_ctx = (jax.nn.silu(c_ctx) @ ada_w[i] + ada_b[i]).reshape(6, D)

        h_lat = modulate(rmsnorm(x, norm_g[i, 0]), m_lat[..., 0, :], m_lat[..., 1, :])
        h_ctx = modulate(rmsnorm(ctx, norm_g[i, 0]), m_ctx[0], m_ctx[1])
        j = i // N_MIXERS
        kind = i % N_MIXERS
        if kind == 0:
            y_ctx, y_lat = na_mixer(h_ctx, h_lat, na_w_qkv[j], na_w_o[j], na_rpb[j], not last)
        elif kind == 1:
            y_ctx, y_lat = gdn_mixer(h_ctx, h_lat, gdn_w_in[j], gdn_conv[j], gdn_a_log[j],
                                     gdn_dt_bias[j], gdn_norm_g[j], gdn_w_o[j], not last)
        else:
            y_ctx, y_lat = pool_mixer(h_ctx, h_lat, pool_w[j], pool_scale[j], not last)
        x = x + m_lat[..., 2, :] * y_lat
        if not last:
            ctx = ctx + m_ctx[2] * y_ctx

        h_lat = modulate(rmsnorm(x, norm_g[i, 1]), m_lat[..., 3, :], m_lat[..., 4, :])
        h_ctx = None if last else modulate(rmsnorm(ctx, norm_g[i, 1]), m_ctx[3], m_ctx[4])
        f = i // 2
        if i % 2 == 0:
            y_lat = swiglu(h_lat, ffn_w1[f], ffn_w3[f], ffn_w2[f])
            y_ctx = None if last else swiglu(h_ctx, ffn_w1[f], ffn_w3[f], ffn_w2[f])
        else:
            y_ctx, y_lat = moe_streams(h_ctx, h_lat, moe_router[f], moe_w1[f], moe_w3[f], moe_w2[f])
        x = x + m_lat[..., 5, :] * y_lat
        if not last:
            ctx = ctx + m_ctx[5] * y_ctx
    return rmsnorm(x, final_g)
```

```python
import functools

import numpy as np
import jax
import jax.numpy as jnp
from jax import lax
from jax.experimental import pallas as pl
from jax.experimental.pallas import tpu as pltpu

F32 = jnp.float32
BF16 = jnp.bfloat16
HI = lax.Precision.HIGHEST

NORM_EPS = 1e-6
NEG_INF = -1e30
GRID_W = 64
NA_HEADS = 16
NA_HEAD_DIM = 64
NA_ROWS = 8
NA_COLS = 16
GDN_HEADS = 8
GDN_DK = 128
GDN_CHUNK = 64
POOL_WINDOWS = (2, 4, 8, 16)
N_EXPERTS = 8
TOP_K = 2
MOE_BLOCK = 512
LANE = 128
V7X_VMEM_LIMIT = 56 * 1024 * 1024

NT = (((1,), (1,)), ((), ()))
TN = (((0,), (0,)), ((), ()))


def _cparams(sem, vmem=V7X_VMEM_LIMIT):
    return pltpu.CompilerParams(dimension_semantics=sem, vmem_limit_bytes=vmem)


def _silu(v):
    return v * jax.nn.sigmoid(v)


def _prenorm(x, g, shift, scale):
    ms = jnp.mean(x * x, axis=-1, keepdims=True)
    y = x * lax.rsqrt(ms + NORM_EPS) * g
    return y * (1.0 + scale) + shift


def _mod_index(i, tm, BL, N, B):
    nct = BL // tm
    return jnp.where(i < nct, B, (i - nct) // (N // tm))


def _mod_spec(tm, off, dims):
    BL, N, B, D = dims
    return pl.BlockSpec((1, 1, D), lambda i, *_: (_mod_index(i + off, tm, BL, N, B), 0, 0))


def _ada_kernel(c_ref, w_ref, b_ref, o_ref):
    cv = c_ref[...]
    o_ref[0] = jnp.dot(_silu(cv), w_ref[0], precision=HI, preferred_element_type=F32) + b_ref[0]


def ada_table(cvec, ada_w, ada_b):
    depth, D, D6 = ada_w.shape
    R = cvec.shape[0]
    tn = 1536
    return pl.pallas_call(
        _ada_kernel,
        grid=(depth, D6 // tn),
        in_specs=[pl.BlockSpec((R, D), lambda l, j: (0, 0)),
                  pl.BlockSpec((1, D, tn), lambda l, j: (l, 0, j)),
                  pl.BlockSpec((1, 1, tn), lambda l, j: (l, 0, j))],
        out_specs=pl.BlockSpec((1, R, tn), lambda l, j: (l, 0, j)),
        out_shape=jax.ShapeDtypeStruct((depth, R, D6), F32),
        compiler_params=_cparams(("arbitrary", "arbitrary")),
        name="ada_table",
    )(cvec, ada_w, ada_b.reshape(depth, 1, D6))


def _prenorm_matmul_kernel(x_ref, g_ref, sh_ref, sc_ref, w_ref, o_ref, *, precision):
    h = _prenorm(x_ref[...], g_ref[...], sh_ref[0], sc_ref[0]).astype(w_ref.dtype)
    o_ref[...] = jnp.dot(h, w_ref[...], precision=precision,
                         preferred_element_type=F32).astype(o_ref.dtype)


def prenorm_matmul(xs, g, shift, scale, w, dims, out_dtype, tm=512, precision=None):
    M, D = xs.shape
    Nout = w.shape[1]
    return pl.pallas_call(
        functools.partial(_prenorm_matmul_kernel, precision=precision),
        grid=(M // tm,),
        in_specs=[pl.BlockSpec((tm, D), lambda i: (i, 0)),
                  pl.BlockSpec((1, D), lambda i: (0, 0)),
                  _mod_spec(tm, 0, dims), _mod_spec(tm, 0, dims),
                  pl.BlockSpec((D, Nout), lambda i: (0, 0))],
        out_specs=pl.BlockSpec((tm, Nout), lambda i: (i, 0)),
        out_shape=jax.ShapeDtypeStruct((M, Nout), out_dtype),
        compiler_params=_cparams(("parallel",)),
        name="prenorm_matmul",
    )(xs, g.reshape(1, D), shift, scale, w)


def _matmul_res_kernel(a_ref, w_ref, x_ref, gate_ref, o_ref):
    y = jnp.dot(a_ref[...], w_ref[...], preferred_element_type=F32)
    o_ref[...] = x_ref[...] + gate_ref[0] * y


def matmul_residual(a, w, xs, gate, dims, row_start=0, tm=512):
    M, D = xs.shape
    K = a.shape[1]
    off = row_start // tm
    return pl.pallas_call(
        _matmul_res_kernel,
        grid=(a.shape[0] // tm,),
        in_specs=[pl.BlockSpec((tm, K), lambda i: (i, 0)),
                  pl.BlockSpec((K, D), lambda i: (0, 0)),
                  pl.BlockSpec((tm, D), lambda i: (i + off, 0)),
                  _mod_spec(tm, off, dims)],
        out_specs=pl.BlockSpec((tm, D), lambda i: (i + off, 0)),
        out_shape=jax.ShapeDtypeStruct((M, D), F32),
        input_output_aliases={2: 0},
        compiler_params=_cparams(("parallel",)),
        name="matmul_residual",
    )(a, w, xs, gate)


def _ffn_kernel(x_ref, g_ref, sh_ref, sc_ref, gate_ref, w1_ref, w3_ref, w2_ref, o_ref, a_ref, *, tf):
    x = x_ref[...]
    h = _prenorm(x, g_ref[...], sh_ref[0], sc_ref[0]).astype(BF16)
    F = w1_ref.shape[1]
    for f0 in range(0, F, tf):
        gg = jnp.dot(h, w1_ref[:, f0:f0 + tf], preferred_element_type=F32)
        uu = jnp.dot(h, w3_ref[:, f0:f0 + tf], preferred_element_type=F32)
        a_ref[:, f0:f0 + tf] = (_silu(gg) * uu).astype(BF16)
    y = jnp.dot(a_ref[...], w2_ref[...], preferred_element_type=F32)
    o_ref[...] = x + gate_ref[0] * y


def ffn_sublayer(xs, g, shift, scale, gate, w1, w3, w2, dims, row_start=0, tm=512, tf=256):
    M, D = xs.shape
    F = w1.shape[1]
    off = row_start // tm
    resident = dict(pipeline_mode=pl.Buffered(1))
    return pl.pallas_call(
        functools.partial(_ffn_kernel, tf=tf),
        grid=(M // tm - off,),
        in_specs=[pl.BlockSpec((tm, D), lambda i: (i + off, 0)),
                  pl.BlockSpec((1, D), lambda i: (0, 0)),
                  _mod_spec(tm, off, dims), _mod_spec(tm, off, dims), _mod_spec(tm, off, dims),
                  pl.BlockSpec((D, F), lambda i: (0, 0), **resident),
                  pl.BlockSpec((D, F), lambda i: (0, 0), **resident),
                  pl.BlockSpec((F, D), lambda i: (0, 0), **resident)],
        out_specs=pl.BlockSpec((tm, D), lambda i: (i + off, 0)),
        out_shape=jax.ShapeDtypeStruct((M, D), F32),
        scratch_shapes=[pltpu.VMEM((tm, F), BF16)],
        input_output_aliases={0: 0},
        compiler_params=_cparams(("parallel",)),
        name="ffn_sublayer",
    )(xs, g.reshape(1, D), shift, scale, gate, w1, w3, w2)


def _na_bias_table(rpb):
    H = rpb.shape[0]
    col = np.arange(GRID_W)
    c0 = np.clip(col - NA_COLS // 2, 0, GRID_W - NA_COLS)
    in_win = (col[None, :] >= c0[:, None]) & (col[None, :] < c0[:, None] + NA_COLS)
    dc = np.clip(col[None, :] - col[:, None], -(NA_COLS - 1), NA_COLS - 1) + NA_COLS - 1
    var = np.arange(NA_ROWS)[:, None]
    t = np.arange(NA_ROWS)[None, :]
    dr = t - var + NA_ROWS - 1
    tab = rpb[:, dr][:, :, :, dc]
    tab = jnp.where(jnp.asarray(in_win)[None, None, None], tab, NEG_INF)
    tab = tab.reshape(H // 2, 2, NA_ROWS, NA_ROWS, GRID_W, GRID_W)
    tab = jnp.transpose(tab, (0, 2, 3, 5, 1, 4))
    return tab.reshape(H // 2, NA_ROWS, NA_ROWS * GRID_W, 2 * GRID_W)


def _na_kernel(q_ref, k_ref, v_ref, kc_ref, vc_ref, bias_ref, o_ref, *, rows):
    rg = pl.program_id(2)
    W = GRID_W
    KW = NA_ROWS * W
    sub = lax.broadcasted_iota(jnp.int32, (2 * W, LANE), 0)
    lane = lax.broadcasted_iota(jnp.int32, (2 * W, LANE), 1)
    same_head = (sub < W) == (lane < W)
    lane_h = lax.broadcasted_iota(jnp.int32, (W, LANE), 1)
    kc = kc_ref[...]
    vc = vc_ref[...]
    scale = jnp.asarray(NA_HEAD_DIM ** -0.5, BF16)

    def body(r, carry):
        row = rg * NA_ROWS + r
        r0 = jnp.clip(row - NA_ROWS // 2, 0, rows - NA_ROWS)
        var = row - r0
        qr = q_ref[pl.ds(pl.multiple_of(r * W, W), W), :]
        q2 = jnp.concatenate([qr, qr], axis=0)
        q2 = jnp.where(same_head, q2, jnp.zeros_like(q2)) * scale
        kbase = pl.multiple_of(r0 * W, W)
        kw = k_ref[pl.ds(kbase, KW), :]
        s_loc = lax.dot_general(kw, q2, NT, preferred_element_type=F32)
        s_loc = s_loc + bias_ref[0, var]
        s_ctx = lax.dot_general(kc, q2, NT, preferred_element_type=F32)
        m = jnp.maximum(jnp.max(s_loc, axis=0, keepdims=True), jnp.max(s_ctx, axis=0, keepdims=True))
        p_loc = jnp.exp(s_loc - m)
        p_ctx = jnp.exp(s_ctx - m)
        inv = 1.0 / (jnp.sum(p_loc, axis=0, keepdims=True) + jnp.sum(p_ctx, axis=0, keepdims=True))
        p_loc = (p_loc * inv).astype(BF16)
        p_ctx = (p_ctx * inv).astype(BF16)
        vw = v_ref[pl.ds(kbase, KW), :]
        o = (lax.dot_general(p_loc, vw, TN, preferred_element_type=F32)
             + lax.dot_general(p_ctx, vc, TN, preferred_element_type=F32))
        o_sel = jnp.where(lane_h < W, o[:W], o[W:])
        o_ref[pl.ds(pl.multiple_of(r * W, W), W), :] = o_sel.astype(o_ref.dtype)
        return carry

    lax.fori_loop(0, NA_ROWS, body, 0)


def na_attention(qkv, bias_tab, dims, L):
    BL, N, B, D = dims
    rows = N // GRID_W
    HP = NA_HEADS // 2
    tq = NA_ROWS * GRID_W
    assert BL % N == 0 and N % tq == 0 and rows >= NA_ROWS
    return pl.pallas_call(
        functools.partial(_na_kernel, rows=rows),
        grid=(B, HP, rows // NA_ROWS),
        in_specs=[pl.BlockSpec((tq, LANE), lambda b, hp, rg: ((BL + b * N) // tq + rg, hp)),
                  pl.BlockSpec((N, LANE), lambda b, hp, rg: (BL // N + b, HP + hp)),
                  pl.BlockSpec((N, LANE), lambda b, hp, rg: (BL // N + b, 2 * HP + hp)),
                  pl.BlockSpec((L, LANE), lambda b, hp, rg: (b, HP + hp)),
                  pl.BlockSpec((L, LANE), lambda b, hp, rg: (b, 2 * HP + hp)),
                  pl.BlockSpec((1, NA_ROWS, tq, LANE), lambda b, hp, rg: (hp, 0, 0, 0))],
        out_specs=pl.BlockSpec((tq, LANE), lambda b, hp, rg: ((b * N) // tq + rg, hp)),
        out_shape=jax.ShapeDtypeStruct((B * N, D), BF16),
        compiler_params=_cparams(("parallel", "parallel", "arbitrary")),
        name="na_attention",
    )(qkv, qkv, qkv, qkv, qkv, bias_tab)


def _ctx_attn_kernel(q_ref, k_ref, v_ref, o_ref):
    q = q_ref[...]
    k = k_ref[...]
    v = v_ref[...]
    lane = lax.broadcasted_iota(jnp.int32, q.shape, 1)
    scale = jnp.asarray(NA_HEAD_DIM ** -0.5, BF16)
    outs = []
    for hh in range(2):
        msk = (lane < GRID_W) if hh == 0 else (lane >= GRID_W)
        qh = jnp.where(msk, q, jnp.zeros_like(q)) * scale
        s = lax.dot_general(qh, k, NT, preferred_element_type=F32)
        m = jnp.max(s, axis=-1, keepdims=True)
        p = jnp.exp(s - m)
        p = (p / jnp.sum(p, axis=-1, keepdims=True)).astype(BF16)
        outs.append(jnp.dot(p, v, preferred_element_type=F32))
    o_ref[...] = jnp.where(lane < GRID_W, outs[0], outs[1]).astype(o_ref.dtype)


def ctx_attention(qkv, dims, L):
    BL, N, B, D = dims
    HP = NA_HEADS // 2
    return pl.pallas_call(
        _ctx_attn_kernel,
        grid=(B, HP),
        in_specs=[pl.BlockSpec((L, LANE), lambda b, hp: (b, hp)),
                  pl.BlockSpec((L, LANE), lambda b, hp: (b, HP + hp)),
                  pl.BlockSpec((L, LANE), lambda b, hp: (b, 2 * HP + hp))],
        out_specs=pl.BlockSpec((L, LANE), lambda b, hp: (b, hp)),
        out_shape=jax.ShapeDtypeStruct((BL, D), BF16),
        compiler_params=_cparams(("parallel", "parallel")),
        name="ctx_attention",
    )(qkv, qkv, qkv)


def _gdn_feat_kernel(cur_ref, prev_ref, next_ref, cw_ref, o_ref, *, tm, BL, L, N):
    i = pl.program_id(0)
    j = pl.program_id(1)
    r0 = i * tm
    in_ctx = r0 < BL
    seg = jnp.where(in_ctx, L, N)
    off = jnp.where(in_ctx, r0, r0 - BL)
    keep_prev = jnp.where((off % seg) == 0, 0.0, 1.0)
    keep_next = jnp.where(((off + tm) % seg) == 0, 0.0, 1.0)
    row = lax.broadcasted_iota(jnp.int32, (tm, LANE), 0)
    qk_scale = jnp.where(j == 0, GDN_DK ** -0.5, 1.0)
    for c in range(cur_ref.shape[1] // LANE):
        sl = slice(c * LANE, (c + 1) * LANE)
        x = cur_ref[:, sl].astype(F32)
        pv = prev_ref[:, sl].astype(F32) * keep_prev
        nx = next_ref[:, sl].astype(F32) * keep_next
        w = cw_ref[:, sl]
        xm1 = jnp.where(row == 0, pv[15:16], pltpu.roll(x, 1, 0))
        xm2 = jnp.where(row == 0, pv[14:15], jnp.where(row == 1, pv[15:16], pltpu.roll(x, 2, 0)))
        xp1 = jnp.where(row == tm - 1, nx[0:1], pltpu.roll(x, tm - 1, 0))
        y = _silu(w[0:1] * xm2 + w[1:2] * xm1 + w[2:3] * x + w[3:4] * xp1)
        nrm = y * lax.rsqrt(jnp.sum(y * y, axis=-1, keepdims=True) + NORM_EPS) * qk_scale
        o_ref[:, sl] = jnp.where(j == 2, y, nrm).astype(o_ref.dtype)


def gdn_features(proj, conv_w, dims, L, tm=256):
    BL, N, B, D = dims
    M = proj.shape[0]
    C = GDN_HEADS * GDN_DK
    hb = tm // 16
    nhb = M // 16
    return pl.pallas_call(
        functools.partial(_gdn_feat_kernel, tm=tm, BL=BL, L=L, N=N),
        grid=(M // tm, 3),
        in_specs=[pl.BlockSpec((tm, C), lambda i, j: (i, j)),
                  pl.BlockSpec((16, C), lambda i, j: (jnp.maximum(i * hb - 1, 0), j)),
                  pl.BlockSpec((16, C), lambda i, j: (jnp.minimum((i + 1) * hb, nhb - 1), j)),
                  pl.BlockSpec((4, C), lambda i, j: (0, j))],
        out_specs=pl.BlockSpec((tm, C), lambda i, j: (i, j)),
        out_shape=jax.ShapeDtypeStruct((M, 3 * C), BF16),
        compiler_params=_cparams(("parallel", "arbitrary")),
        name="gdn_features",
    )(proj, proj, proj, conv_w)


def _gdn_intra_kernel(f_ref, ab_ref, pa_ref, pdt_ref, lm_ref, sm_ref,
                      w_ref, qd_ref, kt_ref, u_ref, aqk_ref, egl_ref, *, tm):
    d = pl.program_id(1)
    C = GDN_CHUNK
    H = GDN_HEADS
    DK = GDN_DK
    Lm = lm_ref[0]
    incl = Lm > 0.5
    strict = sm_ref[0] > 0.5
    eye = (lax.broadcasted_iota(jnp.int32, (C, C), 0) == lax.broadcasted_iota(jnp.int32, (C, C), 1)).astype(F32)
    lane = lax.broadcasted_iota(jnp.int32, (C, LANE), 1)
    pa = pa_ref[0]
    pdt = pdt_ref[0]

    def mm(a, b):
        return jnp.dot(a.astype(BF16), b.astype(BF16), preferred_element_type=F32)

    def chunk(c, carry):
        r = pl.multiple_of(c * C, C)
        ab = ab_ref[pl.ds(r, C), :]
        sp = jnp.maximum(ab + pdt, 0.0) + jnp.log1p(jnp.exp(-jnp.abs(ab + pdt)))
        gval = jnp.where(lane < H, pa * sp, 0.0)
        gc_all = jnp.dot(Lm, gval, precision=HI, preferred_element_type=F32)
        gc_t = gc_all.T
        beta_all = jax.nn.sigmoid(ab)
        g_last = jnp.where(d == 0, gc_all[C - 1:C], gc_all[0:1])
        for h in range(H):
            sl = slice(h * DK, (h + 1) * DK)
            q = f_ref[pl.ds(r, C), h * DK:(h + 1) * DK].astype(F32)
            k = f_ref[pl.ds(r, C), (H + h) * DK:(H + h + 1) * DK].astype(F32)
            v = f_ref[pl.ds(r, C), (2 * H + h) * DK:(2 * H + h + 1) * DK].astype(F32)
            gc = jnp.broadcast_to(gc_all[:, h:h + 1], (C, DK))
            diff = gc[:, :C] - gc_t[h:h + 1, :]
            decay = jnp.where(incl, jnp.exp(jnp.where(incl, diff, 0.0)), 0.0)
            beta = jnp.broadcast_to(beta_all[:, H + h:H + h + 1], (C, DK))
            kb = k * beta
            kk = lax.dot_general(kb.astype(BF16), k.astype(BF16), NT, preferred_element_type=F32)
            nm = jnp.where(strict, kk * decay, 0.0)
            xinv = eye - nm
            pw = nm
            for _ in range(5):
                pw = mm(pw, pw)
                xinv = xinv + mm(xinv, pw)
            eg = jnp.exp(gc)
            u = mm(xinv, v * beta)
            w = mm(xinv, kb * eg)
            aqk = lax.dot_general(q.astype(BF16), k.astype(BF16), NT, preferred_element_type=F32) * decay
            gl = jnp.broadcast_to(g_last[:, h:h + 1], (1, DK))
            w_ref[0, pl.ds(r, C), sl] = w.astype(w_ref.dtype)
            u_ref[0, pl.ds(r, C), sl] = u
            qd_ref[0, pl.ds(r, C), sl] = (q * eg).astype(qd_ref.dtype)
            kt_ref[0, pl.ds(r, C), sl] = (k * jnp.exp(gl - gc)).astype(kt_ref.dtype)
            aqk_ref[0, pl.ds(r, C), h * C:(h + 1) * C] = aqk.astype(aqk_ref.dtype)
            egl_ref[0, c, h:h + 1, :] = jnp.exp(gl)
        return carry

    lax.fori_loop(0, tm // C, chunk, 0)


def gdn_intra(feats, ab, a_log, dt_bias, tm=256):
    M = feats.shape[0]
    H, DK, C = GDN_HEADS, GDN_DK, GDN_CHUNK
    HD = H * DK
    pa = jnp.zeros((2, 1, LANE), F32).at[:, 0, :H].set(-jnp.exp(a_log.astype(F32)))
    pdt = jnp.zeros((2, 1, LANE), F32).at[:, 0, :H].set(dt_bias.astype(F32))
    idx = np.arange(C)
    lower = idx[:, None] >= idx[None, :]
    lm = jnp.asarray(np.stack([lower, lower.T]).astype(np.float32))
    sm = jnp.asarray(np.stack([idx[:, None] > idx[None, :], idx[:, None] < idx[None, :]]).astype(np.float32))
    big = lambda dt: jax.ShapeDtypeStruct((2, M, HD), dt)
    dspec = pl.BlockSpec((1, tm, HD), lambda i, d: (d, i, 0))
    return pl.pallas_call(
        functools.partial(_gdn_intra_kernel, tm=tm),
        grid=(M // tm, 2),
        in_specs=[pl.BlockSpec((tm, 3 * HD), lambda i, d: (i, 0)),
                  pl.BlockSpec((tm, LANE), lambda i, d: (i, d)),
                  pl.BlockSpec((1, 1, LANE), lambda i, d: (d, 0, 0)),
                  pl.BlockSpec((1, 1, LANE), lambda i, d: (d, 0, 0)),
                  pl.BlockSpec((1, C, C), lambda i, d: (d, 0, 0)),
                  pl.BlockSpec((1, C, C), lambda i, d: (d, 0, 0))],
        out_specs=[dspec, dspec, dspec, dspec,
                   pl.BlockSpec((1, tm, H * C), lambda i, d: (d, i, 0)),
                   pl.BlockSpec((1, tm // C, H, LANE), lambda i, d: (d, i, 0, 0))],
        out_shape=[big(BF16), big(BF16), big(BF16), big(F32),
                   jax.ShapeDtypeStruct((2, M, H * C), BF16),
                   jax.ShapeDtypeStruct((2, M // C, H, LANE), F32)],
        compiler_params=_cparams(("parallel", "arbitrary")),
        name="gdn_intra",
    )(feats, ab, pa, pdt, lm, sm)


def _gdn_scan_kernel(w_ref, qd_ref, kt_ref, u_ref, aqk_ref, egl_ref, o_ref, s_ref, *, nch):
    d = pl.program_id(1)
    C = GDN_CHUNK
    DK = GDN_DK

    @pl.when(pl.program_id(2) == 0)
    def _():
        s_ref[...] = jnp.zeros_like(s_ref)

    def chunk(j, carry):
        c = jnp.where(d == 0, j, nch - 1 - j)
        r = pl.multiple_of(c * C, C)
        for h in range(GDN_HEADS):
            sl = slice(h * DK, (h + 1) * DK)
            S = s_ref[h]
            Sb = S.astype(BF16)
            vnew = u_ref[0, pl.ds(r, C), sl] - jnp.dot(w_ref[0, pl.ds(r, C), sl], Sb, preferred_element_type=F32)
            vb = vnew.astype(BF16)
            o = (jnp.dot(qd_ref[0, pl.ds(r, C), sl], Sb, preferred_element_type=F32)
                 + jnp.dot(aqk_ref[0, pl.ds(r, C), h * C:(h + 1) * C], vb, preferred_element_type=F32))
            o_ref[0, pl.ds(r, C), sl] = o
            s_ref[h] = S * egl_ref[0, c, h:h + 1, :] + lax.dot_general(
                kt_ref[0, pl.ds(r, C), sl], vb, TN, preferred_element_type=F32)
        return carry

    lax.fori_loop(0, nch, chunk, 0)


def gdn_scan(w, qd, kt, u, aqk, egl, dims, L):
    BL, N, B, D = dims
    M = w.shape[1]
    H, DK, C = GDN_HEADS, GDN_DK, GDN_CHUNK
    HD = H * DK
    blk = L
    nlat = N // blk
    assert N % blk == 0 and blk % C == 0

    def rb(b, d, s):
        lat = BL // blk + b * nlat + jnp.where(d == 0, s - 1, nlat - s)
        return jnp.where(s == 0, b, lat)

    dspec = pl.BlockSpec((1, blk, HD), lambda b, d, s: (d, rb(b, d, s), 0))
    return pl.pallas_call(
        functools.partial(_gdn_scan_kernel, nch=blk // C),
        grid=(B, 2, 1 + nlat),
        in_specs=[dspec, dspec, dspec, dspec,
                  pl.BlockSpec((1, blk, H * C), lambda b, d, s: (d, rb(b, d, s), 0)),
                  pl.BlockSpec((1, blk // C, H, LANE), lambda b, d, s: (d, rb(b, d, s), 0, 0))],
        out_specs=dspec,
        out_shape=jax.ShapeDtypeStruct((2, M, HD), F32),
        scratch_shapes=[pltpu.VMEM((H, DK, DK), F32)],
        compiler_params=_cparams(("parallel", "parallel", "arbitrary")),
        name="gdn_scan",
    )(w, qd, kt, u, aqk, egl)


def _gdn_out_kernel(of_ref, ob_ref, z_ref, ng_ref, w_ref, x_ref, gate_ref, o_ref, a_ref):
    DK = GDN_DK
    for h in range(GDN_HEADS):
        sl = slice(h * DK, (h + 1) * DK)
        o = of_ref[0, :, sl] + ob_ref[0, :, sl]
        y = o * lax.rsqrt(jnp.mean(o * o, axis=-1, keepdims=True) + NORM_EPS) * ng_ref[...]
        z = z_ref[:, sl].astype(F32)
        a_ref[:, sl] = (y * _silu(z)).astype(BF16)
    y = jnp.dot(a_ref[...], w_ref[...], preferred_element_type=F32)
    o_ref[...] = x_ref[...] + gate_ref[0] * y


def gdn_output(o2, proj, norm_g, w_o, xs, gate, dims, row_start=0, tm=512):
    M, D = xs.shape
    HD = GDN_HEADS * GDN_DK
    off = row_start // tm
    return pl.pallas_call(
        _gdn_out_kernel,
        grid=(M // tm - off,),
        in_specs=[pl.BlockSpec((1, tm, HD), lambda i: (0, i + off, 0)),
                  pl.BlockSpec((1, tm, HD), lambda i: (1, i + off, 0)),
                  pl.BlockSpec((tm, HD), lambda i: (i + off, 3)),
                  pl.BlockSpec((1, GDN_DK), lambda i: (0, 0)),
                  pl.BlockSpec((HD, D), lambda i: (0, 0)),
                  pl.BlockSpec((tm, D), lambda i: (i + off, 0)),
                  _mod_spec(tm, off, dims)],
        out_specs=pl.BlockSpec((tm, D), lambda i: (i + off, 0)),
        out_shape=jax.ShapeDtypeStruct((M, D), F32),
        scratch_shapes=[pltpu.VMEM((tm, HD), BF16)],
        input_output_aliases={5: 0},
        compiler_params=_cparams(("parallel",)),
        name="gdn_output",
    )(o2, o2, proj, norm_g.reshape(1, GDN_DK), w_o, xs, gate)


def _pool_tables(tm, L):
    amats, invs = [], []
    t = np.arange(tm)
    for seg in (L, GRID_W):
        a_v, i_v = [], []
        tl = t % seg
        for win in POOL_WINDOWS:
            lo = np.clip(tl - win // 2, 0, seg)
            hi = np.clip(tl + win // 2, 0, seg)
            same = (t[:, None] // seg) == (t[None, :] // seg)
            a = same & (tl[None, :] >= lo[:, None]) & (tl[None, :] < hi[:, None])
            a_v.append(a.astype(np.float32))
            i_v.append(np.broadcast_to((1.0 / (hi - lo))[:, None], (tm, LANE)).astype(np.float32))
        amats.append(np.stack(a_v))
        invs.append(np.stack(i_v))
    return jnp.asarray(np.stack(amats), BF16), jnp.asarray(np.stack(invs), F32)


def _pool_kernel(x_ref, g_ref, sh_ref, sc_ref, gate_ref, a_ref, ic_ref, pw_ref, ls_ref, o_ref):
    x = x_ref[...]
    h = _prenorm(x, g_ref[...], sh_ref[0], sc_ref[0])
    G = pw_ref.shape[1]
    for gi in range(len(POOL_WINDOWS)):
        sl = slice(gi * G, (gi + 1) * G)
        hg = h[:, sl]
        hi = hg.astype(BF16)
        lo = (hg - hi.astype(F32)).astype(BF16)
        am = a_ref[0, gi]
        wsum = jnp.dot(am, hi, preferred_element_type=F32) + jnp.dot(am, lo, preferred_element_type=F32)
        ic = ic_ref[0, gi]
        mean = wsum * jnp.concatenate([ic] * (G // LANE), axis=-1)
        pooled = (mean - hg).astype(BF16)
        y = jnp.dot(pooled, pw_ref[gi], preferred_element_type=F32) * ls_ref[:, sl]
        o_ref[:, sl] = x[:, sl] + gate_ref[0][:, sl] * y


def pool_sublayer(xs, g, shift, scale, gate, pool_w, ls, dims, L, row_start=0, tm=256):
    BL, N, B, D = dims
    M = xs.shape[0]
    assert tm == L and tm % GRID_W == 0
    amat, inv = _pool_tables(tm, L)
    off = row_start // tm
    nct = BL // tm
    G = D // len(POOL_WINDOWS)
    variant = lambda i: jnp.where(i + off < nct, 0, 1)
    return pl.pallas_call(
        _pool_kernel,
        grid=(M // tm - off,),
        in_specs=[pl.BlockSpec((tm, D), lambda i: (i + off, 0)),
                  pl.BlockSpec((1, D), lambda i: (0, 0)),
                  _mod_spec(tm, off, dims), _mod_spec(tm, off, dims), _mod_spec(tm, off, dims),
                  pl.BlockSpec((1, 4, tm, tm), lambda i: (variant(i), 0, 0, 0)),
                  pl.BlockSpec((1, 4, tm, LANE), lambda i: (variant(i), 0, 0, 0)),
                  pl.BlockSpec((4, G, G), lambda i: (0, 0, 0)),
                  pl.BlockSpec((1, D), lambda i: (0, 0))],
        out_specs=pl.BlockSpec((tm, D), lambda i: (i + off, 0)),
        out_shape=jax.ShapeDtypeStruct((M, D), F32),
        input_output_aliases={0: 0},
        compiler_params=_cparams(("parallel",)),
        name="pool_sublayer",
    )(xs, g.reshape(1, D), shift, scale, gate, amat, inv, pool_w.astype(BF16), ls.reshape(1, D))


def _router_kernel(x_ref, g_ref, sh_ref, sc_ref, wr_ref, h_ref, ti_ref, tp_ref):
    h = _prenorm(x_ref[...], g_ref[...], sh_ref[0], sc_ref[0])
    h_ref[...] = h
    logits = jnp.dot(h, wr_ref[...], precision=HI, preferred_element_type=F32)
    lane = lax.broadcasted_iota(jnp.int32, logits.shape, 1)
    valid = lane < N_EXPERTS
    lg = jnp.where(valid, logits, NEG_INF)
    e = jnp.where(valid, jnp.exp(lg - jnp.max(lg, axis=-1, keepdims=True)), 0.0)
    probs = jnp.where(valid, e / jnp.sum(e, axis=-1, keepdims=True), -1.0)
    p1 = jnp.max(probs, axis=-1, keepdims=True)
    i1 = jnp.min(jnp.where(probs == p1, lane, LANE), axis=-1, keepdims=True)
    rest = jnp.where(lane == i1, -1.0, probs)
    p2 = jnp.max(rest, axis=-1, keepdims=True)
    i2 = jnp.min(jnp.where(rest == p2, lane, LANE), axis=-1, keepdims=True)
    tot = p1 + p2
    tp_ref[...] = jnp.where(lane == 0, p1 / tot, jnp.where(lane == 1, p2 / tot, 0.0))
    ti_ref[...] = jnp.where(lane == 0, i1, jnp.where(lane == 1, i2, 0))


def moe_router(xs, g, shift, scale, w_router, dims, row_start=0, tm=512):
    M, D = xs.shape
    off = row_start // tm
    wr = jnp.zeros((D, LANE), F32).at[:, :N_EXPERTS].set(w_router)
    Mo = M - row_start
    ospec = lambda w: pl.BlockSpec((tm, w), lambda i: (i, 0))
    return pl.pallas_call(
        _router_kernel,
        grid=(Mo // tm,),
        in_specs=[pl.BlockSpec((tm, D), lambda i: (i + off, 0)),
                  pl.BlockSpec((1, D), lambda i: (0, 0)),
                  _mod_spec(tm, off, dims), _mod_spec(tm, off, dims),
                  pl.BlockSpec((D, LANE), lambda i: (0, 0))],
        out_specs=[ospec(D), ospec(LANE), ospec(LANE)],
        out_shape=[jax.ShapeDtypeStruct((Mo, D), F32),
                   jax.ShapeDtypeStruct((Mo, LANE), jnp.int32),
                   jax.ShapeDtypeStruct((Mo, LANE), F32)],
        compiler_params=_cparams(("parallel",)),
        name="moe_router",
    )(xs, g.reshape(1, D), shift, scale, wr)


def _route_positions(top_e):
    n_tok = top_e.shape[0]
    e_flat = top_e.reshape(-1)
    onehot = (e_flat[:, None] == jnp.arange(N_EXPERTS, dtype=jnp.int32)[None, :]).astype(jnp.int32)
    csum = jnp.cumsum(onehot, axis=0)
    rank = jnp.sum(csum * onehot, axis=-1) - 1
    counts = csum[-1]
    padded = ((counts + MOE_BLOCK - 1) // MOE_BLOCK) * MOE_BLOCK
    ends = jnp.cumsum(padded)
    starts = ends - padded
    pos = (jnp.sum(onehot * starts[None, :], axis=-1) + rank).reshape(n_tok, TOP_K)
    n_blocks = -(-(n_tok * TOP_K) // MOE_BLOCK) + N_EXPERTS
    block_e = jnp.minimum(jnp.searchsorted(ends, jnp.arange(n_blocks) * MOE_BLOCK, side='right'),
                          N_EXPERTS - 1).astype(jnp.int32)
    return pos.astype(jnp.int32), block_e, n_blocks


def _dispatch_kernel(pos_ref, h_hbm, buf_in, buf_out, sem, *, tm):
    del buf_in
    base = pl.program_id(0) * tm

    def row_copy(t, p):
        return pltpu.make_async_copy(h_hbm.at[pl.ds(base + t, 1), :], buf_out.at[pl.ds(p, 1), :], sem)

    def issue(t, carry):
        for j in range(TOP_K):
            row_copy(t, pos_ref[0, 0, TOP_K * t + j]).start()
        return carry

    def drain(t, carry):
        row_copy(0, 0).wait()
        return carry

    lax.fori_loop(0, tm, issue, 0)
    lax.fori_loop(0, TOP_K * tm, drain, 0)


def moe_dispatch(h, pos, cap, tm=512):
    Mo, D = h.shape
    nt = Mo // tm
    return pl.pallas_call(
        functools.partial(_dispatch_kernel, tm=tm),
        grid=(nt,),
        in_specs=[pl.BlockSpec((1, 1, TOP_K * tm), lambda i: (i, 0, 0), memory_space=pltpu.SMEM),
                  pl.BlockSpec(memory_space=pl.ANY),
                  pl.BlockSpec(memory_space=pl.ANY)],
        out_specs=pl.BlockSpec(memory_space=pl.ANY),
        out_shape=jax.ShapeDtypeStruct((cap, D), F32),
        scratch_shapes=[pltpu.SemaphoreType.DMA(())],
        input_output_aliases={2: 0},
        compiler_params=_cparams(("arbitrary",)),
        name="moe_dispatch",
    )(pos.reshape(nt, 1, TOP_K * tm), h, jnp.zeros((cap, D), F32))


def _expert_kernel(be_ref, xb_ref, w1_ref, w3_ref, w2_ref, o_ref, hb_ref):
    del be_ref
    f = pl.program_id(1)

    @pl.when(f == 0)
    def _():
        hb_ref[...] = xb_ref[...].astype(BF16)

    hb = hb_ref[...]
    gg = jnp.dot(hb, w1_ref[0], preferred_element_type=F32)
    uu = jnp.dot(hb, w3_ref[0], preferred_element_type=F32)
    y = jnp.dot((_silu(gg) * uu).astype(BF16), w2_ref[0], preferred_element_type=F32)

    @pl.when(f == 0)
    def _():
        o_ref[...] = y

    @pl.when(f > 0)
    def _():
        o_ref[...] += y


def moe_experts(buf, block_e, w1, w3, w2, tf=512):
    cap, D = buf.shape
    F = w1.shape[2]
    nb = cap // MOE_BLOCK
    grid_spec = pltpu.PrefetchScalarGridSpec(
        num_scalar_prefetch=1,
        grid=(nb, F // tf),
        in_specs=[pl.BlockSpec((MOE_BLOCK, D), lambda i, f, be: (i, 0)),
                  pl.BlockSpec((1, D, tf), lambda i, f, be: (be[i], 0, f)),
                  pl.BlockSpec((1, D, tf), lambda i, f, be: (be[i], 0, f)),
                  pl.BlockSpec((1, tf, D), lambda i, f, be: (be[i], f, 0))],
        out_specs=pl.BlockSpec((MOE_BLOCK, D), lambda i, f, be: (i, 0)),
        scratch_shapes=[pltpu.VMEM((MOE_BLOCK, D), BF16)],
    )
    return pl.pallas_call(
        _expert_kernel,
        grid_spec=grid_spec,
        out_shape=jax.ShapeDtypeStruct((cap, D), F32),
        compiler_params=_cparams(("parallel", "arbitrary")),
        name="moe_experts",
    )(block_e, buf, w1, w3, w2)


def _combine_kernel(pos_ref, ob_hbm, tp_ref, x_ref, gate_ref, o_ref, rows_ref, sem, *, tm):
    def row_copy(j, t, p):
        return pltpu.make_async_copy(ob_hbm.at[pl.ds(p, 1), :], rows_ref.at[j, pl.ds(t, 1), :], sem)

    def issue(t, carry):
        for j in range(TOP_K):
            row_copy(j, t, pos_ref[0, 0, TOP_K * t + j]).start()
        return carry

    def drain(t, carry):
        row_copy(0, 0, 0).wait()
        return carry

    lax.fori_loop(0, tm, issue, 0)
    lax.fori_loop(0, TOP_K * tm, drain, 0)
    tp = tp_ref[...]
    y = tp[:, 0:1] * rows_ref[0] + tp[:, 1:2] * rows_ref[1]
    o_ref[...] = x_ref[...] + gate_ref[0] * y


def moe_combine(out_buf, pos, top_p, xs, gate, dims, row_start=0, tm=256):
    M, D = xs.shape
    off = row_start // tm
    nt = (M - row_start) // tm
    return pl.pallas_call(
        functools.partial(_combine_kernel, tm=tm),
        grid=(nt,),
        in_specs=[pl.BlockSpec((1, 1, TOP_K * tm), lambda i: (i, 0, 0), memory_space=pltpu.SMEM),
                  pl.BlockSpec(memory_space=pl.ANY),
                  pl.BlockSpec((tm, LANE), lambda i: (i, 0)),
                  pl.BlockSpec((tm, D), lambda i: (i + off, 0)),
                  _mod_spec(tm, off, dims)],
        out_specs=pl.BlockSpec((tm, D), lambda i: (i + off, 0)),
        out_shape=jax.ShapeDtypeStruct((M, D), F32),
        scratch_shapes=[pltpu.VMEM((TOP_K, tm, D), F32), pltpu.SemaphoreType.DMA(())],
        input_output_aliases={3: 0},
        compiler_params=_cparams(("arbitrary",)),
        name="moe_combine",
    )(pos.reshape(nt, 1, TOP_K * tm), out_buf, top_p, xs, gate)


def moe_sublayer(xs, g, shift, scale, gate, w_router, w1, w3, w2, dims, row_start):
    h, top_i, top_p = moe_router(xs, g, shift, scale, w_router, dims, row_start)
    pos, block_e, n_blocks = _route_positions(top_i[:, :TOP_K])
    buf = moe_dispatch(h, pos, n_blocks * MOE_BLOCK)
    out_buf = moe_experts(buf, block_e, w1, w3, w2)
    return moe_combine(out_buf, pos, top_p, xs, gate, dims, row_start)


def _final_norm_kernel(x_ref, g_ref, o_ref):
    x = x_ref[...]
    o_ref[...] = x * lax.rsqrt(jnp.mean(x * x, axis=-1, keepdims=True) + NORM_EPS) * g_ref[...]


def final_norm(xs, g, row_start, tm=512):
    M, D = xs.shape
    off = row_start // tm
    return pl.pallas_call(
        _final_norm_kernel,
        grid=((M - row_start) // tm,),
        in_specs=[pl.BlockSpec((tm, D), lambda i: (i + off, 0)),
                  pl.BlockSpec((1, D), lambda i: (0, 0))],
        out_specs=pl.BlockSpec((tm, D), lambda i: (i, 0)),
        out_shape=jax.ShapeDtypeStruct((M - row_start, D), F32),
        compiler_params=_cparams(("parallel",)),
        name="final_norm",
    )(xs, g.reshape(1, D))


def na_mixer(xs, g, m, w_qkv, w_o, rpb, dims, L, need_ctx):
    BL, N, B, D = dims
    qkv = prenorm_matmul(xs, g, m[0], m[1], w_qkv.astype(BF16), dims, BF16)
    bias_tab = _na_bias_table(rpb.astype(F32))
    w_o = w_o.astype(BF16)
    if need_ctx:
        xs = matmul_residual(ctx_attention(qkv, dims, L), w_o, xs, m[2], dims, row_start=0)
    return matmul_residual(na_attention(qkv, bias_tab, dims, L), w_o, xs, m[2], dims, row_start=BL)


def gdn_mixer(xs, g, m, w_in, conv_w, a_log, dt_bias, norm_g, w_o, dims, L, need_ctx):
    BL, N, B, D = dims
    H = GDN_HEADS
    HD = H * GDN_DK
    w_main = w_in[:, :4 * HD].astype(BF16)
    w_gate = w_in[:, 4 * HD:].astype(F32)
    w_ab = jnp.zeros((D, 2 * LANE), F32)
    for d in range(2):
        w_ab = w_ab.at[:, d * LANE:d * LANE + H].set(w_gate[:, d * H:(d + 1) * H])
        w_ab = w_ab.at[:, d * LANE + H:d * LANE + 2 * H].set(w_gate[:, (2 + d) * H:(3 + d) * H])
    proj = prenorm_matmul(xs, g, m[0], m[1], w_main, dims, BF16)
    ab = prenorm_matmul(xs, g, m[0], m[1], w_ab, dims, F32, precision=HI)
    feats = gdn_features(proj, conv_w.astype(F32), dims, L)
    w, qd, kt, u, aqk, egl = gdn_intra(feats, ab, a_log, dt_bias)
    o2 = gdn_scan(w, qd, kt, u, aqk, egl, dims, L)
    return gdn_output(o2, proj, norm_g.astype(F32), w_o.astype(BF16), xs, m[2], dims,
                      row_start=0 if need_ctx else BL)


def kernel(x, c, ctx, c_ctx, ada_w, ada_b, norm_g, final_g, na_w_qkv, na_w_o, na_rpb, gdn_w_in, gdn_conv, gdn_a_log, gdn_dt_bias, gdn_norm_g, gdn_w_o, pool_w, pool_scale, ffn_w1, ffn_w3, ffn_w2, moe_router, moe_w1, moe_w3, moe_w2):
    B, N, D = x.shape
    L = ctx.shape[1]
    depth = ada_w.shape[0]
    BL = B * L
    dims = (BL, N, B, D)
    xs = jnp.concatenate([ctx.reshape(BL, D), x.reshape(B * N, D)], axis=0)

    R = -(-(B + 1) // 8) * 8
    cvec = jnp.zeros((R, D), F32).at[:B].set(c).at[B].set(c_ctx)
    mods = ada_table(cvec, ada_w, ada_b)[:, :B + 1].reshape(depth, B + 1, 6, 1, D)

    for i in range(depth):
        last = i == depth - 1
        m = [mods[i, :, k] for k in range(6)]
        j = i // 3
        kind = i % 3
        if kind == 0:
            xs = na_mixer(xs, norm_g[i, 0], m, na_w_qkv[j], na_w_o[j], na_rpb[j], dims, L, not last)
        elif kind == 1:
            xs = gdn_mixer(xs, norm_g[i, 0], m, gdn_w_in[j], gdn_conv[j], gdn_a_log[j], gdn_dt_bias[j],
                           gdn_norm_g[j], gdn_w_o[j], dims, L, not last)
        else:
            xs = pool_sublayer(xs, norm_g[i, 0], m[0], m[1], m[2], pool_w[j], pool_scale[j], dims, L,
                               row_start=0 if not last else BL)
        row_start = BL if last else 0
        f = i // 2
        if i % 2 == 0:
            xs = ffn_sublayer(xs, norm_g[i, 1], m[3], m[4], m[5], ffn_w1[f].astype(BF16),
                              ffn_w3[f].astype(BF16), ffn_w2[f].astype(BF16), dims, row_start=row_start)
        else:
            xs = moe_sublayer(xs, norm_g[i, 1], m[3], m[4], m[5], moe_router[f], moe_w1[f].astype(BF16),
                              moe_w3[f].astype(BF16), moe_w2[f].astype(BF16), dims, row_start)
    return final_norm(xs, final_g, BL).reshape(B, N, D)
```

```python
import functools

import numpy as np
import jax
import jax.numpy as jnp
from jax import lax
from jax.experimental import pallas as pl
from jax.experimental.pallas import tpu as pltpu

F32 = jnp.float32
BF16 = jnp.bfloat16
HI = lax.Precision.HIGHEST

NORM_EPS = 1e-6
NEG_INF = -1e30
GRID_W = 64
NA_HEADS = 16
NA_HEAD_DIM = 64
NA_ROWS = 8
NA_COLS = 16
GDN_HEADS = 8
GDN_DK = 128
GDN_CHUNK = 64
POOL_WINDOWS = (2, 4, 8, 16)
N_EXPERTS = 8
TOP_K = 2
MOE_BLOCK = 512
LANE = 128
V7X_VMEM_LIMIT = 56 * 1024 * 1024

NT = (((1,), (1,)), ((), ()))
TN = (((0,), (0,)), ((), ()))


def _cparams(sem, vmem=V7X_VMEM_LIMIT):
    return pltpu.CompilerParams(dimension_semantics=sem, vmem_limit_bytes=vmem)


def _silu(v):
    return v * jax.nn.sigmoid(v)


def _prenorm(x, g, shift, scale):
    ms = jnp.mean(x * x, axis=-1, keepdims=True)
    y = x * lax.rsqrt(ms + NORM_EPS) * g
    return y * (1.0 + scale) + shift


def _mod_index(i, tm, BL, N, B):
    nct = BL // tm
    return jnp.where(i < nct, B, (i - nct) // (N // tm))


def _mod_spec(tm, off, dims):
    BL, N, B, D = dims
    return pl.BlockSpec((1, 1, D), lambda i, *_: (_mod_index(i + off, tm, BL, N, B), 0, 0))


def _ada_kernel(c_ref, w_ref, b_ref, o_ref):
    cv = c_ref[...]
    o_ref[0] = jnp.dot(_silu(cv), w_ref[0], precision=HI, preferred_element_type=F32) + b_ref[0]


def ada_table(cvec, ada_w, ada_b):
    depth, D, D6 = ada_w.shape
    R = cvec.shape[0]
    tn = 1536
    return pl.pallas_call(
        _ada_kernel,
        grid=(depth, D6 // tn),
        in_specs=[pl.BlockSpec((R, D), lambda l, j: (0, 0)),
                  pl.BlockSpec((1, D, tn), lambda l, j: (l, 0, j)),
                  pl.BlockSpec((1, 1, tn), lambda l, j: (l, 0, j))],
        out_specs=pl.BlockSpec((1, R, tn), lambda l, j: (l, 0, j)),
        out_shape=jax.ShapeDtypeStruct((depth, R, D6), F32),
        compiler_params=_cparams(("arbitrary", "arbitrary")),
        name="ada_table",
    )(cvec, ada_w, ada_b.reshape(depth, 1, D6))


def _prenorm_matmul_kernel(x_ref, g_ref, sh_ref, sc_ref, w_ref, o_ref, *, precision):
    h = _prenorm(x_ref[...], g_ref[...], sh_ref[0], sc_ref[0]).astype(w_ref.dtype)
    o_ref[...] = jnp.dot(h, w_ref[...], precision=precision,
                         preferred_element_type=F32).astype(o_ref.dtype)


def prenorm_matmul(xs, g, shift, scale, w, dims, out_dtype, tm=512, precision=None):
    M, D = xs.shape
    Nout = w.shape[1]
    return pl.pallas_call(
        functools.partial(_prenorm_matmul_kernel, precision=precision),
        grid=(M // tm,),
        in_specs=[pl.BlockSpec((tm, D), lambda i: (i, 0)),
                  pl.BlockSpec((1, D), lambda i: (0, 0)),
                  _mod_spec(tm, 0, dims), _mod_spec(tm, 0, dims),
                  pl.BlockSpec((D, Nout), lambda i: (0, 0))],
        out_specs=pl.BlockSpec((tm, Nout), lambda i: (i, 0)),
        out_shape=jax.ShapeDtypeStruct((M, Nout), out_dtype),
        compiler_params=_cparams(("parallel",)),
        name="prenorm_matmul",
    )(xs, g.reshape(1, D), shift, scale, w)


def _matmul_res_kernel(a_ref, w_ref, x_ref, gate_ref, o_ref):
    y = jnp.dot(a_ref[...], w_ref[...], preferred_element_type=F32)
    o_ref[...] = x_ref[...] + gate_ref[0] * y


def matmul_residual(a, w, xs, gate, dims, row_start=0, tm=512):
    M, D = xs.shape
    K = a.shape[1]
    off = row_start // tm
    return pl.pallas_call(
        _matmul_res_kernel,
        grid=(a.shape[0] // tm,),
        in_specs=[pl.BlockSpec((tm, K), lambda i: (i, 0)),
                  pl.BlockSpec((K, D), lambda i: (0, 0)),
                  pl.BlockSpec((tm, D), lambda i: (i + off, 0)),
                  _mod_spec(tm, off, dims)],
        out_specs=pl.BlockSpec((tm, D), lambda i: (i + off, 0)),
        out_shape=jax.ShapeDtypeStruct((M, D), F32),
        input_output_aliases={2: 0},
        compiler_params=_cparams(("parallel",)),
        name="matmul_residual",
    )(a, w, xs, gate)


def _ffn_kernel(x_ref, g_ref, sh_ref, sc_ref, gate_ref, w1_ref, w3_ref, w2_ref, o_ref, a_ref, *, tf):
    x = x_ref[...]
    h = _prenorm(x, g_ref[...], sh_ref[0], sc_ref[0]).astype(BF16)
    F = w1_ref.shape[1]
    for f0 in range(0, F, tf):
        gg = jnp.dot(h, w1_ref[:, f0:f0 + tf], preferred_element_type=F32)
        uu = jnp.dot(h, w3_ref[:, f0:f0 + tf], preferred_element_type=F32)
        a_ref[:, f0:f0 + tf] = (_silu(gg) * uu).astype(BF16)
    y = jnp.dot(a_ref[...], w2_ref[...], preferred_element_type=F32)
    o_ref[...] = x + gate_ref[0] * y


def ffn_sublayer(xs, g, shift, scale, gate, w1, w3, w2, dims, row_start=0, tm=512, tf=256):
    M, D = xs.shape
    F = w1.shape[1]
    off = row_start // tm
    resident = dict(pipeline_mode=pl.Buffered(1))
    return pl.pallas_call(
        functools.partial(_ffn_kernel, tf=tf),
        grid=(M // tm - off,),
        in_specs=[pl.BlockSpec((tm, D), lambda i: (i + off, 0)),
                  pl.BlockSpec((1, D), lambda i: (0, 0)),
                  _mod_spec(tm, off, dims), _mod_spec(tm, off, dims), _mod_spec(tm, off, dims),
                  pl.BlockSpec((D, F), lambda i: (0, 0), **resident),
                  pl.BlockSpec((D, F), lambda i: (0, 0), **resident),
                  pl.BlockSpec((F, D), lambda i: (0, 0), **resident)],
        out_specs=pl.BlockSpec((tm, D), lambda i: (i + off, 0)),
        out_shape=jax.ShapeDtypeStruct((M, D), F32),
        scratch_shapes=[pltpu.VMEM((tm, F), BF16)],
        input_output_aliases={0: 0},
        compiler_params=_cparams(("parallel",)),
        name="ffn_sublayer",
    )(xs, g.reshape(1, D), shift, scale, gate, w1, w3, w2)


def _na_bias_table(rpb):
    H = rpb.shape[0]
    col = np.arange(GRID_W)
    c0 = np.clip(col - NA_COLS // 2, 0, GRID_W - NA_COLS)
    in_win = (col[None, :] >= c0[:, None]) & (col[None, :] < c0[:, None] + NA_COLS)
    dc = np.clip(col[None, :] - col[:, None], -(NA_COLS - 1), NA_COLS - 1) + NA_COLS - 1
    var = np.arange(NA_ROWS)[:, None]
    t = np.arange(NA_ROWS)[None, :]
    dr = t - var + NA_ROWS - 1
    tab = rpb[:, dr][:, :, :, dc]
    tab = jnp.where(jnp.asarray(in_win)[None, None, None], tab, NEG_INF)
    tab = tab.reshape(H // 2, 2, NA_ROWS, NA_ROWS, GRID_W, GRID_W)
    tab = jnp.transpose(tab, (0, 2, 3, 5, 1, 4))
    return tab.reshape(H // 2, NA_ROWS, NA_ROWS * GRID_W, 2 * GRID_W)


def _na_kernel(q_ref, k_ref, v_ref, kc_ref, vc_ref, bias_ref, o_ref, *, rows, rows_per_iter):
    rg = pl.program_id(2)
    W = GRID_W
    KW = NA_ROWS * W
    sub = lax.broadcasted_iota(jnp.int32, (2 * W, LANE), 0)
    lane = lax.broadcasted_iota(jnp.int32, (2 * W, LANE), 1)
    same_head = (sub < W) == (lane < W)
    lane_h = lax.broadcasted_iota(jnp.int32, (W, LANE), 1)
    kc = kc_ref[...]
    vc = vc_ref[...]
    scale = jnp.asarray(NA_HEAD_DIM ** -0.5, BF16)

    def body(g, carry):
        rs = [g * rows_per_iter + t for t in range(rows_per_iter)]
        qoff, kbase, var, s_loc, s_ctx = [], [], [], [], []
        for r in rs:
            row = rg * NA_ROWS + r
            r0 = jnp.clip(row - NA_ROWS // 2, 0, rows - NA_ROWS)
            var.append(row - r0)
            qoff.append(pl.multiple_of(r * W, W))
            kbase.append(pl.multiple_of(r0 * W, W))
            qr = q_ref[pl.ds(qoff[-1], W), :]
            q2 = jnp.concatenate([qr, qr], axis=0)
            q2 = jnp.where(same_head, q2, jnp.zeros_like(q2)) * scale
            kw = k_ref[pl.ds(kbase[-1], KW), :]
            s_loc.append(lax.dot_general(kw, q2, NT, preferred_element_type=F32))
            s_ctx.append(lax.dot_general(kc, q2, NT, preferred_element_type=F32))
        p_loc, p_ctx = [], []
        for t in range(rows_per_iter):
            sl = s_loc[t] + bias_ref[0, var[t]]
            m = jnp.maximum(jnp.max(sl, axis=0, keepdims=True), jnp.max(s_ctx[t], axis=0, keepdims=True))
            pl_ = jnp.exp(sl - m)
            pc_ = jnp.exp(s_ctx[t] - m)
            inv = 1.0 / (jnp.sum(pl_, axis=0, keepdims=True) + jnp.sum(pc_, axis=0, keepdims=True))
            p_loc.append((pl_ * inv).astype(BF16))
            p_ctx.append((pc_ * inv).astype(BF16))
        for t in range(rows_per_iter):
            vw = v_ref[pl.ds(kbase[t], KW), :]
            o = (lax.dot_general(p_loc[t], vw, TN, preferred_element_type=F32)
                 + lax.dot_general(p_ctx[t], vc, TN, preferred_element_type=F32))
            o_sel = jnp.where(lane_h < W, o[:W], o[W:])
            o_ref[pl.ds(qoff[t], W), :] = o_sel.astype(o_ref.dtype)
        return carry

    lax.fori_loop(0, NA_ROWS // rows_per_iter, body, 0)


def na_attention(qkv, bias_tab, dims, L):
    BL, N, B, D = dims
    rows = N // GRID_W
    HP = NA_HEADS // 2
    tq = NA_ROWS * GRID_W
    assert BL % N == 0 and N % tq == 0 and rows >= NA_ROWS
    return pl.pallas_call(
        functools.partial(_na_kernel, rows=rows, rows_per_iter=4),
        grid=(B, HP, rows // NA_ROWS),
        in_specs=[pl.BlockSpec((tq, LANE), lambda b, hp, rg: ((BL + b * N) // tq + rg, hp)),
                  pl.BlockSpec((N, LANE), lambda b, hp, rg: (BL // N + b, HP + hp)),
                  pl.BlockSpec((N, LANE), lambda b, hp, rg: (BL // N + b, 2 * HP + hp)),
                  pl.BlockSpec((L, LANE), lambda b, hp, rg: (b, HP + hp)),
                  pl.BlockSpec((L, LANE), lambda b, hp, rg: (b, 2 * HP + hp)),
                  pl.BlockSpec((1, NA_ROWS, tq, LANE), lambda b, hp, rg: (hp, 0, 0, 0))],
        out_specs=pl.BlockSpec((tq, LANE), lambda b, hp, rg: ((b * N) // tq + rg, hp)),
        out_shape=jax.ShapeDtypeStruct((B * N, D), BF16),
        compiler_params=_cparams(("parallel", "parallel", "arbitrary")),
        name="na_attention",
    )(qkv, qkv, qkv, qkv, qkv, bias_tab)


def _ctx_attn_kernel(q_ref, k_ref, v_ref, o_ref):
    q = q_ref[...]
    k = k_ref[...]
    v = v_ref[...]
    lane = lax.broadcasted_iota(jnp.int32, q.shape, 1)
    scale = jnp.asarray(NA_HEAD_DIM ** -0.5, BF16)
    outs = []
    for hh in range(2):
        msk = (lane < GRID_W) if hh == 0 else (lane >= GRID_W)
        qh = jnp.where(msk, q, jnp.zeros_like(q)) * scale
        s = lax.dot_general(qh, k, NT, preferred_element_type=F32)
        m = jnp.max(s, axis=-1, keepdims=True)
        p = jnp.exp(s - m)
        p = (p / jnp.sum(p, axis=-1, keepdims=True)).astype(BF16)
        outs.append(jnp.dot(p, v, preferred_element_type=F32))
    o_ref[...] = jnp.where(lane < GRID_W, outs[0], outs[1]).astype(o_ref.dtype)


def ctx_attention(qkv, dims, L):
    BL, N, B, D = dims
    HP = NA_HEADS // 2
    return pl.pallas_call(
        _ctx_attn_kernel,
        grid=(B, HP),
        in_specs=[pl.BlockSpec((L, LANE), lambda b, hp: (b, hp)),
                  pl.BlockSpec((L, LANE), lambda b, hp: (b, HP + hp)),
                  pl.BlockSpec((L, LANE), lambda b, hp: (b, 2 * HP + hp))],
        out_specs=pl.BlockSpec((L, LANE), lambda b, hp: (b, hp)),
        out_shape=jax.ShapeDtypeStruct((BL, D), BF16),
        compiler_params=_cparams(("parallel", "parallel")),
        name="ctx_attention",
    )(qkv, qkv, qkv)


def _gdn_feat_kernel(cur_ref, prev_ref, next_ref, cw_ref, o_ref, *, tm, BL, L, N):
    i = pl.program_id(0)
    j = pl.program_id(1)
    r0 = i * tm
    in_ctx = r0 < BL
    seg = jnp.where(in_ctx, L, N)
    off = jnp.where(in_ctx, r0, r0 - BL)
    keep_prev = jnp.where((off % seg) == 0, 0.0, 1.0)
    keep_next = jnp.where(((off + tm) % seg) == 0, 0.0, 1.0)
    row = lax.broadcasted_iota(jnp.int32, (tm, LANE), 0)
    qk_scale = jnp.where(j == 0, GDN_DK ** -0.5, 1.0)
    for c in range(cur_ref.shape[1] // LANE):
        sl = slice(c * LANE, (c + 1) * LANE)
        x = cur_ref[:, sl].astype(F32)
        pv = prev_ref[:, sl].astype(F32) * keep_prev
        nx = next_ref[:, sl].astype(F32) * keep_next
        w = cw_ref[:, sl]
        xm1 = jnp.where(row == 0, pv[15:16], pltpu.roll(x, 1, 0))
        xm2 = jnp.where(row == 0, pv[14:15], jnp.where(row == 1, pv[15:16], pltpu.roll(x, 2, 0)))
        xp1 = jnp.where(row == tm - 1, nx[0:1], pltpu.roll(x, tm - 1, 0))
        y = _silu(w[0:1] * xm2 + w[1:2] * xm1 + w[2:3] * x + w[3:4] * xp1)
        nrm = y * lax.rsqrt(jnp.sum(y * y, axis=-1, keepdims=True) + NORM_EPS) * qk_scale
        o_ref[:, sl] = jnp.where(j == 2, y, nrm).astype(o_ref.dtype)


def gdn_features(proj, conv_w, dims, L, tm=256):
    BL, N, B, D = dims
    M = proj.shape[0]
    C = GDN_HEADS * GDN_DK
    hb = tm // 16
    nhb = M // 16
    return pl.pallas_call(
        functools.partial(_gdn_feat_kernel, tm=tm, BL=BL, L=L, N=N),
        grid=(M // tm, 3),
        in_specs=[pl.BlockSpec((tm, C), lambda i, j: (i, j)),
                  pl.BlockSpec((16, C), lambda i, j: (jnp.maximum(i * hb - 1, 0), j)),
                  pl.BlockSpec((16, C), lambda i, j: (jnp.minimum((i + 1) * hb, nhb - 1), j)),
                  pl.BlockSpec((4, C), lambda i, j: (0, j))],
        out_specs=pl.BlockSpec((tm, C), lambda i, j: (i, j)),
        out_shape=jax.ShapeDtypeStruct((M, 3 * C), BF16),
        compiler_params=_cparams(("parallel", "arbitrary")),
        name="gdn_features",
    )(proj, proj, proj, conv_w)


def _gdn_intra_kernel(f_ref, ab_ref, pa_ref, pdt_ref, lm_ref, sm_ref,
                      w_ref, qd_ref, kt_ref, u_ref, aqk_ref, egl_ref, *, tm, cpg):
    d = pl.program_id(1)
    C = GDN_CHUNK
    H = GDN_HEADS
    DK = GDN_DK
    Lm = lm_ref[0]
    incl = Lm > 0.5
    strict = sm_ref[0] > 0.5
    eye = (lax.broadcasted_iota(jnp.int32, (C, C), 0) == lax.broadcasted_iota(jnp.int32, (C, C), 1)).astype(F32)
    lane = lax.broadcasted_iota(jnp.int32, (C, LANE), 1)
    pa = pa_ref[0]
    pdt = pdt_ref[0]

    def mm(a, b):
        return jnp.dot(a.astype(BF16), b.astype(BF16), preferred_element_type=F32)

    def nt(a, b):
        return lax.dot_general(a.astype(BF16), b.astype(BF16), NT, preferred_element_type=F32)

    def chunk_group(cg, carry):
        pairs = [(cc, h) for cc in range(cpg) for h in range(H)]
        rows = [pl.multiple_of((cg * cpg + cc) * C, C) for cc in range(cpg)]
        gc_all, gc_t, beta_all, g_last = [], [], [], []
        for cc in range(cpg):
            ab = ab_ref[pl.ds(rows[cc], C), :]
            sp = jnp.maximum(ab + pdt, 0.0) + jnp.log1p(jnp.exp(-jnp.abs(ab + pdt)))
            gval = jnp.where(lane < H, pa * sp, 0.0)
            ga = jnp.dot(Lm, gval, precision=HI, preferred_element_type=F32)
            gc_all.append(ga)
            gc_t.append(ga.T)
            beta_all.append(jax.nn.sigmoid(ab))
            g_last.append(jnp.where(d == 0, ga[C - 1:C], ga[0:1]))
        ld = lambda cc, col: f_ref[pl.ds(rows[cc], C), col * DK:(col + 1) * DK]
        q16 = [ld(cc, h) for cc, h in pairs]
        k16 = [ld(cc, H + h) for cc, h in pairs]
        k = [t.astype(F32) for t in k16]
        gc = [jnp.broadcast_to(gc_all[cc][:, h:h + 1], (C, DK)) for cc, h in pairs]
        beta = [jnp.broadcast_to(beta_all[cc][:, H + h:H + h + 1], (C, DK)) for cc, h in pairs]
        decay = []
        for i, (cc, h) in enumerate(pairs):
            diff = gc[i][:, :C] - gc_t[cc][h:h + 1, :]
            decay.append(jnp.where(incl, jnp.exp(jnp.where(incl, diff, 0.0)), 0.0))
        kb = [k[i] * beta[i] for i in range(len(pairs))]
        nm = [jnp.where(strict, nt(kb[i], k16[i]) * decay[i], 0.0) for i in range(len(pairs))]
        xinv = [eye - t for t in nm]
        pw = [mm(t, t) for t in nm]
        for it in range(5):
            xinv = [xinv[i] + mm(xinv[i], pw[i]) for i in range(len(pairs))]
            if it < 4:
                pw = [mm(t, t) for t in pw]
        eg = [jnp.exp(t) for t in gc]
        u = [mm(xinv[i], ld(cc, 2 * H + h).astype(F32) * beta[i]) for i, (cc, h) in enumerate(pairs)]
        w = [mm(xinv[i], kb[i] * eg[i]) for i in range(len(pairs))]
        aqk = [nt(q16[i], k16[i]) * decay[i] for i in range(len(pairs))]
        for i, (cc, h) in enumerate(pairs):
            sl = slice(h * DK, (h + 1) * DK)
            r = rows[cc]
            gl = jnp.broadcast_to(g_last[cc][:, h:h + 1], (1, DK))
            w_ref[0, pl.ds(r, C), sl] = w[i].astype(w_ref.dtype)
            u_ref[0, pl.ds(r, C), sl] = u[i]
            qd_ref[0, pl.ds(r, C), sl] = (q16[i].astype(F32) * eg[i]).astype(qd_ref.dtype)
            kt_ref[0, pl.ds(r, C), sl] = (k[i] * jnp.exp(gl - gc[i])).astype(kt_ref.dtype)
            aqk_ref[0, pl.ds(r, C), h * C:(h + 1) * C] = aqk[i].astype(aqk_ref.dtype)
            egl_ref[0, cg * cpg + cc, h:h + 1, :] = jnp.exp(gl)
        return carry

    lax.fori_loop(0, tm // (C * cpg), chunk_group, 0)


def gdn_intra(feats, ab, a_log, dt_bias, tm=256):
    M = feats.shape[0]
    H, DK, C = GDN_HEADS, GDN_DK, GDN_CHUNK
    HD = H * DK
    pa = jnp.zeros((2, 1, LANE), F32).at[:, 0, :H].set(-jnp.exp(a_log.astype(F32)))
    pdt = jnp.zeros((2, 1, LANE), F32).at[:, 0, :H].set(dt_bias.astype(F32))
    idx = np.arange(C)
    lower = idx[:, None] >= idx[None, :]
    lm = jnp.asarray(np.stack([lower, lower.T]).astype(np.float32))
    sm = jnp.asarray(np.stack([idx[:, None] > idx[None, :], idx[:, None] < idx[None, :]]).astype(np.float32))
    big = lambda dt: jax.ShapeDtypeStruct((2, M, HD), dt)
    dspec = pl.BlockSpec((1, tm, HD), lambda i, d: (d, i, 0))
    return pl.pallas_call(
        functools.partial(_gdn_intra_kernel, tm=tm, cpg=2),
        grid=(M // tm, 2),
        in_specs=[pl.BlockSpec((tm, 3 * HD), lambda i, d: (i, 0)),
                  pl.BlockSpec((tm, LANE), lambda i, d: (i, d)),
                  pl.BlockSpec((1, 1, LANE), lambda i, d: (d, 0, 0)),
                  pl.BlockSpec((1, 1, LANE), lambda i, d: (d, 0, 0)),
                  pl.BlockSpec((1, C, C), lambda i, d: (d, 0, 0)),
                  pl.BlockSpec((1, C, C), lambda i, d: (d, 0, 0))],
        out_specs=[dspec, dspec, dspec, dspec,
                   pl.BlockSpec((1, tm, H * C), lambda i, d: (d, i, 0)),
                   pl.BlockSpec((1, tm // C, H, LANE), lambda i, d: (d, i, 0, 0))],
        out_shape=[big(BF16), big(BF16), big(BF16), big(F32),
                   jax.ShapeDtypeStruct((2, M, H * C), BF16),
                   jax.ShapeDtypeStruct((2, M // C, H, LANE), F32)],
        compiler_params=_cparams(("parallel", "arbitrary")),
        name="gdn_intra",
    )(feats, ab, pa, pdt, lm, sm)


def _gdn_scan_kernel(w_ref, qd_ref, kt_ref, u_ref, aqk_ref, egl_ref, o_ref, s_ref, *, nch):
    d = pl.program_id(1)
    C = GDN_CHUNK
    DK = GDN_DK

    @pl.when(pl.program_id(2) == 0)
    def _():
        s_ref[...] = jnp.zeros_like(s_ref)

    def chunk(j, carry):
        c = jnp.where(d == 0, j, nch - 1 - j)
        r = pl.multiple_of(c * C, C)
        hs = range(GDN_HEADS)
        sls = [slice(h * DK, (h + 1) * DK) for h in hs]
        S = [s_ref[h] for h in hs]
        Sb = [t.astype(BF16) for t in S]
        ws = [jnp.dot(w_ref[0, pl.ds(r, C), sls[h]], Sb[h], preferred_element_type=F32) for h in hs]
        qs = [jnp.dot(qd_ref[0, pl.ds(r, C), sls[h]], Sb[h], preferred_element_type=F32) for h in hs]
        vb = [(u_ref[0, pl.ds(r, C), sls[h]] - ws[h]).astype(BF16) for h in hs]
        av = [jnp.dot(aqk_ref[0, pl.ds(r, C), h * C:(h + 1) * C], vb[h], preferred_element_type=F32) for h in hs]
        kv = [lax.dot_general(kt_ref[0, pl.ds(r, C), sls[h]], vb[h], TN, preferred_element_type=F32) for h in hs]
        for h in hs:
            o_ref[0, pl.ds(r, C), sls[h]] = qs[h] + av[h]
            s_ref[h] = S[h] * egl_ref[0, c, h:h + 1, :] + kv[h]
        return carry

    lax.fori_loop(0, nch, chunk, 0)


def gdn_scan(w, qd, kt, u, aqk, egl, dims, L):
    BL, N, B, D = dims
    M = w.shape[1]
    H, DK, C = GDN_HEADS, GDN_DK, GDN_CHUNK
    HD = H * DK
    blk = L
    nlat = N // blk
    assert N % blk == 0 and blk % C == 0

    def rb(b, d, s):
        lat = BL // blk + b * nlat + jnp.where(d == 0, s - 1, nlat - s)
        return jnp.where(s == 0, b, lat)

    dspec = pl.BlockSpec((1, blk, HD), lambda b, d, s: (d, rb(b, d, s), 0))
    return pl.pallas_call(
        functools.partial(_gdn_scan_kernel, nch=blk // C),
        grid=(B, 2, 1 + nlat),
        in_specs=[dspec, dspec, dspec, dspec,
                  pl.BlockSpec((1, blk, H * C), lambda b, d, s: (d, rb(b, d, s), 0)),
                  pl.BlockSpec((1, blk // C, H, LANE), lambda b, d, s: (d, rb(b, d, s), 0, 0))],
        out_specs=dspec,
        out_shape=jax.ShapeDtypeStruct((2, M, HD), F32),
        scratch_shapes=[pltpu.VMEM((H, DK, DK), F32)],
        compiler_params=_cparams(("parallel", "parallel", "arbitrary")),
        name="gdn_scan",
    )(w, qd, kt, u, aqk, egl)


def _gdn_out_kernel(of_ref, ob_ref, z_ref, ng_ref, w_ref, x_ref, gate_ref, o_ref, a_ref):
    DK = GDN_DK
    for h in range(GDN_HEADS):
        sl = slice(h * DK, (h + 1) * DK)
        o = of_ref[0, :, sl] + ob_ref[0, :, sl]
        y = o * lax.rsqrt(jnp.mean(o * o, axis=-1, keepdims=True) + NORM_EPS) * ng_ref[...]
        z = z_ref[:, sl].astype(F32)
        a_ref[:, sl] = (y * _silu(z)).astype(BF16)
    y = jnp.dot(a_ref[...], w_ref[...], preferred_element_type=F32)
    o_ref[...] = x_ref[...] + gate_ref[0] * y


def gdn_output(o2, proj, norm_g, w_o, xs, gate, dims, row_start=0, tm=512):
    M, D = xs.shape
    HD = GDN_HEADS * GDN_DK
    off = row_start // tm
    return pl.pallas_call(
        _gdn_out_kernel,
        grid=(M // tm - off,),
        in_specs=[pl.BlockSpec((1, tm, HD), lambda i: (0, i + off, 0)),
                  pl.BlockSpec((1, tm, HD), lambda i: (1, i + off, 0)),
                  pl.BlockSpec((tm, HD), lambda i: (i + off, 3)),
                  pl.BlockSpec((1, GDN_DK), lambda i: (0, 0)),
                  pl.BlockSpec((HD, D), lambda i: (0, 0)),
                  pl.BlockSpec((tm, D), lambda i: (i + off, 0)),
                  _mod_spec(tm, off, dims)],
        out_specs=pl.BlockSpec((tm, D), lambda i: (i + off, 0)),
        out_shape=jax.ShapeDtypeStruct((M, D), F32),
        scratch_shapes=[pltpu.VMEM((tm, HD), BF16)],
        input_output_aliases={5: 0},
        compiler_params=_cparams(("parallel",)),
        name="gdn_output",
    )(o2, o2, proj, norm_g.reshape(1, GDN_DK), w_o, xs, gate)


def _pool_tables(tm, L):
    amats, invs = [], []
    t = np.arange(tm)
    for seg in (L, GRID_W):
        a_v, i_v = [], []
        tl = t % seg
        for win in POOL_WINDOWS:
            lo = np.clip(tl - win // 2, 0, seg)
            hi = np.clip(tl + win // 2, 0, seg)
            same = (t[:, None] // seg) == (t[None, :] // seg)
            a = same & (tl[None, :] >= lo[:, None]) & (tl[None, :] < hi[:, None])
            a_v.append(a.astype(np.float32))
            i_v.append(np.broadcast_to((1.0 / (hi - lo))[:, None], (tm, LANE)).astype(np.float32))
        amats.append(np.stack(a_v))
        invs.append(np.stack(i_v))
    return jnp.asarray(np.stack(amats), BF16), jnp.asarray(np.stack(invs), F32)


def _pool_kernel(x_ref, g_ref, sh_ref, sc_ref, gate_ref, a_ref, ic_ref, pw_ref, ls_ref, o_ref):
    x = x_ref[...]
    h = _prenorm(x, g_ref[...], sh_ref[0], sc_ref[0])
    G = pw_ref.shape[1]
    for gi in range(len(POOL_WINDOWS)):
        sl = slice(gi * G, (gi + 1) * G)
        hg = h[:, sl]
        hi = hg.astype(BF16)
        lo = (hg - hi.astype(F32)).astype(BF16)
        am = a_ref[0, gi]
        wsum = jnp.dot(am, hi, preferred_element_type=F32) + jnp.dot(am, lo, preferred_element_type=F32)
        ic = ic_ref[0, gi]
        mean = wsum * jnp.concatenate([ic] * (G // LANE), axis=-1)
        pooled = (mean - hg).astype(BF16)
        y = jnp.dot(pooled, pw_ref[gi], preferred_element_type=F32) * ls_ref[:, sl]
        o_ref[:, sl] = x[:, sl] + gate_ref[0][:, sl] * y


def pool_sublayer(xs, g, shift, scale, gate, pool_w, ls, dims, L, row_start=0, tm=256):
    BL, N, B, D = dims
    M = xs.shape[0]
    assert tm == L and tm % GRID_W == 0
    amat, inv = _pool_tables(tm, L)
    off = row_start // tm
    nct = BL // tm
    G = D // len(POOL_WINDOWS)
    variant = lambda i: jnp.where(i + off < nct, 0, 1)
    return pl.pallas_call(
        _pool_kernel,
        grid=(M // tm - off,),
        in_specs=[pl.BlockSpec((tm, D), lambda i: (i + off, 0)),
                  pl.BlockSpec((1, D), lambda i: (0, 0)),
                  _mod_spec(tm, off, dims), _mod_spec(tm, off, dims), _mod_spec(tm, off, dims),
                  pl.BlockSpec((1, 4, tm, tm), lambda i: (variant(i), 0, 0, 0)),
                  pl.BlockSpec((1, 4, tm, LANE), lambda i: (variant(i), 0, 0, 0)),
                  pl.BlockSpec((4, G, G), lambda i: (0, 0, 0)),
                  pl.BlockSpec((1, D), lambda i: (0, 0))],
        out_specs=pl.BlockSpec((tm, D), lambda i: (i + off, 0)),
        out_shape=jax.ShapeDtypeStruct((M, D), F32),
        input_output_aliases={0: 0},
        compiler_params=_cparams(("parallel",)),
        name="pool_sublayer",
    )(xs, g.reshape(1, D), shift, scale, gate, amat, inv, pool_w.astype(BF16), ls.reshape(1, D))


def _router_kernel(x_ref, g_ref, sh_ref, sc_ref, wr_ref, h_ref, ti_ref, tp_ref):
    h = _prenorm(x_ref[...], g_ref[...], sh_ref[0], sc_ref[0])
    h_ref[...] = h
    logits = jnp.dot(h, wr_ref[...], precision=HI, preferred_element_type=F32)
    lane = lax.broadcasted_iota(jnp.int32, logits.shape, 1)
    valid = lane < N_EXPERTS
    lg = jnp.where(valid, logits, NEG_INF)
    e = jnp.where(valid, jnp.exp(lg - jnp.max(lg, axis=-1, keepdims=True)), 0.0)
    probs = jnp.where(valid, e / jnp.sum(e, axis=-1, keepdims=True), -1.0)
    p1 = jnp.max(probs, axis=-1, keepdims=True)
    i1 = jnp.min(jnp.where(probs == p1, lane, LANE), axis=-1, keepdims=True)
    rest = jnp.where(lane == i1, -1.0, probs)
    p2 = jnp.max(rest, axis=-1, keepdims=True)
    i2 = jnp.min(jnp.where(rest == p2, lane, LANE), axis=-1, keepdims=True)
    tot = p1 + p2
    tp_ref[...] = jnp.where(lane == 0, p1 / tot, jnp.where(lane == 1, p2 / tot, 0.0))
    ti_ref[...] = jnp.where(lane == 0, i1, jnp.where(lane == 1, i2, 0))


def moe_router(xs, g, shift, scale, w_router, dims, row_start=0, tm=512):
    M, D = xs.shape
    off = row_start // tm
    wr = jnp.zeros((D, LANE), F32).at[:, :N_EXPERTS].set(w_router)
    Mo = M - row_start
    ospec = lambda w: pl.BlockSpec((tm, w), lambda i: (i, 0))
    return pl.pallas_call(
        _router_kernel,
        grid=(Mo // tm,),
        in_specs=[pl.BlockSpec((tm, D), lambda i: (i + off, 0)),
                  pl.BlockSpec((1, D), lambda i: (0, 0)),
                  _mod_spec(tm, off, dims), _mod_spec(tm, off, dims),
                  pl.BlockSpec((D, LANE), lambda i: (0, 0))],
        out_specs=[ospec(D), ospec(LANE), ospec(LANE)],
        out_shape=[jax.ShapeDtypeStruct((Mo, D), F32),
                   jax.ShapeDtypeStruct((Mo, LANE), jnp.int32),
                   jax.ShapeDtypeStruct((Mo, LANE), F32)],
        compiler_params=_cparams(("parallel",)),
        name="moe_router",
    )(xs, g.reshape(1, D), shift, scale, wr)


def _route_positions(top_e):
    n_tok = top_e.shape[0]
    e_flat = top_e.reshape(-1)
    onehot = (e_flat[:, None] == jnp.arange(N_EXPERTS, dtype=jnp.int32)[None, :]).astype(jnp.int32)
    csum = jnp.cumsum(onehot, axis=0)
    rank = jnp.sum(csum * onehot, axis=-1) - 1
    counts = csum[-1]
    padded = ((counts + MOE_BLOCK - 1) // MOE_BLOCK) * MOE_BLOCK
    ends = jnp.cumsum(padded)
    starts = ends - padded
    pos = (jnp.sum(onehot * starts[None, :], axis=-1) + rank).reshape(n_tok, TOP_K)
    n_blocks = -(-(n_tok * TOP_K) // MOE_BLOCK) + N_EXPERTS
    block_e = jnp.minimum(jnp.searchsorted(ends, jnp.arange(n_blocks) * MOE_BLOCK, side='right'),
                          N_EXPERTS - 1).astype(jnp.int32)
    pos = pos.astype(jnp.int32)
    src = jnp.zeros((n_blocks * MOE_BLOCK,), jnp.int32).at[pos.reshape(-1)].set(
        jnp.arange(n_tok * TOP_K, dtype=jnp.int32) // TOP_K, unique_indices=True)
    n_used = (ends[-1] // MOE_BLOCK).astype(jnp.int32).reshape(1)
    return pos, src, block_e, n_used, n_blocks


def _expert_kernel(be_ref, nu_ref, idx_ref, idxn_ref, h_hbm, w1_ref, w3_ref, w2_ref, o_ref,
                   xg_ref, hb_ref, sem):
    del be_ref
    i = pl.program_id(0)
    f = pl.program_id(1)
    n_used = nu_ref[0]
    slot = i % 2

    def issue(src_ref, s):
        def body(t, carry):
            pltpu.make_async_copy(h_hbm.at[pl.ds(src_ref[0, 0, t], 1), :],
                                  xg_ref.at[s, pl.ds(t, 1), :], sem.at[s]).start()
            return carry
        lax.fori_loop(0, MOE_BLOCK, body, 0)

    @pl.when((f == 0) & (i == 0) & (i < n_used))
    def _():
        issue(idx_ref, 0)

    @pl.when((f == 0) & (i < n_used))
    def _():
        pltpu.make_async_copy(h_hbm.at[pl.ds(0, MOE_BLOCK), :], xg_ref.at[slot], sem.at[slot]).wait()
        hb_ref[...] = xg_ref[slot].astype(BF16)

    @pl.when((f == 1) & (i + 1 < n_used))
    def _():
        issue(idxn_ref, 1 - slot)

    @pl.when(i < n_used)
    def _():
        hb = hb_ref[...]
        gg = jnp.dot(hb, w1_ref[0], preferred_element_type=F32)
        uu = jnp.dot(hb, w3_ref[0], preferred_element_type=F32)
        y = jnp.dot((_silu(gg) * uu).astype(BF16), w2_ref[0], preferred_element_type=F32)

        @pl.when(f == 0)
        def _():
            o_ref[...] = y

        @pl.when(f > 0)
        def _():
            o_ref[...] += y

    @pl.when((i >= n_used) & (f == 0))
    def _():
        o_ref[...] = jnp.zeros_like(o_ref)


def moe_experts(h, src, block_e, n_used, w1, w3, w2, n_blocks, tf=512):
    D = h.shape[1]
    F = w1.shape[2]
    assert F // tf >= 2
    last = n_blocks - 1
    grid_spec = pltpu.PrefetchScalarGridSpec(
        num_scalar_prefetch=2,
        grid=(n_blocks, F // tf),
        in_specs=[pl.BlockSpec((1, 1, MOE_BLOCK), lambda i, f, be, nu: (i, 0, 0), memory_space=pltpu.SMEM),
                  pl.BlockSpec((1, 1, MOE_BLOCK), lambda i, f, be, nu: (jnp.minimum(i + 1, last), 0, 0),
                               memory_space=pltpu.SMEM),
                  pl.BlockSpec(memory_space=pl.ANY),
                  pl.BlockSpec((1, D, tf), lambda i, f, be, nu: (be[i], 0, f)),
                  pl.BlockSpec((1, D, tf), lambda i, f, be, nu: (be[i], 0, f)),
                  pl.BlockSpec((1, tf, D), lambda i, f, be, nu: (be[i], f, 0))],
        out_specs=pl.BlockSpec((MOE_BLOCK, D), lambda i, f, be, nu: (i, 0)),
        scratch_shapes=[pltpu.VMEM((2, MOE_BLOCK, D), F32), pltpu.VMEM((MOE_BLOCK, D), BF16),
                        pltpu.SemaphoreType.DMA((2,))],
    )
    src3 = src.reshape(n_blocks, 1, MOE_BLOCK)
    return pl.pallas_call(
        _expert_kernel,
        grid_spec=grid_spec,
        out_shape=jax.ShapeDtypeStruct((n_blocks * MOE_BLOCK, D), F32),
        compiler_params=_cparams(("arbitrary", "arbitrary")),
        name="moe_experts",
    )(block_e, n_used, src3, src3, h, w1, w3, w2)


def _combine_kernel(pos_ref, posn_ref, ob_hbm, tp_ref, x_ref, gate_ref, o_ref, rows_ref, sem, *, tm, nt):
    i = pl.program_id(0)
    slot = i % 2

    def issue(p_ref, s):
        def body(t, carry):
            for j in range(TOP_K):
                pltpu.make_async_copy(ob_hbm.at[pl.ds(p_ref[0, 0, TOP_K * t + j], 1), :],
                                      rows_ref.at[s, pl.ds(j * tm + t, 1), :], sem.at[s]).start()
            return carry
        lax.fori_loop(0, tm, body, 0)

    @pl.when(i == 0)
    def _():
        issue(pos_ref, 0)

    @pl.when(i + 1 < nt)
    def _():
        issue(posn_ref, 1 - slot)

    pltpu.make_async_copy(ob_hbm.at[pl.ds(0, TOP_K * tm), :], rows_ref.at[slot], sem.at[slot]).wait()
    tp = tp_ref[...]
    y = tp[:, 0:1] * rows_ref[slot, 0:tm, :] + tp[:, 1:2] * rows_ref[slot, tm:2 * tm, :]
    o_ref[...] = x_ref[...] + gate_ref[0] * y


def moe_combine(out_buf, pos, top_p, xs, gate, dims, row_start=0, tm=512):
    M, D = xs.shape
    off = row_start // tm
    nt = (M - row_start) // tm
    pos3 = pos.reshape(nt, 1, TOP_K * tm)
    return pl.pallas_call(
        functools.partial(_combine_kernel, tm=tm, nt=nt),
        grid=(nt,),
        in_specs=[pl.BlockSpec((1, 1, TOP_K * tm), lambda i: (i, 0, 0), memory_space=pltpu.SMEM),
                  pl.BlockSpec((1, 1, TOP_K * tm), lambda i: (jnp.minimum(i + 1, nt - 1), 0, 0),
                               memory_space=pltpu.SMEM),
                  pl.BlockSpec(memory_space=pl.ANY),
                  pl.BlockSpec((tm, LANE), lambda i: (i, 0)),
                  pl.BlockSpec((tm, D), lambda i: (i + off, 0)),
                  _mod_spec(tm, off, dims)],
        out_specs=pl.BlockSpec((tm, D), lambda i: (i + off, 0)),
        out_shape=jax.ShapeDtypeStruct((M, D), F32),
        scratch_shapes=[pltpu.VMEM((2, TOP_K * tm, D), F32), pltpu.SemaphoreType.DMA((2,))],
        input_output_aliases={4: 0},
        compiler_params=_cparams(("arbitrary",)),
        name="moe_combine",
    )(pos3, pos3, out_buf, top_p, xs, gate)


def moe_sublayer(xs, g, shift, scale, gate, w_router, w1, w3, w2, dims, row_start):
    h, top_i, top_p = moe_router(xs, g, shift, scale, w_router, dims, row_start)
    pos, src, block_e, n_used, n_blocks = _route_positions(top_i[:, :TOP_K])
    out_buf = moe_experts(h, src, block_e, n_used, w1, w3, w2, n_blocks)
    return moe_combine(out_buf, pos, top_p, xs, gate, dims, row_start)


def _final_norm_kernel(x_ref, g_ref, o_ref):
    x = x_ref[...]
    o_ref[...] = x * lax.rsqrt(jnp.mean(x * x, axis=-1, keepdims=True) + NORM_EPS) * g_ref[...]


def final_norm(xs, g, row_start, tm=512):
    M, D = xs.shape
    off = row_start // tm
    return pl.pallas_call(
        _final_norm_kernel,
        grid=((M - row_start) // tm,),
        in_specs=[pl.BlockSpec((tm, D), lambda i: (i + off, 0)),
                  pl.BlockSpec((1, D), lambda i: (0, 0))],
        out_specs=pl.BlockSpec((tm, D), lambda i: (i, 0)),
        out_shape=jax.ShapeDtypeStruct((M - row_start, D), F32),
        compiler_params=_cparams(("parallel",)),
        name="final_norm",
    )(xs, g.reshape(1, D))


def na_mixer(xs, g, m, w_qkv, w_o, rpb, dims, L, need_ctx):
    BL, N, B, D = dims
    qkv = prenorm_matmul(xs, g, m[0], m[1], w_qkv.astype(BF16), dims, BF16)
    bias_tab = _na_bias_table(rpb.astype(F32))
    w_o = w_o.astype(BF16)
    if need_ctx:
        xs = matmul_residual(ctx_attention(qkv, dims, L), w_o, xs, m[2], dims, row_start=0)
    return matmul_residual(na_attention(qkv, bias_tab, dims, L), w_o, xs, m[2], dims, row_start=BL)


def gdn_mixer(xs, g, m, w_in, conv_w, a_log, dt_bias, norm_g, w_o, dims, L, need_ctx):
    BL, N, B, D = dims
    H = GDN_HEADS
    HD = H * GDN_DK
    w_main = w_in[:, :4 * HD].astype(BF16)
    w_gate = w_in[:, 4 * HD:].astype(F32)
    w_ab = jnp.zeros((D, 2 * LANE), F32)
    for d in range(2):
        w_ab = w_ab.at[:, d * LANE:d * LANE + H].set(w_gate[:, d * H:(d + 1) * H])
        w_ab = w_ab.at[:, d * LANE + H:d * LANE + 2 * H].set(w_gate[:, (2 + d) * H:(3 + d) * H])
    proj = prenorm_matmul(xs, g, m[0], m[1], w_main, dims, BF16)
    ab = prenorm_matmul(xs, g, m[0], m[1], w_ab, dims, F32, precision=HI)
    feats = gdn_features(proj, conv_w.astype(F32), dims, L)
    w, qd, kt, u, aqk, egl = gdn_intra(feats, ab, a_log, dt_bias)
    o2 = gdn_scan(w, qd, kt, u, aqk, egl, dims, L)
    return gdn_output(o2, proj, norm_g.astype(F32), w_o.astype(BF16), xs, m[2], dims,
                      row_start=0 if need_ctx else BL)


def kernel(x, c, ctx, c_ctx, ada_w, ada_b, norm_g, final_g, na_w_qkv, na_w_o, na_rpb, gdn_w_in, gdn_conv, gdn_a_log, gdn_dt_bias, gdn_norm_g, gdn_w_o, pool_w, pool_scale, ffn_w1, ffn_w3, ffn_w2, moe_router, moe_w1, moe_w3, moe_w2):
    B, N, D = x.shape
    L = ctx.shape[1]
    depth = ada_w.shape[0]
    BL = B * L
    dims = (BL, N, B, D)
    xs = jnp.concatenate([ctx.reshape(BL, D), x.reshape(B * N, D)], axis=0)

    R = -(-(B + 1) // 8) * 8
    cvec = jnp.zeros((R, D), F32).at[:B].set(c).at[B].set(c_ctx)
    mods = ada_table(cvec, ada_w, ada_b)[:, :B + 1].reshape(depth, B + 1, 6, 1, D)

    for i in range(depth):
        last = i == depth - 1
        m = [mods[i, :, k] for k in range(6)]
        j = i // 3
        kind = i % 3
        if kind == 0:
            xs = na_mixer(xs, norm_g[i, 0], m, na_w_qkv[j], na_w_o[j], na_rpb[j], dims, L, not last)
        elif kind == 1:
            xs = gdn_mixer(xs, norm_g[i, 0], m, gdn_w_in[j], gdn_conv[j], gdn_a_log[j], gdn_dt_bias[j],
                           gdn_norm_g[j], gdn_w_o[j], dims, L, not last)
        else:
            xs = pool_sublayer(xs, norm_g[i, 0], m[0], m[1], m[2], pool_w[j], pool_scale[j], dims, L,
                               row_start=0 if not last else BL)
        row_start = BL if last else 0
        f = i // 2
        if i % 2 == 0:
            xs = ffn_sublayer(xs, norm_g[i, 1], m[3], m[4], m[5], ffn_w1[f].astype(BF16),
                              ffn_w3[f].astype(BF16), ffn_w2[f].astype(BF16), dims, row_start=row_start)
        else:
            xs = moe_sublayer(xs, norm_g[i, 1], m[3], m[4], m[5], moe_router[f], moe_w1[f].astype(BF16),
                              moe_w3[f].astype(BF16), moe_w2[f].astype(BF16), dims, row_start)
    return final_norm(xs, final_g, BL).reshape(B, N, D)
```

```python
import functools

import numpy as np
import jax
import jax.numpy as jnp
from jax import lax
from jax.experimental import pallas as pl
from jax.experimental.pallas import tpu as pltpu

F32 = jnp.float32
BF16 = jnp.bfloat16
HI = lax.Precision.HIGHEST

NORM_EPS = 1e-6
NEG_INF = -1e30
GRID_W = 64
NA_HEADS = 16
NA_HEAD_DIM = 64
NA_ROWS = 8
NA_COLS = 16
GDN_HEADS = 8
GDN_DK = 128
GDN_CHUNK = 64
POOL_WINDOWS = (2, 4, 8, 16)
N_EXPERTS = 8
TOP_K = 2
MOE_BLOCK = 512
LANE = 128
V7X_VMEM_LIMIT = 56 * 1024 * 1024

NT = (((1,), (1,)), ((), ()))
TN = (((0,), (0,)), ((), ()))


def _cparams(sem, vmem=V7X_VMEM_LIMIT):
    return pltpu.CompilerParams(dimension_semantics=sem, vmem_limit_bytes=vmem)


def _silu(v):
    return v * jax.nn.sigmoid(v)


def _prenorm(x, g, shift, scale):
    ms = jnp.mean(x * x, axis=-1, keepdims=True)
    y = x * lax.rsqrt(ms + NORM_EPS) * g
    return y * (1.0 + scale) + shift


def _mod_index(i, tm, BL, N, B):
    nct = BL // tm
    return jnp.where(i < nct, B, (i - nct) // (N // tm))


def _mod_spec(tm, off, dims):
    BL, N, B, D = dims
    return pl.BlockSpec((1, 1, D), lambda i, *_: (_mod_index(i + off, tm, BL, N, B), 0, 0))


def _ada_kernel(c_ref, w_ref, b_ref, o_ref):
    cv = c_ref[...]
    o_ref[0] = jnp.dot(_silu(cv), w_ref[0], precision=HI, preferred_element_type=F32) + b_ref[0]


def ada_table(cvec, ada_w, ada_b):
    depth, D, D6 = ada_w.shape
    R = cvec.shape[0]
    tn = 1536
    return pl.pallas_call(
        _ada_kernel,
        grid=(depth, D6 // tn),
        in_specs=[pl.BlockSpec((R, D), lambda l, j: (0, 0)),
                  pl.BlockSpec((1, D, tn), lambda l, j: (l, 0, j)),
                  pl.BlockSpec((1, 1, tn), lambda l, j: (l, 0, j))],
        out_specs=pl.BlockSpec((1, R, tn), lambda l, j: (l, 0, j)),
        out_shape=jax.ShapeDtypeStruct((depth, R, D6), F32),
        compiler_params=_cparams(("arbitrary", "arbitrary")),
        name="ada_table",
    )(cvec, ada_w, ada_b.reshape(depth, 1, D6))


def _prenorm_matmul_kernel(x_ref, g_ref, sh_ref, sc_ref, w_ref, o_ref, *, precision):
    h = _prenorm(x_ref[...], g_ref[...], sh_ref[0], sc_ref[0]).astype(w_ref.dtype)
    o_ref[...] = jnp.dot(h, w_ref[...], precision=precision,
                         preferred_element_type=F32).astype(o_ref.dtype)


def prenorm_matmul(xs, g, shift, scale, w, dims, out_dtype, tm=512, precision=None):
    M, D = xs.shape
    Nout = w.shape[1]
    return pl.pallas_call(
        functools.partial(_prenorm_matmul_kernel, precision=precision),
        grid=(M // tm,),
        in_specs=[pl.BlockSpec((tm, D), lambda i: (i, 0)),
                  pl.BlockSpec((1, D), lambda i: (0, 0)),
                  _mod_spec(tm, 0, dims), _mod_spec(tm, 0, dims),
                  pl.BlockSpec((D, Nout), lambda i: (0, 0))],
        out_specs=pl.BlockSpec((tm, Nout), lambda i: (i, 0)),
        out_shape=jax.ShapeDtypeStruct((M, Nout), out_dtype),
        compiler_params=_cparams(("parallel",)),
        name="prenorm_matmul",
    )(xs, g.reshape(1, D), shift, scale, w)


def _matmul_res_kernel(a_ref, w_ref, x_ref, gate_ref, o_ref):
    y = jnp.dot(a_ref[...], w_ref[...], preferred_element_type=F32)
    o_ref[...] = x_ref[...] + gate_ref[0] * y


def matmul_residual(a, w, xs, gate, dims, row_start=0, tm=512):
    M, D = xs.shape
    K = a.shape[1]
    off = row_start // tm
    return pl.pallas_call(
        _matmul_res_kernel,
        grid=(a.shape[0] // tm,),
        in_specs=[pl.BlockSpec((tm, K), lambda i: (i, 0)),
                  pl.BlockSpec((K, D), lambda i: (0, 0)),
                  pl.BlockSpec((tm, D), lambda i: (i + off, 0)),
                  _mod_spec(tm, off, dims)],
        out_specs=pl.BlockSpec((tm, D), lambda i: (i + off, 0)),
        out_shape=jax.ShapeDtypeStruct((M, D), F32),
        input_output_aliases={2: 0},
        compiler_params=_cparams(("parallel",)),
        name="matmul_residual",
    )(a, w, xs, gate)


def _ffn_kernel(x_ref, g_ref, sh_ref, sc_ref, gate_ref, w1_ref, w3_ref, w2_ref, o_ref, a_ref, *, tf):
    x = x_ref[...]
    h = _prenorm(x, g_ref[...], sh_ref[0], sc_ref[0]).astype(BF16)
    F = w1_ref.shape[1]
    for f0 in range(0, F, tf):
        gg = jnp.dot(h, w1_ref[:, f0:f0 + tf], preferred_element_type=F32)
        uu = jnp.dot(h, w3_ref[:, f0:f0 + tf], preferred_element_type=F32)
        a_ref[:, f0:f0 + tf] = (_silu(gg) * uu).astype(BF16)
    y = jnp.dot(a_ref[...], w2_ref[...], preferred_element_type=F32)
    o_ref[...] = x + gate_ref[0] * y


def ffn_sublayer(xs, g, shift, scale, gate, w1, w3, w2, dims, row_start=0, tm=512, tf=256):
    M, D = xs.shape
    F = w1.shape[1]
    off = row_start // tm
    resident = dict(pipeline_mode=pl.Buffered(1))
    return pl.pallas_call(
        functools.partial(_ffn_kernel, tf=tf),
        grid=(M // tm - off,),
        in_specs=[pl.BlockSpec((tm, D), lambda i: (i + off, 0)),
                  pl.BlockSpec((1, D), lambda i: (0, 0)),
                  _mod_spec(tm, off, dims), _mod_spec(tm, off, dims), _mod_spec(tm, off, dims),
                  pl.BlockSpec((D, F), lambda i: (0, 0), **resident),
                  pl.BlockSpec((D, F), lambda i: (0, 0), **resident),
                  pl.BlockSpec((F, D), lambda i: (0, 0), **resident)],
        out_specs=pl.BlockSpec((tm, D), lambda i: (i + off, 0)),
        out_shape=jax.ShapeDtypeStruct((M, D), F32),
        scratch_shapes=[pltpu.VMEM((tm, F), BF16)],
        input_output_aliases={0: 0},
        compiler_params=_cparams(("parallel",)),
        name="ffn_sublayer",
    )(xs, g.reshape(1, D), shift, scale, gate, w1, w3, w2)


def _na_bias_table(rpb):
    H = rpb.shape[0]
    col = np.arange(GRID_W)
    c0 = np.clip(col - NA_COLS // 2, 0, GRID_W - NA_COLS)
    in_win = (col[None, :] >= c0[:, None]) & (col[None, :] < c0[:, None] + NA_COLS)
    dc = np.clip(col[None, :] - col[:, None], -(NA_COLS - 1), NA_COLS - 1) + NA_COLS - 1
    var = np.arange(NA_ROWS)[:, None]
    t = np.arange(NA_ROWS)[None, :]
    dr = t - var + NA_ROWS - 1
    tab = rpb[:, dr][:, :, :, dc]
    tab = jnp.where(jnp.asarray(in_win)[None, None, None], tab, NEG_INF)
    tab = tab.reshape(H // 2, 2, NA_ROWS, NA_ROWS, GRID_W, GRID_W)
    tab = jnp.transpose(tab, (0, 2, 3, 5, 1, 4))
    return tab.reshape(H // 2, NA_ROWS, NA_ROWS * GRID_W, 2 * GRID_W)


def _na_kernel(q_ref, k_ref, v_ref, kc_ref, vc_ref, bias_ref, o_ref, *, rows, rows_per_iter):
    rg = pl.program_id(2)
    W = GRID_W
    KW = NA_ROWS * W
    sub = lax.broadcasted_iota(jnp.int32, (2 * W, LANE), 0)
    lane = lax.broadcasted_iota(jnp.int32, (2 * W, LANE), 1)
    same_head = (sub < W) == (lane < W)
    lane_h = lax.broadcasted_iota(jnp.int32, (W, LANE), 1)
    kc = kc_ref[...]
    vc = vc_ref[...]
    scale = jnp.asarray(NA_HEAD_DIM ** -0.5, BF16)

    def body(g, carry):
        rs = [g * rows_per_iter + t for t in range(rows_per_iter)]
        qoff, kbase, var, s_loc, s_ctx = [], [], [], [], []
        for r in rs:
            row = rg * NA_ROWS + r
            r0 = jnp.clip(row - NA_ROWS // 2, 0, rows - NA_ROWS)
            var.append(row - r0)
            qoff.append(pl.multiple_of(r * W, W))
            kbase.append(pl.multiple_of(r0 * W, W))
            qr = q_ref[pl.ds(qoff[-1], W), :]
            q2 = jnp.concatenate([qr, qr], axis=0)
            q2 = jnp.where(same_head, q2, jnp.zeros_like(q2)) * scale
            kw = k_ref[pl.ds(kbase[-1], KW), :]
            s_loc.append(lax.dot_general(kw, q2, NT, preferred_element_type=F32))
            s_ctx.append(lax.dot_general(kc, q2, NT, preferred_element_type=F32))
        p_loc, p_ctx = [], []
        for t in range(rows_per_iter):
            sl = s_loc[t] + bias_ref[0, var[t]]
            m = jnp.maximum(jnp.max(sl, axis=0, keepdims=True), jnp.max(s_ctx[t], axis=0, keepdims=True))
            pl_ = jnp.exp(sl - m)
            pc_ = jnp.exp(s_ctx[t] - m)
            inv = 1.0 / (jnp.sum(pl_, axis=0, keepdims=True) + jnp.sum(pc_, axis=0, keepdims=True))
            p_loc.append((pl_ * inv).astype(BF16))
            p_ctx.append((pc_ * inv).astype(BF16))
        for t in range(rows_per_iter):
            vw = v_ref[pl.ds(kbase[t], KW), :]
            o = (lax.dot_general(p_loc[t], vw, TN, preferred_element_type=F32)
                 + lax.dot_general(p_ctx[t], vc, TN, preferred_element_type=F32))
            o_sel = jnp.where(lane_h < W, o[:W], o[W:])
            o_ref[pl.ds(qoff[t], W), :] = o_sel.astype(o_ref.dtype)
        return carry

    lax.fori_loop(0, NA_ROWS // rows_per_iter, body, 0)


def na_attention(qkv, bias_tab, dims, L):
    BL, N, B, D = dims
    rows = N // GRID_W
    HP = NA_HEADS // 2
    tq = NA_ROWS * GRID_W
    assert BL % N == 0 and N % tq == 0 and rows >= NA_ROWS
    return pl.pallas_call(
        functools.partial(_na_kernel, rows=rows, rows_per_iter=8),
        grid=(B, HP, rows // NA_ROWS),
        in_specs=[pl.BlockSpec((tq, LANE), lambda b, hp, rg: ((BL + b * N) // tq + rg, hp)),
                  pl.BlockSpec((N, LANE), lambda b, hp, rg: (BL // N + b, HP + hp)),
                  pl.BlockSpec((N, LANE), lambda b, hp, rg: (BL // N + b, 2 * HP + hp)),
                  pl.BlockSpec((L, LANE), lambda b, hp, rg: (b, HP + hp)),
                  pl.BlockSpec((L, LANE), lambda b, hp, rg: (b, 2 * HP + hp)),
                  pl.BlockSpec((1, NA_ROWS, tq, LANE), lambda b, hp, rg: (hp, 0, 0, 0))],
        out_specs=pl.BlockSpec((tq, LANE), lambda b, hp, rg: ((b * N) // tq + rg, hp)),
        out_shape=jax.ShapeDtypeStruct((B * N, D), BF16),
        compiler_params=_cparams(("parallel", "parallel", "arbitrary")),
        name="na_attention",
    )(qkv, qkv, qkv, qkv, qkv, bias_tab)


def _ctx_attn_kernel(q_ref, k_ref, v_ref, o_ref):
    q = q_ref[...]
    k = k_ref[...]
    v = v_ref[...]
    lane = lax.broadcasted_iota(jnp.int32, q.shape, 1)
    scale = jnp.asarray(NA_HEAD_DIM ** -0.5, BF16)
    outs = []
    for hh in range(2):
        msk = (lane < GRID_W) if hh == 0 else (lane >= GRID_W)
        qh = jnp.where(msk, q, jnp.zeros_like(q)) * scale
        s = lax.dot_general(qh, k, NT, preferred_element_type=F32)
        m = jnp.max(s, axis=-1, keepdims=True)
        p = jnp.exp(s - m)
        p = (p / jnp.sum(p, axis=-1, keepdims=True)).astype(BF16)
        outs.append(jnp.dot(p, v, preferred_element_type=F32))
    o_ref[...] = jnp.where(lane < GRID_W, outs[0], outs[1]).astype(o_ref.dtype)


def ctx_attention(qkv, dims, L):
    BL, N, B, D = dims
    HP = NA_HEADS // 2
    return pl.pallas_call(
        _ctx_attn_kernel,
        grid=(B, HP),
        in_specs=[pl.BlockSpec((L, LANE), lambda b, hp: (b, hp)),
                  pl.BlockSpec((L, LANE), lambda b, hp: (b, HP + hp)),
                  pl.BlockSpec((L, LANE), lambda b, hp: (b, 2 * HP + hp))],
        out_specs=pl.BlockSpec((L, LANE), lambda b, hp: (b, hp)),
        out_shape=jax.ShapeDtypeStruct((BL, D), BF16),
        compiler_params=_cparams(("parallel", "parallel")),
        name="ctx_attention",
    )(qkv, qkv, qkv)


def _gdn_feat_kernel(cur_ref, prev_ref, next_ref, cw_ref, o_ref, *, tm, BL, L, N):
    i = pl.program_id(0)
    j = pl.program_id(1)
    r0 = i * tm
    in_ctx = r0 < BL
    seg = jnp.where(in_ctx, L, N)
    off = jnp.where(in_ctx, r0, r0 - BL)
    keep_prev = jnp.where((off % seg) == 0, 0.0, 1.0)
    keep_next = jnp.where(((off + tm) % seg) == 0, 0.0, 1.0)
    row = lax.broadcasted_iota(jnp.int32, (tm, LANE), 0)
    qk_scale = jnp.where(j == 0, GDN_DK ** -0.5, 1.0)
    for c in range(cur_ref.shape[1] // LANE):
        sl = slice(c * LANE, (c + 1) * LANE)
        x = cur_ref[:, sl].astype(F32)
        pv = prev_ref[:, sl].astype(F32) * keep_prev
        nx = next_ref[:, sl].astype(F32) * keep_next
        w = cw_ref[:, sl]
        xm1 = jnp.where(row == 0, pv[15:16], pltpu.roll(x, 1, 0))
        xm2 = jnp.where(row == 0, pv[14:15], jnp.where(row == 1, pv[15:16], pltpu.roll(x, 2, 0)))
        xp1 = jnp.where(row == tm - 1, nx[0:1], pltpu.roll(x, tm - 1, 0))
        y = _silu(w[0:1] * xm2 + w[1:2] * xm1 + w[2:3] * x + w[3:4] * xp1)
        nrm = y * lax.rsqrt(jnp.sum(y * y, axis=-1, keepdims=True) + NORM_EPS) * qk_scale
        o_ref[:, sl] = jnp.where(j == 2, y, nrm).astype(o_ref.dtype)


def gdn_features(proj, conv_w, dims, L, tm=256):
    BL, N, B, D = dims
    M = proj.shape[0]
    C = GDN_HEADS * GDN_DK
    hb = tm // 16
    nhb = M // 16
    return pl.pallas_call(
        functools.partial(_gdn_feat_kernel, tm=tm, BL=BL, L=L, N=N),
        grid=(M // tm, 3),
        in_specs=[pl.BlockSpec((tm, C), lambda i, j: (i, j)),
                  pl.BlockSpec((16, C), lambda i, j: (jnp.maximum(i * hb - 1, 0), j)),
                  pl.BlockSpec((16, C), lambda i, j: (jnp.minimum((i + 1) * hb, nhb - 1), j)),
                  pl.BlockSpec((4, C), lambda i, j: (0, j))],
        out_specs=pl.BlockSpec((tm, C), lambda i, j: (i, j)),
        out_shape=jax.ShapeDtypeStruct((M, 3 * C), BF16),
        compiler_params=_cparams(("parallel", "arbitrary")),
        name="gdn_features",
    )(proj, proj, proj, conv_w)


def _gdn_intra_kernel(f_ref, ab_ref, pa_ref, pdt_ref, lm_ref, sm_ref,
                      w_ref, qd_ref, kt_ref, u_ref, aqk_ref, egl_ref, *, tm, cpg):
    d = pl.program_id(1)
    C = GDN_CHUNK
    H = GDN_HEADS
    DK = GDN_DK
    Lm = lm_ref[0]
    incl = Lm > 0.5
    strict = sm_ref[0] > 0.5
    eye = (lax.broadcasted_iota(jnp.int32, (C, C), 0) == lax.broadcasted_iota(jnp.int32, (C, C), 1)).astype(F32)
    lane = lax.broadcasted_iota(jnp.int32, (C, LANE), 1)
    pa = pa_ref[0]
    pdt = pdt_ref[0]

    def mm(a, b):
        return jnp.dot(a.astype(BF16), b.astype(BF16), preferred_element_type=F32)

    def nt(a, b):
        return lax.dot_general(a.astype(BF16), b.astype(BF16), NT, preferred_element_type=F32)

    def chunk_group(cg, carry):
        pairs = [(cc, h) for cc in range(cpg) for h in range(H)]
        rows = [pl.multiple_of((cg * cpg + cc) * C, C) for cc in range(cpg)]
        gc_all, gc_t, beta_all, g_last = [], [], [], []
        for cc in range(cpg):
            ab = ab_ref[pl.ds(rows[cc], C), :]
            sp = jnp.maximum(ab + pdt, 0.0) + jnp.log1p(jnp.exp(-jnp.abs(ab + pdt)))
            gval = jnp.where(lane < H, pa * sp, 0.0)
            ga = jnp.dot(Lm, gval, precision=HI, preferred_element_type=F32)
            gc_all.append(ga)
            gc_t.append(ga.T)
            beta_all.append(jax.nn.sigmoid(ab))
            g_last.append(jnp.where(d == 0, ga[C - 1:C], ga[0:1]))
        ld = lambda cc, col: f_ref[pl.ds(rows[cc], C), col * DK:(col + 1) * DK]
        q16 = [ld(cc, h) for cc, h in pairs]
        k16 = [ld(cc, H + h) for cc, h in pairs]
        k = [t.astype(F32) for t in k16]
        gc = [jnp.broadcast_to(gc_all[cc][:, h:h + 1], (C, DK)) for cc, h in pairs]
        beta = [jnp.broadcast_to(beta_all[cc][:, H + h:H + h + 1], (C, DK)) for cc, h in pairs]
        decay = []
        for i, (cc, h) in enumerate(pairs):
            diff = gc[i][:, :C] - gc_t[cc][h:h + 1, :]
            decay.append(jnp.where(incl, jnp.exp(jnp.where(incl, diff, 0.0)), 0.0))
        kb = [k[i] * beta[i] for i in range(len(pairs))]
        nm = [jnp.where(strict, nt(kb[i], k16[i]) * decay[i], 0.0) for i in range(len(pairs))]
        xinv = [eye - t for t in nm]
        pw = [mm(t, t) for t in nm]
        for it in range(5):
            xinv = [xinv[i] + mm(xinv[i], pw[i]) for i in range(len(pairs))]
            if it < 4:
                pw = [mm(t, t) for t in pw]
        eg = [jnp.exp(t) for t in gc]
        u = [mm(xinv[i], ld(cc, 2 * H + h).astype(F32) * beta[i]) for i, (cc, h) in enumerate(pairs)]
        w = [mm(xinv[i], kb[i] * eg[i]) for i in range(len(pairs))]
        aqk = [nt(q16[i], k16[i]) * decay[i] for i in range(len(pairs))]
        for i, (cc, h) in enumerate(pairs):
            sl = slice(h * DK, (h + 1) * DK)
            r = rows[cc]
            gl = jnp.broadcast_to(g_last[cc][:, h:h + 1], (1, DK))
            w_ref[0, pl.ds(r, C), sl] = w[i].astype(w_ref.dtype)
            u_ref[0, pl.ds(r, C), sl] = u[i]
            qd_ref[0, pl.ds(r, C), sl] = (q16[i].astype(F32) * eg[i]).astype(qd_ref.dtype)
            kt_ref[0, pl.ds(r, C), sl] = (k[i] * jnp.exp(gl - gc[i])).astype(kt_ref.dtype)
            aqk_ref[0, pl.ds(r, C), h * C:(h + 1) * C] = aqk[i].astype(aqk_ref.dtype)
            egl_ref[0, cg * cpg + cc, h:h + 1, :] = jnp.exp(gl)
        return carry

    lax.fori_loop(0, tm // (C * cpg), chunk_group, 0)


def gdn_intra(feats, ab, a_log, dt_bias, tm=256):
    M = feats.shape[0]
    H, DK, C = GDN_HEADS, GDN_DK, GDN_CHUNK
    HD = H * DK
    pa = jnp.zeros((2, 1, LANE), F32).at[:, 0, :H].set(-jnp.exp(a_log.astype(F32)))
    pdt = jnp.zeros((2, 1, LANE), F32).at[:, 0, :H].set(dt_bias.astype(F32))
    idx = np.arange(C)
    lower = idx[:, None] >= idx[None, :]
    lm = jnp.asarray(np.stack([lower, lower.T]).astype(np.float32))
    sm = jnp.asarray(np.stack([idx[:, None] > idx[None, :], idx[:, None] < idx[None, :]]).astype(np.float32))
    big = lambda dt: jax.ShapeDtypeStruct((2, M, HD), dt)
    dspec = pl.BlockSpec((1, tm, HD), lambda i, d: (d, i, 0))
    return pl.pallas_call(
        functools.partial(_gdn_intra_kernel, tm=tm, cpg=4),
        grid=(M // tm, 2),
        in_specs=[pl.BlockSpec((tm, 3 * HD), lambda i, d: (i, 0)),
                  pl.BlockSpec((tm, LANE), lambda i, d: (i, d)),
                  pl.BlockSpec((1, 1, LANE), lambda i, d: (d, 0, 0)),
                  pl.BlockSpec((1, 1, LANE), lambda i, d: (d, 0, 0)),
                  pl.BlockSpec((1, C, C), lambda i, d: (d, 0, 0)),
                  pl.BlockSpec((1, C, C), lambda i, d: (d, 0, 0))],
        out_specs=[dspec, dspec, dspec, dspec,
                   pl.BlockSpec((1, tm, H * C), lambda i, d: (d, i, 0)),
                   pl.BlockSpec((1, tm // C, H, LANE), lambda i, d: (d, i, 0, 0))],
        out_shape=[big(BF16), big(BF16), big(BF16), big(F32),
                   jax.ShapeDtypeStruct((2, M, H * C), BF16),
                   jax.ShapeDtypeStruct((2, M // C, H, LANE), F32)],
        compiler_params=_cparams(("parallel", "arbitrary")),
        name="gdn_intra",
    )(feats, ab, pa, pdt, lm, sm)


def _gdn_scan_kernel(w_ref, qd_ref, kt_ref, u_ref, aqk_ref, egl_ref, o_ref, s_ref, *, nch):
    d = pl.program_id(1)
    C = GDN_CHUNK
    DK = GDN_DK

    @pl.when(pl.program_id(2) == 0)
    def _():
        s_ref[...] = jnp.zeros_like(s_ref)

    def chunk(j, carry):
        c = jnp.where(d == 0, j, nch - 1 - j)
        r = pl.multiple_of(c * C, C)
        hs = range(GDN_HEADS)
        sls = [slice(h * DK, (h + 1) * DK) for h in hs]
        S = [s_ref[h] for h in hs]
        Sb = [t.astype(BF16) for t in S]
        ws = [jnp.dot(w_ref[0, pl.ds(r, C), sls[h]], Sb[h], preferred_element_type=F32) for h in hs]
        qs = [jnp.dot(qd_ref[0, pl.ds(r, C), sls[h]], Sb[h], preferred_element_type=F32) for h in hs]
        vb = [(u_ref[0, pl.ds(r, C), sls[h]] - ws[h]).astype(BF16) for h in hs]
        av = [jnp.dot(aqk_ref[0, pl.ds(r, C), h * C:(h + 1) * C], vb[h], preferred_element_type=F32) for h in hs]
        kv = [lax.dot_general(kt_ref[0, pl.ds(r, C), sls[h]], vb[h], TN, preferred_element_type=F32) for h in hs]
        for h in hs:
            o_ref[0, pl.ds(r, C), sls[h]] = qs[h] + av[h]
            s_ref[h] = S[h] * egl_ref[0, c, h:h + 1, :] + kv[h]
        return carry

    lax.fori_loop(0, nch, chunk, 0)


def gdn_scan(w, qd, kt, u, aqk, egl, dims, L):
    BL, N, B, D = dims
    M = w.shape[1]
    H, DK, C = GDN_HEADS, GDN_DK, GDN_CHUNK
    HD = H * DK
    blk = L
    nlat = N // blk
    assert N % blk == 0 and blk % C == 0

    def rb(b, d, s):
        lat = BL // blk + b * nlat + jnp.where(d == 0, s - 1, nlat - s)
        return jnp.where(s == 0, b, lat)

    dspec = pl.BlockSpec((1, blk, HD), lambda b, d, s: (d, rb(b, d, s), 0))
    return pl.pallas_call(
        functools.partial(_gdn_scan_kernel, nch=blk // C),
        grid=(B, 2, 1 + nlat),
        in_specs=[dspec, dspec, dspec, dspec,
                  pl.BlockSpec((1, blk, H * C), lambda b, d, s: (d, rb(b, d, s), 0)),
                  pl.BlockSpec((1, blk // C, H, LANE), lambda b, d, s: (d, rb(b, d, s), 0, 0))],
        out_specs=dspec,
        out_shape=jax.ShapeDtypeStruct((2, M, HD), F32),
        scratch_shapes=[pltpu.VMEM((H, DK, DK), F32)],
        compiler_params=_cparams(("parallel", "parallel", "arbitrary")),
        name="gdn_scan",
    )(w, qd, kt, u, aqk, egl)


def _gdn_out_kernel(of_ref, ob_ref, z_ref, ng_ref, w_ref, x_ref, gate_ref, o_ref, a_ref):
    DK = GDN_DK
    for h in range(GDN_HEADS):
        sl = slice(h * DK, (h + 1) * DK)
        o = of_ref[0, :, sl] + ob_ref[0, :, sl]
        y = o * lax.rsqrt(jnp.mean(o * o, axis=-1, keepdims=True) + NORM_EPS) * ng_ref[...]
        z = z_ref[:, sl].astype(F32)
        a_ref[:, sl] = (y * _silu(z)).astype(BF16)
    y = jnp.dot(a_ref[...], w_ref[...], preferred_element_type=F32)
    o_ref[...] = x_ref[...] + gate_ref[0] * y


def gdn_output(o2, proj, norm_g, w_o, xs, gate, dims, row_start=0, tm=512):
    M, D = xs.shape
    HD = GDN_HEADS * GDN_DK
    off = row_start // tm
    return pl.pallas_call(
        _gdn_out_kernel,
        grid=(M // tm - off,),
        in_specs=[pl.BlockSpec((1, tm, HD), lambda i: (0, i + off, 0)),
                  pl.BlockSpec((1, tm, HD), lambda i: (1, i + off, 0)),
                  pl.BlockSpec((tm, HD), lambda i: (i + off, 3)),
                  pl.BlockSpec((1, GDN_DK), lambda i: (0, 0)),
                  pl.BlockSpec((HD, D), lambda i: (0, 0)),
                  pl.BlockSpec((tm, D), lambda i: (i + off, 0)),
                  _mod_spec(tm, off, dims)],
        out_specs=pl.BlockSpec((tm, D), lambda i: (i + off, 0)),
        out_shape=jax.ShapeDtypeStruct((M, D), F32),
        scratch_shapes=[pltpu.VMEM((tm, HD), BF16)],
        input_output_aliases={5: 0},
        compiler_params=_cparams(("parallel",)),
        name="gdn_output",
    )(o2, o2, proj, norm_g.reshape(1, GDN_DK), w_o, xs, gate)


def _pool_tables(tm, L):
    amats, invs = [], []
    t = np.arange(tm)
    for seg in (L, GRID_W):
        a_v, i_v = [], []
        tl = t % seg
        for win in POOL_WINDOWS:
            lo = np.clip(tl - win // 2, 0, seg)
            hi = np.clip(tl + win // 2, 0, seg)
            same = (t[:, None] // seg) == (t[None, :] // seg)
            a = same & (tl[None, :] >= lo[:, None]) & (tl[None, :] < hi[:, None])
            a_v.append(a.astype(np.float32))
            i_v.append(np.broadcast_to((1.0 / (hi - lo))[:, None], (tm, LANE)).astype(np.float32))
        amats.append(np.stack(a_v))
        invs.append(np.stack(i_v))
    return jnp.asarray(np.stack(amats), BF16), jnp.asarray(np.stack(invs), F32)


def _pool_kernel(x_ref, g_ref, sh_ref, sc_ref, gate_ref, a_ref, ic_ref, pw_ref, ls_ref, o_ref):
    x = x_ref[...]
    h = _prenorm(x, g_ref[...], sh_ref[0], sc_ref[0])
    G = pw_ref.shape[1]
    for gi in range(len(POOL_WINDOWS)):
        sl = slice(gi * G, (gi + 1) * G)
        hg = h[:, sl]
        hi = hg.astype(BF16)
        lo = (hg - hi.astype(F32)).astype(BF16)
        am = a_ref[0, gi]
        wsum = jnp.dot(am, hi, preferred_element_type=F32) + jnp.dot(am, lo, preferred_element_type=F32)
        ic = ic_ref[0, gi]
        mean = wsum * jnp.concatenate([ic] * (G // LANE), axis=-1)
        pooled = (mean - hg).astype(BF16)
        y = jnp.dot(pooled, pw_ref[gi], preferred_element_type=F32) * ls_ref[:, sl]
        o_ref[:, sl] = x[:, sl] + gate_ref[0][:, sl] * y


def pool_sublayer(xs, g, shift, scale, gate, pool_w, ls, dims, L, row_start=0, tm=256):
    BL, N, B, D = dims
    M = xs.shape[0]
    assert tm == L and tm % GRID_W == 0
    amat, inv = _pool_tables(tm, L)
    off = row_start // tm
    nct = BL // tm
    G = D // len(POOL_WINDOWS)
    variant = lambda i: jnp.where(i + off < nct, 0, 1)
    return pl.pallas_call(
        _pool_kernel,
        grid=(M // tm - off,),
        in_specs=[pl.BlockSpec((tm, D), lambda i: (i + off, 0)),
                  pl.BlockSpec((1, D), lambda i: (0, 0)),
                  _mod_spec(tm, off, dims), _mod_spec(tm, off, dims), _mod_spec(tm, off, dims),
                  pl.BlockSpec((1, 4, tm, tm), lambda i: (variant(i), 0, 0, 0)),
                  pl.BlockSpec((1, 4, tm, LANE), lambda i: (variant(i), 0, 0, 0)),
                  pl.BlockSpec((4, G, G), lambda i: (0, 0, 0)),
                  pl.BlockSpec((1, D), lambda i: (0, 0))],
        out_specs=pl.BlockSpec((tm, D), lambda i: (i + off, 0)),
        out_shape=jax.ShapeDtypeStruct((M, D), F32),
        input_output_aliases={0: 0},
        compiler_params=_cparams(("parallel",)),
        name="pool_sublayer",
    )(xs, g.reshape(1, D), shift, scale, gate, amat, inv, pool_w.astype(BF16), ls.reshape(1, D))


def _router_kernel(x_ref, g_ref, sh_ref, sc_ref, wr_ref, h_ref, ti_ref, tp_ref):
    h = _prenorm(x_ref[...], g_ref[...], sh_ref[0], sc_ref[0])
    h_ref[...] = h
    logits = jnp.dot(h, wr_ref[...], precision=HI, preferred_element_type=F32)
    lane = lax.broadcasted_iota(jnp.int32, logits.shape, 1)
    valid = lane < N_EXPERTS
    lg = jnp.where(valid, logits, NEG_INF)
    e = jnp.where(valid, jnp.exp(lg - jnp.max(lg, axis=-1, keepdims=True)), 0.0)
    probs = jnp.where(valid, e / jnp.sum(e, axis=-1, keepdims=True), -1.0)
    p1 = jnp.max(probs, axis=-1, keepdims=True)
    i1 = jnp.min(jnp.where(probs == p1, lane, LANE), axis=-1, keepdims=True)
    rest = jnp.where(lane == i1, -1.0, probs)
    p2 = jnp.max(rest, axis=-1, keepdims=True)
    i2 = jnp.min(jnp.where(rest == p2, lane, LANE), axis=-1, keepdims=True)
    tot = p1 + p2
    tp_ref[...] = jnp.where(lane == 0, p1 / tot, jnp.where(lane == 1, p2 / tot, 0.0))
    ti_ref[...] = jnp.where(lane == 0, i1, jnp.where(lane == 1, i2, 0))


def moe_router(xs, g, shift, scale, w_router, dims, row_start=0, tm=512):
    M, D = xs.shape
    off = row_start // tm
    wr = jnp.zeros((D, LANE), F32).at[:, :N_EXPERTS].set(w_router)
    Mo = M - row_start
    ospec = lambda w: pl.BlockSpec((tm, w), lambda i: (i, 0))
    return pl.pallas_call(
        _router_kernel,
        grid=(Mo // tm,),
        in_specs=[pl.BlockSpec((tm, D), lambda i: (i + off, 0)),
                  pl.BlockSpec((1, D), lambda i: (0, 0)),
                  _mod_spec(tm, off, dims), _mod_spec(tm, off, dims),
                  pl.BlockSpec((D, LANE), lambda i: (0, 0))],
        out_specs=[ospec(D), ospec(LANE), ospec(LANE)],
        out_shape=[jax.ShapeDtypeStruct((Mo, D), F32),
                   jax.ShapeDtypeStruct((Mo, LANE), jnp.int32),
                   jax.ShapeDtypeStruct((Mo, LANE), F32)],
        compiler_params=_cparams(("parallel",)),
        name="moe_router",
    )(xs, g.reshape(1, D), shift, scale, wr)


def _route_positions(top_e):
    n_tok = top_e.shape[0]
    e_flat = top_e.reshape(-1)
    onehot = (e_flat[:, None] == jnp.arange(N_EXPERTS, dtype=jnp.int32)[None, :]).astype(jnp.int32)
    csum = jnp.cumsum(onehot, axis=0)
    rank = jnp.sum(csum * onehot, axis=-1) - 1
    counts = csum[-1]
    padded = ((counts + MOE_BLOCK - 1) // MOE_BLOCK) * MOE_BLOCK
    ends = jnp.cumsum(padded)
    starts = ends - padded
    pos = (jnp.sum(onehot * starts[None, :], axis=-1) + rank).reshape(n_tok, TOP_K)
    n_blocks = -(-(n_tok * TOP_K) // MOE_BLOCK) + N_EXPERTS
    block_e = jnp.minimum(jnp.searchsorted(ends, jnp.arange(n_blocks) * MOE_BLOCK, side='right'),
                          N_EXPERTS - 1).astype(jnp.int32)
    pos = pos.astype(jnp.int32)
    src = jnp.zeros((n_blocks * MOE_BLOCK,), jnp.int32).at[pos.reshape(-1)].set(
        jnp.arange(n_tok * TOP_K, dtype=jnp.int32) // TOP_K, unique_indices=True)
    n_used = (ends[-1] // MOE_BLOCK).astype(jnp.int32).reshape(1)
    return pos, src, block_e, n_used, n_blocks


def _expert_kernel(be_ref, nu_ref, idx_ref, idxn_ref, h_hbm, w1_ref, w3_ref, w2_ref, o_ref,
                   xg_ref, a_ref, sem, *, tf, n_blocks):
    del be_ref
    i = pl.program_id(0)
    n_used = nu_ref[0]
    slot = i % 2
    F = w1_ref.shape[2]
    nchunk = F // tf
    per = -(-MOE_BLOCK // nchunk)

    def row_copy(src_ref, s, t):
        return pltpu.make_async_copy(h_hbm.at[pl.ds(src_ref[0, 0, t], 1), :],
                                     xg_ref.at[s, pl.ds(t, 1), :], sem.at[s])

    def issue_loop(src_ref, s):
        def body(t, carry):
            row_copy(src_ref, s, t).start()
            return carry
        lax.fori_loop(0, MOE_BLOCK, body, 0)

    def wait_block(s):
        pltpu.make_async_copy(h_hbm.at[pl.ds(0, MOE_BLOCK), :], xg_ref.at[s], sem.at[s]).wait()

    @pl.when(i == 0)
    def _():
        issue_loop(idx_ref, 0)

    wait_block(slot)

    @pl.when(i < n_used)
    def _():
        hb = xg_ref[slot].astype(BF16)
        for c in range(nchunk):
            cs = slice(c * tf, (c + 1) * tf)
            gg = jnp.dot(hb, w1_ref[0, :, cs], preferred_element_type=F32)
            uu = jnp.dot(hb, w3_ref[0, :, cs], preferred_element_type=F32)
            a_ref[:, cs] = (_silu(gg) * uu).astype(BF16)
            for t in range(c * per, min((c + 1) * per, MOE_BLOCK)):
                row_copy(idxn_ref, 1 - slot, t).start()
        o_ref[...] = jnp.dot(a_ref[...], w2_ref[0], preferred_element_type=F32)

    @pl.when(i >= n_used)
    def _():
        issue_loop(idxn_ref, 1 - slot)
        o_ref[...] = jnp.zeros_like(o_ref)

    @pl.when(i == n_blocks - 1)
    def _():
        wait_block(1 - slot)


def moe_experts(h, src, block_e, n_used, w1, w3, w2, n_blocks, tf=256):
    D = h.shape[1]
    F = w1.shape[2]
    last = n_blocks - 1
    resident = dict(pipeline_mode=pl.Buffered(1))
    grid_spec = pltpu.PrefetchScalarGridSpec(
        num_scalar_prefetch=2,
        grid=(n_blocks,),
        in_specs=[pl.BlockSpec((1, 1, MOE_BLOCK), lambda i, be, nu: (i, 0, 0), memory_space=pltpu.SMEM),
                  pl.BlockSpec((1, 1, MOE_BLOCK), lambda i, be, nu: (jnp.minimum(i + 1, last), 0, 0),
                               memory_space=pltpu.SMEM),
                  pl.BlockSpec(memory_space=pl.ANY),
                  pl.BlockSpec((1, D, F), lambda i, be, nu: (be[i], 0, 0), **resident),
                  pl.BlockSpec((1, D, F), lambda i, be, nu: (be[i], 0, 0), **resident),
                  pl.BlockSpec((1, F, D), lambda i, be, nu: (be[i], 0, 0), **resident)],
        out_specs=pl.BlockSpec((MOE_BLOCK, D), lambda i, be, nu: (i, 0)),
        scratch_shapes=[pltpu.VMEM((2, MOE_BLOCK, D), F32), pltpu.VMEM((MOE_BLOCK, F), BF16),
                        pltpu.SemaphoreType.DMA((2,))],
    )
    src3 = src.reshape(n_blocks, 1, MOE_BLOCK)
    return pl.pallas_call(
        functools.partial(_expert_kernel, tf=tf, n_blocks=n_blocks),
        grid_spec=grid_spec,
        out_shape=jax.ShapeDtypeStruct((n_blocks * MOE_BLOCK, D), F32),
        compiler_params=_cparams(("arbitrary",)),
        name="moe_experts",
    )(block_e, n_used, src3, src3, h, w1, w3, w2)


def _combine_kernel(pos_ref, posn_ref, ob_hbm, tp_ref, x_ref, gate_ref, o_ref, rows_ref, sem, *, tm, nt):
    i = pl.program_id(0)
    slot = i % 2

    def issue(p_ref, s):
        def body(t, carry):
            for j in range(TOP_K):
                pltpu.make_async_copy(ob_hbm.at[pl.ds(p_ref[0, 0, TOP_K * t + j], 1), :],
                                      rows_ref.at[s, pl.ds(j * tm + t, 1), :], sem.at[s]).start()
            return carry
        lax.fori_loop(0, tm, body, 0, unroll=8)

    @pl.when(i == 0)
    def _():
        issue(pos_ref, 0)

    @pl.when(i + 1 < nt)
    def _():
        issue(posn_ref, 1 - slot)

    pltpu.make_async_copy(ob_hbm.at[pl.ds(0, TOP_K * tm), :], rows_ref.at[slot], sem.at[slot]).wait()
    tp = tp_ref[...]
    y = tp[:, 0:1] * rows_ref[slot, 0:tm, :] + tp[:, 1:2] * rows_ref[slot, tm:2 * tm, :]
    o_ref[...] = x_ref[...] + gate_ref[0] * y


def moe_combine(out_buf, pos, top_p, xs, gate, dims, row_start=0, tm=512):
    M, D = xs.shape
    off = row_start // tm
    nt = (M - row_start) // tm
    pos3 = pos.reshape(nt, 1, TOP_K * tm)
    return pl.pallas_call(
        functools.partial(_combine_kernel, tm=tm, nt=nt),
        grid=(nt,),
        in_specs=[pl.BlockSpec((1, 1, TOP_K * tm), lambda i: (i, 0, 0), memory_space=pltpu.SMEM),
                  pl.BlockSpec((1, 1, TOP_K * tm), lambda i: (jnp.minimum(i + 1, nt - 1), 0, 0),
                               memory_space=pltpu.SMEM),
                  pl.BlockSpec(memory_space=pl.ANY),
                  pl.BlockSpec((tm, LANE), lambda i: (i, 0)),
                  pl.BlockSpec((tm, D), lambda i: (i + off, 0)),
                  _mod_spec(tm, off, dims)],
        out_specs=pl.BlockSpec((tm, D), lambda i: (i + off, 0)),
        out_shape=jax.ShapeDtypeStruct((M, D), F32),
        scratch_shapes=[pltpu.VMEM((2, TOP_K * tm, D), F32), pltpu.SemaphoreType.DMA((2,))],
        input_output_aliases={4: 0},
        compiler_params=_cparams(("arbitrary",)),
        name="moe_combine",
    )(pos3, pos3, out_buf, top_p, xs, gate)


def moe_sublayer(xs, g, shift, scale, gate, w_router, w1, w3, w2, dims, row_start):
    h, top_i, top_p = moe_router(xs, g, shift, scale, w_router, dims, row_start)
    pos, src, block_e, n_used, n_blocks = _route_positions(top_i[:, :TOP_K])
    out_buf = moe_experts(h, src, block_e, n_used, w1, w3, w2, n_blocks)
    return moe_combine(out_buf, pos, top_p, xs, gate, dims, row_start)


def _final_norm_kernel(x_ref, g_ref, o_ref):
    x = x_ref[...]
    o_ref[...] = x * lax.rsqrt(jnp.mean(x * x, axis=-1, keepdims=True) + NORM_EPS) * g_ref[...]


def final_norm(xs, g, row_start, tm=512):
    M, D = xs.shape
    off = row_start // tm
    return pl.pallas_call(
        _final_norm_kernel,
        grid=((M - row_start) // tm,),
        in_specs=[pl.BlockSpec((tm, D), lambda i: (i + off, 0)),
                  pl.BlockSpec((1, D), lambda i: (0, 0))],
        out_specs=pl.BlockSpec((tm, D), lambda i: (i, 0)),
        out_shape=jax.ShapeDtypeStruct((M - row_start, D), F32),
        compiler_params=_cparams(("parallel",)),
        name="final_norm",
    )(xs, g.reshape(1, D))


def na_mixer(xs, g, m, w_qkv, w_o, rpb, dims, L, need_ctx):
    BL, N, B, D = dims
    qkv = prenorm_matmul(xs, g, m[0], m[1], w_qkv.astype(BF16), dims, BF16)
    bias_tab = _na_bias_table(rpb.astype(F32))
    w_o = w_o.astype(BF16)
    if need_ctx:
        xs = matmul_residual(ctx_attention(qkv, dims, L), w_o, xs, m[2], dims, row_start=0)
    return matmul_residual(na_attention(qkv, bias_tab, dims, L), w_o, xs, m[2], dims, row_start=BL)


def gdn_mixer(xs, g, m, w_in, conv_w, a_log, dt_bias, norm_g, w_o, dims, L, need_ctx):
    BL, N, B, D = dims
    H = GDN_HEADS
    HD = H * GDN_DK
    w_main = w_in[:, :4 * HD].astype(BF16)
    w_gate = w_in[:, 4 * HD:].astype(F32)
    w_ab = jnp.zeros((D, 2 * LANE), F32)
    for d in range(2):
        w_ab = w_ab.at[:, d * LANE:d * LANE + H].set(w_gate[:, d * H:(d + 1) * H])
        w_ab = w_ab.at[:, d * LANE + H:d * LANE + 2 * H].set(w_gate[:, (2 + d) * H:(3 + d) * H])
    proj = prenorm_matmul(xs, g, m[0], m[1], w_main, dims, BF16)
    ab = prenorm_matmul(xs, g, m[0], m[1], w_ab, dims, F32, precision=HI)
    feats = gdn_features(proj, conv_w.astype(F32), dims, L)
    w, qd, kt, u, aqk, egl = gdn_intra(feats, ab, a_log, dt_bias)
    o2 = gdn_scan(w, qd, kt, u, aqk, egl, dims, L)
    return gdn_output(o2, proj, norm_g.astype(F32), w_o.astype(BF16), xs, m[2], dims,
                      row_start=0 if need_ctx else BL)


def kernel(x, c, ctx, c_ctx, ada_w, ada_b, norm_g, final_g, na_w_qkv, na_w_o, na_rpb, gdn_w_in, gdn_conv, gdn_a_log, gdn_dt_bias, gdn_norm_g, gdn_w_o, pool_w, pool_scale, ffn_w1, ffn_w3, ffn_w2, moe_router, moe_w1, moe_w3, moe_w2):
    B, N, D = x.shape
    L = ctx.shape[1]
    depth = ada_w.shape[0]
    BL = B * L
    dims = (BL, N, B, D)
    xs = jnp.concatenate([ctx.reshape(BL, D), x.reshape(B * N, D)], axis=0)

    R = -(-(B + 1) // 8) * 8
    cvec = jnp.zeros((R, D), F32).at[:B].set(c).at[B].set(c_ctx)
    mods = ada_table(cvec, ada_w, ada_b)[:, :B + 1].reshape(depth, B + 1, 6, 1, D)

    for i in range(depth):
        last = i == depth - 1
        m = [mods[i, :, k] for k in range(6)]
        j = i // 3
        kind = i % 3
        if kind == 0:
            xs = na_mixer(xs, norm_g[i, 0], m, na_w_qkv[j], na_w_o[j], na_rpb[j], dims, L, not last)
        elif kind == 1:
            xs = gdn_mixer(xs, norm_g[i, 0], m, gdn_w_in[j], gdn_conv[j], gdn_a_log[j], gdn_dt_bias[j],
                           gdn_norm_g[j], gdn_w_o[j], dims, L, not last)
        else:
            xs = pool_sublayer(xs, norm_g[i, 0], m[0], m[1], m[2], pool_w[j], pool_scale[j], dims, L,
                               row_start=0 if not last else BL)
        row_start = BL if last else 0
        f = i // 2
        if i % 2 == 0:
            xs = ffn_sublayer(xs, norm_g[i, 1], m[3], m[4], m[5], ffn_w1[f].astype(BF16),
                              ffn_w3[f].astype(BF16), ffn_w2[f].astype(BF16), dims, row_start=row_start)
        else:
            xs = moe_sublayer(xs, norm_g[i, 1], m[3], m[4], m[5], moe_router[f], moe_w1[f].astype(BF16),
                              moe_w3[f].astype(BF16), moe_w2[f].astype(BF16), dims, row_start)
    return final_norm(xs, final_g, BL).reshape(B, N, D)
```

```python
import functools

import numpy as np
import jax
import jax.numpy as jnp
from jax import lax
from jax.experimental import pallas as pl
from jax.experimental.pallas import tpu as pltpu

F32 = jnp.float32
BF16 = jnp.bfloat16
HI = lax.Precision.HIGHEST

NORM_EPS = 1e-6
NEG_INF = -1e30
LOG2E = 1.4426950408889634
GRID_W = 64
NA_HEADS = 16
NA_HEAD_DIM = 64
NA_ROWS = 8
NA_COLS = 16
GDN_HEADS = 8
GDN_DK = 128
GDN_CHUNK = 64
POOL_WINDOWS = (2, 4, 8, 16)
N_EXPERTS = 8
TOP_K = 2
MOE_BLOCK = 512
LANE = 128
V7X_VMEM_LIMIT = 56 * 1024 * 1024

NT = (((1,), (1,)), ((), ()))
TN = (((0,), (0,)), ((), ()))


def _cparams(sem, vmem=V7X_VMEM_LIMIT):
    return pltpu.CompilerParams(dimension_semantics=sem, vmem_limit_bytes=vmem)


def _silu(v):
    return v * jax.nn.sigmoid(v)


def _prenorm(x, g, shift, scale):
    ms = jnp.mean(x * x, axis=-1, keepdims=True)
    y = x * lax.rsqrt(ms + NORM_EPS) * g
    return y * (1.0 + scale) + shift


def _mod_index(i, tm, BL, N, B):
    nct = BL // tm
    return jnp.where(i < nct, B, (i - nct) // (N // tm))


def _mod_spec(tm, off, dims):
    BL, N, B, D = dims
    return pl.BlockSpec((1, 1, D), lambda i, *_: (_mod_index(i + off, tm, BL, N, B), 0, 0))


def _ada_kernel(c_ref, w_ref, b_ref, o_ref):
    cv = c_ref[...]
    o_ref[0] = jnp.dot(_silu(cv), w_ref[0], precision=HI, preferred_element_type=F32) + b_ref[0]


def ada_table(cvec, ada_w, ada_b):
    depth, D, D6 = ada_w.shape
    R = cvec.shape[0]
    tn = 1536
    return pl.pallas_call(
        _ada_kernel,
        grid=(depth, D6 // tn),
        in_specs=[pl.BlockSpec((R, D), lambda l, j: (0, 0)),
                  pl.BlockSpec((1, D, tn), lambda l, j: (l, 0, j)),
                  pl.BlockSpec((1, 1, tn), lambda l, j: (l, 0, j))],
        out_specs=pl.BlockSpec((1, R, tn), lambda l, j: (l, 0, j)),
        out_shape=jax.ShapeDtypeStruct((depth, R, D6), F32),
        compiler_params=_cparams(("arbitrary", "arbitrary")),
        name="ada_table",
    )(cvec, ada_w, ada_b.reshape(depth, 1, D6))


def _prenorm_matmul_kernel(x_ref, g_ref, sh_ref, sc_ref, w_ref, cs_ref, o_ref):
    h = _prenorm(x_ref[...], g_ref[...], sh_ref[0], sc_ref[0]).astype(BF16)
    y = jnp.dot(h, w_ref[...], preferred_element_type=F32)
    o_ref[...] = (y * cs_ref[...]).astype(o_ref.dtype)


def _prenorm_matmul2_kernel(x_ref, g_ref, sh_ref, sc_ref, w_ref, w2_ref, o_ref, o2_ref):
    h = _prenorm(x_ref[...], g_ref[...], sh_ref[0], sc_ref[0])
    o_ref[...] = jnp.dot(h.astype(BF16), w_ref[...], preferred_element_type=F32).astype(o_ref.dtype)
    o2_ref[...] = jnp.dot(h, w2_ref[...], precision=HI, preferred_element_type=F32)


def prenorm_matmul(xs, g, shift, scale, w, dims, col_scale=None, w_hi=None, tm=512):
    M, D = xs.shape
    Nout = w.shape[1]
    specs = [pl.BlockSpec((tm, D), lambda i: (i, 0)),
             pl.BlockSpec((1, D), lambda i: (0, 0)),
             _mod_spec(tm, 0, dims), _mod_spec(tm, 0, dims),
             pl.BlockSpec((D, Nout), lambda i: (0, 0))]
    ospec = pl.BlockSpec((tm, Nout), lambda i: (i, 0))
    oshape = jax.ShapeDtypeStruct((M, Nout), BF16)
    if w_hi is None:
        cs = jnp.ones((1, Nout), F32) if col_scale is None else col_scale.reshape(1, Nout)
        return pl.pallas_call(
            _prenorm_matmul_kernel,
            grid=(M // tm,),
            in_specs=specs + [pl.BlockSpec((1, Nout), lambda i: (0, 0))],
            out_specs=ospec,
            out_shape=oshape,
            compiler_params=_cparams(("parallel",)),
            name="prenorm_matmul",
        )(xs, g.reshape(1, D), shift, scale, w, cs)
    N2 = w_hi.shape[1]
    return pl.pallas_call(
        _prenorm_matmul2_kernel,
        grid=(M // tm,),
        in_specs=specs + [pl.BlockSpec((D, N2), lambda i: (0, 0))],
        out_specs=[ospec, pl.BlockSpec((tm, N2), lambda i: (i, 0))],
        out_shape=[oshape, jax.ShapeDtypeStruct((M, N2), F32)],
        compiler_params=_cparams(("parallel",)),
        name="prenorm_matmul2",
    )(xs, g.reshape(1, D), shift, scale, w, w_hi)


def _matmul_res_kernel(a_ref, w_ref, x_ref, gate_ref, o_ref):
    y = jnp.dot(a_ref[...], w_ref[...], preferred_element_type=F32)
    o_ref[...] = x_ref[...] + gate_ref[0] * y


def matmul_residual(a, w, xs, gate, dims, row_start=0, tm=512):
    M, D = xs.shape
    K = a.shape[1]
    off = row_start // tm
    return pl.pallas_call(
        _matmul_res_kernel,
        grid=(a.shape[0] // tm,),
        in_specs=[pl.BlockSpec((tm, K), lambda i: (i, 0)),
                  pl.BlockSpec((K, D), lambda i: (0, 0)),
                  pl.BlockSpec((tm, D), lambda i: (i + off, 0)),
                  _mod_spec(tm, off, dims)],
        out_specs=pl.BlockSpec((tm, D), lambda i: (i + off, 0)),
        out_shape=jax.ShapeDtypeStruct((M, D), F32),
        input_output_aliases={2: 0},
        compiler_params=_cparams(("parallel",)),
        name="matmul_residual",
    )(a, w, xs, gate)


def _ffn_kernel(x_ref, g_ref, sh_ref, sc_ref, gate_ref, w1_ref, w3_ref, w2_ref, o_ref, a_ref, *, tf):
    x = x_ref[...]
    h = _prenorm(x, g_ref[...], sh_ref[0], sc_ref[0]).astype(BF16)
    F = w1_ref.shape[1]
    for f0 in range(0, F, tf):
        gg = jnp.dot(h, w1_ref[:, f0:f0 + tf], preferred_element_type=F32)
        uu = jnp.dot(h, w3_ref[:, f0:f0 + tf], preferred_element_type=F32)
        a_ref[:, f0:f0 + tf] = (_silu(gg) * uu).astype(BF16)
    y = jnp.dot(a_ref[...], w2_ref[...], preferred_element_type=F32)
    o_ref[...] = x + gate_ref[0] * y


def ffn_sublayer(xs, g, shift, scale, gate, w1, w3, w2, dims, row_start=0, tm=512, tf=256):
    M, D = xs.shape
    F = w1.shape[1]
    off = row_start // tm
    resident = dict(pipeline_mode=pl.Buffered(1))
    return pl.pallas_call(
        functools.partial(_ffn_kernel, tf=tf),
        grid=(M // tm - off,),
        in_specs=[pl.BlockSpec((tm, D), lambda i: (i + off, 0)),
                  pl.BlockSpec((1, D), lambda i: (0, 0)),
                  _mod_spec(tm, off, dims), _mod_spec(tm, off, dims), _mod_spec(tm, off, dims),
                  pl.BlockSpec((D, F), lambda i: (0, 0), **resident),
                  pl.BlockSpec((D, F), lambda i: (0, 0), **resident),
                  pl.BlockSpec((F, D), lambda i: (0, 0), **resident)],
        out_specs=pl.BlockSpec((tm, D), lambda i: (i + off, 0)),
        out_shape=jax.ShapeDtypeStruct((M, D), F32),
        scratch_shapes=[pltpu.VMEM((tm, F), BF16)],
        input_output_aliases={0: 0},
        compiler_params=_cparams(("parallel",)),
        name="ffn_sublayer",
    )(xs, g.reshape(1, D), shift, scale, gate, w1, w3, w2)


def _na_bias_table(rpb):
    H = rpb.shape[0]
    col = np.arange(GRID_W)
    c0 = np.clip(col - NA_COLS // 2, 0, GRID_W - NA_COLS)
    in_win = (col[None, :] >= c0[:, None]) & (col[None, :] < c0[:, None] + NA_COLS)
    dc = np.clip(col[None, :] - col[:, None], -(NA_COLS - 1), NA_COLS - 1) + NA_COLS - 1
    var = np.arange(NA_ROWS)[:, None]
    t = np.arange(NA_ROWS)[None, :]
    dr = t - var + NA_ROWS - 1
    tab = rpb[:, dr][:, :, :, dc]
    tab = jnp.where(jnp.asarray(in_win)[None, None, None], tab * LOG2E, NEG_INF)
    tab = tab.reshape(H // 2, 2, NA_ROWS, NA_ROWS, GRID_W, GRID_W)
    tab = jnp.transpose(tab, (0, 2, 3, 5, 1, 4))
    return tab.reshape(H // 2, NA_ROWS, NA_ROWS * GRID_W, 2 * GRID_W)


def _na_kernel(q_ref, k_ref, v_ref, kc_ref, vc_ref, bias_ref, o_ref, *, rows):
    rg = pl.program_id(2)
    W = GRID_W
    KW = NA_ROWS * W
    sub = lax.broadcasted_iota(jnp.int32, (2 * W, LANE), 0)
    lane = lax.broadcasted_iota(jnp.int32, (2 * W, LANE), 1)
    same_head = (sub < W) == (lane < W)
    lane_h = lax.broadcasted_iota(jnp.int32, (W, LANE), 1)
    kc = kc_ref[...]
    vc = vc_ref[...]

    kbase, var, scores, probs = {}, {}, {}, {}

    def score_stage(r):
        row = rg * NA_ROWS + r
        r0 = jnp.clip(row - NA_ROWS // 2, 0, rows - NA_ROWS)
        var[r] = row - r0
        kbase[r] = pl.multiple_of(r0 * W, W)
        qr = q_ref[r * W:(r + 1) * W, :]
        q2 = jnp.concatenate([qr, qr], axis=0)
        q2 = jnp.where(same_head, q2, jnp.zeros_like(q2))
        kw = k_ref[pl.ds(kbase[r], KW), :]
        scores[r] = (lax.dot_general(kw, q2, NT, preferred_element_type=F32),
                     lax.dot_general(kc, q2, NT, preferred_element_type=F32))

    def softmax_stage(r):
        s_loc, s_ctx = scores.pop(r)
        s_loc = s_loc + bias_ref[0, var[r]]
        m = jnp.maximum(jnp.max(s_loc, axis=0, keepdims=True), jnp.max(s_ctx, axis=0, keepdims=True))
        p_loc = jnp.exp2(s_loc - m)
        p_ctx = jnp.exp2(s_ctx - m)
        inv = 1.0 / (jnp.sum(p_loc, axis=0, keepdims=True) + jnp.sum(p_ctx, axis=0, keepdims=True))
        probs[r] = ((p_loc * inv).astype(BF16), (p_ctx * inv).astype(BF16))

    def value_stage(r):
        p_loc, p_ctx = probs.pop(r)
        vw = v_ref[pl.ds(kbase[r], KW), :]
        o = (lax.dot_general(p_loc, vw, TN, preferred_element_type=F32)
             + lax.dot_general(p_ctx, vc, TN, preferred_element_type=F32))
        o_sel = jnp.where(lane_h < W, o[:W], o[W:])
        o_ref[r * W:(r + 1) * W, :] = o_sel.astype(o_ref.dtype)

    for stage in (score_stage, softmax_stage, value_stage):
        for r in range(NA_ROWS):
            stage(r)


def na_attention(qkv, bias_tab, dims, L):
    BL, N, B, D = dims
    rows = N // GRID_W
    HP = NA_HEADS // 2
    tq = NA_ROWS * GRID_W
    assert BL % N == 0 and N % tq == 0 and rows >= NA_ROWS
    return pl.pallas_call(
        functools.partial(_na_kernel, rows=rows),
        grid=(B, HP, rows // NA_ROWS),
        in_specs=[pl.BlockSpec((tq, LANE), lambda b, hp, rg: ((BL + b * N) // tq + rg, hp)),
                  pl.BlockSpec((N, LANE), lambda b, hp, rg: (BL // N + b, HP + hp)),
                  pl.BlockSpec((N, LANE), lambda b, hp, rg: (BL // N + b, 2 * HP + hp)),
                  pl.BlockSpec((L, LANE), lambda b, hp, rg: (b, HP + hp)),
                  pl.BlockSpec((L, LANE), lambda b, hp, rg: (b, 2 * HP + hp)),
                  pl.BlockSpec((1, NA_ROWS, tq, LANE), lambda b, hp, rg: (hp, 0, 0, 0))],
        out_specs=pl.BlockSpec((tq, LANE), lambda b, hp, rg: ((b * N) // tq + rg, hp)),
        out_shape=jax.ShapeDtypeStruct((B * N, D), BF16),
        compiler_params=_cparams(("parallel", "parallel", "arbitrary")),
        name="na_attention",
    )(qkv, qkv, qkv, qkv, qkv, bias_tab)


def _ctx_attn_kernel(q_ref, k_ref, v_ref, o_ref):
    q = q_ref[...]
    k = k_ref[...]
    v = v_ref[...]
    lane = lax.broadcasted_iota(jnp.int32, q.shape, 1)
    outs = []
    for hh in range(2):
        msk = (lane < GRID_W) if hh == 0 else (lane >= GRID_W)
        qh = jnp.where(msk, q, jnp.zeros_like(q))
        s = lax.dot_general(qh, k, NT, preferred_element_type=F32)
        m = jnp.max(s, axis=-1, keepdims=True)
        p = jnp.exp2(s - m)
        p = (p / jnp.sum(p, axis=-1, keepdims=True)).astype(BF16)
        outs.append(jnp.dot(p, v, preferred_element_type=F32))
    o_ref[...] = jnp.where(lane < GRID_W, outs[0], outs[1]).astype(o_ref.dtype)


def ctx_attention(qkv, dims, L):
    BL, N, B, D = dims
    HP = NA_HEADS // 2
    return pl.pallas_call(
        _ctx_attn_kernel,
        grid=(B, HP),
        in_specs=[pl.BlockSpec((L, LANE), lambda b, hp: (b, hp)),
                  pl.BlockSpec((L, LANE), lambda b, hp: (b, HP + hp)),
                  pl.BlockSpec((L, LANE), lambda b, hp: (b, 2 * HP + hp))],
        out_specs=pl.BlockSpec((L, LANE), lambda b, hp: (b, hp)),
        out_shape=jax.ShapeDtypeStruct((BL, D), BF16),
        compiler_params=_cparams(("parallel", "parallel")),
        name="ctx_attention",
    )(qkv, qkv, qkv)


def _gdn_feat_kernel(cur_ref, prev_ref, next_ref, cw_ref, o_ref, *, tm, BL, L, N):
    i = pl.program_id(0)
    j = pl.program_id(1)
    r0 = i * tm
    in_ctx = r0 < BL
    seg = jnp.where(in_ctx, L, N)
    off = jnp.where(in_ctx, r0, r0 - BL)
    keep_prev = jnp.where((off % seg) == 0, 0.0, 1.0)
    keep_next = jnp.where(((off + tm) % seg) == 0, 0.0, 1.0)
    row = lax.broadcasted_iota(jnp.int32, (tm, LANE), 0)
    qk_scale = jnp.where(j == 0, GDN_DK ** -0.5, 1.0)
    for c in range(cur_ref.shape[1] // LANE):
        sl = slice(c * LANE, (c + 1) * LANE)
        x = cur_ref[:, sl].astype(F32)
        pv = prev_ref[:, sl].astype(F32) * keep_prev
        nx = next_ref[:, sl].astype(F32) * keep_next
        w = cw_ref[:, sl]
        xm1 = jnp.where(row == 0, pv[15:16], pltpu.roll(x, 1, 0))
        xm2 = jnp.where(row == 0, pv[14:15], jnp.where(row == 1, pv[15:16], pltpu.roll(x, 2, 0)))
        xp1 = jnp.where(row == tm - 1, nx[0:1], pltpu.roll(x, tm - 1, 0))
        y = _silu(w[0:1] * xm2 + w[1:2] * xm1 + w[2:3] * x + w[3:4] * xp1)
        nrm = y * lax.rsqrt(jnp.sum(y * y, axis=-1, keepdims=True) + NORM_EPS) * qk_scale
        o_ref[:, sl] = jnp.where(j == 2, y, nrm).astype(o_ref.dtype)


def gdn_features(proj, conv_w, dims, L, tm=256):
    BL, N, B, D = dims
    M = proj.shape[0]
    C = GDN_HEADS * GDN_DK
    hb = tm // 16
    nhb = M // 16
    return pl.pallas_call(
        functools.partial(_gdn_feat_kernel, tm=tm, BL=BL, L=L, N=N),
        grid=(M // tm, 3),
        in_specs=[pl.BlockSpec((tm, C), lambda i, j: (i, j)),
                  pl.BlockSpec((16, C), lambda i, j: (jnp.maximum(i * hb - 1, 0), j)),
                  pl.BlockSpec((16, C), lambda i, j: (jnp.minimum((i + 1) * hb, nhb - 1), j)),
                  pl.BlockSpec((4, C), lambda i, j: (0, j))],
        out_specs=pl.BlockSpec((tm, C), lambda i, j: (i, j)),
        out_shape=jax.ShapeDtypeStruct((M, 3 * C), BF16),
        compiler_params=_cparams(("parallel", "arbitrary")),
        name="gdn_features",
    )(proj, proj, proj, conv_w)


def _gdn_intra_kernel(f_ref, ab_ref, pa_ref, pdt_ref, lm_ref, sm_ref,
                      w_ref, qd_ref, kt_ref, u_ref, aqk_ref, egl_ref, *, tm, cpg):
    d = pl.program_id(1)
    C = GDN_CHUNK
    H = GDN_HEADS
    DK = GDN_DK
    Lm = lm_ref[0]
    incl = Lm > 0.5
    strict = sm_ref[0] > 0.5
    eye = (lax.broadcasted_iota(jnp.int32, (C, C), 0) == lax.broadcasted_iota(jnp.int32, (C, C), 1)).astype(F32)
    lane = lax.broadcasted_iota(jnp.int32, (C, LANE), 1)
    pa = pa_ref[0]
    pdt = pdt_ref[0]

    def mm(a, b):
        return jnp.dot(a.astype(BF16), b.astype(BF16), preferred_element_type=F32)

    def nt(a, b):
        return lax.dot_general(a.astype(BF16), b.astype(BF16), NT, preferred_element_type=F32)

    def chunk_group(cg, carry):
        pairs = [(cc, h) for cc in range(cpg) for h in range(H)]
        rows = [pl.multiple_of((cg * cpg + cc) * C, C) for cc in range(cpg)]
        gc_all, gc_t, beta_all, g_last = [], [], [], []
        for cc in range(cpg):
            ab = ab_ref[pl.ds(rows[cc], C), :]
            sp = jnp.maximum(ab + pdt, 0.0) + jnp.log1p(jnp.exp(-jnp.abs(ab + pdt)))
            gval = jnp.where(lane < H, pa * sp, 0.0)
            ga = jnp.dot(Lm, gval, precision=HI, preferred_element_type=F32)
            gc_all.append(ga)
            gc_t.append(ga.T)
            beta_all.append(jax.nn.sigmoid(ab))
            g_last.append(jnp.where(d == 0, ga[C - 1:C], ga[0:1]))
        ld = lambda cc, col: f_ref[pl.ds(rows[cc], C), col * DK:(col + 1) * DK]
        q16 = [ld(cc, h) for cc, h in pairs]
        k16 = [ld(cc, H + h) for cc, h in pairs]
        k = [t.astype(F32) for t in k16]
        gc = [jnp.broadcast_to(gc_all[cc][:, h:h + 1], (C, DK)) for cc, h in pairs]
        beta = [jnp.broadcast_to(beta_all[cc][:, H + h:H + h + 1], (C, DK)) for cc, h in pairs]
        decay = []
        for i, (cc, h) in enumerate(pairs):
            diff = gc[i][:, :C] - gc_t[cc][h:h + 1, :]
            decay.append(jnp.where(incl, jnp.exp(jnp.where(incl, diff, 0.0)), 0.0))
        kb = [k[i] * beta[i] for i in range(len(pairs))]
        nm = [jnp.where(strict, nt(kb[i], k16[i]) * decay[i], 0.0) for i in range(len(pairs))]
        xinv = [eye - t for t in nm]
        pw = [mm(t, t) for t in nm]
        for it in range(5):
            xinv = [xinv[i] + mm(xinv[i], pw[i]) for i in range(len(pairs))]
            if it < 4:
                pw = [mm(t, t) for t in pw]
        eg = [jnp.exp(t) for t in gc]
        u = [mm(xinv[i], ld(cc, 2 * H + h).astype(F32) * beta[i]) for i, (cc, h) in enumerate(pairs)]
        w = [mm(xinv[i], kb[i] * eg[i]) for i in range(len(pairs))]
        aqk = [nt(q16[i], k16[i]) * decay[i] for i in range(len(pairs))]
        for i, (cc, h) in enumerate(pairs):
            sl = slice(h * DK, (h + 1) * DK)
            r = rows[cc]
            gl = jnp.broadcast_to(g_last[cc][:, h:h + 1], (1, DK))
            w_ref[0, pl.ds(r, C), sl] = w[i].astype(w_ref.dtype)
            u_ref[0, pl.ds(r, C), sl] = u[i]
            qd_ref[0, pl.ds(r, C), sl] = (q16[i].astype(F32) * eg[i]).astype(qd_ref.dtype)
            kt_ref[0, pl.ds(r, C), sl] = (k[i] * jnp.exp(gl - gc[i])).astype(kt_ref.dtype)
            aqk_ref[0, pl.ds(r, C), h * C:(h + 1) * C] = aqk[i].astype(aqk_ref.dtype)
            egl_ref[0, cg * cpg + cc, h:h + 1, :] = jnp.exp(gl)
        return carry

    lax.fori_loop(0, tm // (C * cpg), chunk_group, 0)


def gdn_intra(feats, ab, a_log, dt_bias, tm=256):
    M = feats.shape[0]
    H, DK, C = GDN_HEADS, GDN_DK, GDN_CHUNK
    HD = H * DK
    pa = jnp.zeros((2, 1, LANE), F32).at[:, 0, :H].set(-jnp.exp(a_log.astype(F32)))
    pdt = jnp.zeros((2, 1, LANE), F32).at[:, 0, :H].set(dt_bias.astype(F32))
    idx = np.arange(C)
    lower = idx[:, None] >= idx[None, :]
    lm = jnp.asarray(np.stack([lower, lower.T]).astype(np.float32))
    sm = jnp.asarray(np.stack([idx[:, None] > idx[None, :], idx[:, None] < idx[None, :]]).astype(np.float32))
    big = lambda dt: jax.ShapeDtypeStruct((2, M, HD), dt)
    dspec = pl.BlockSpec((1, tm, HD), lambda i, d: (d, i, 0))
    return pl.pallas_call(
        functools.partial(_gdn_intra_kernel, tm=tm, cpg=4),
        grid=(M // tm, 2),
        in_specs=[pl.BlockSpec((tm, 3 * HD), lambda i, d: (i, 0)),
                  pl.BlockSpec((tm, LANE), lambda i, d: (i, d)),
                  pl.BlockSpec((1, 1, LANE), lambda i, d: (d, 0, 0)),
                  pl.BlockSpec((1, 1, LANE), lambda i, d: (d, 0, 0)),
                  pl.BlockSpec((1, C, C), lambda i, d: (d, 0, 0)),
                  pl.BlockSpec((1, C, C), lambda i, d: (d, 0, 0))],
        out_specs=[dspec, dspec, dspec, dspec,
                   pl.BlockSpec((1, tm, H * C), lambda i, d: (d, i, 0)),
                   pl.BlockSpec((1, tm // C, H, LANE), lambda i, d: (d, i, 0, 0))],
        out_shape=[big(BF16), big(BF16), big(BF16), big(F32),
                   jax.ShapeDtypeStruct((2, M, H * C), BF16),
                   jax.ShapeDtypeStruct((2, M // C, H, LANE), F32)],
        compiler_params=_cparams(("parallel", "arbitrary")),
        name="gdn_intra",
    )(feats, ab, pa, pdt, lm, sm)


def _gdn_scan_kernel(*refs, nch):
    ins = (refs[0:6], refs[6:12])
    outs = refs[12:14]
    s_ref = refs[14]
    C = GDN_CHUNK
    DK = GDN_DK
    H = GDN_HEADS

    @pl.when(pl.program_id(1) == 0)
    def _():
        s_ref[...] = jnp.zeros_like(s_ref)

    def chunk(j, carry):
        cs = (j, nch - 1 - j)
        rs = tuple(pl.multiple_of(c * C, C) for c in cs)
        pairs = [(d, h) for d in range(2) for h in range(H)]
        sl = lambda h: slice(h * DK, (h + 1) * DK)
        ld = lambda k, d, h: ins[d][k][0, pl.ds(rs[d], C), sl(h)]
        S = [s_ref[d, h] for d, h in pairs]
        Sb = [t.astype(BF16) for t in S]
        ws = [jnp.dot(ld(0, d, h), Sb[i], preferred_element_type=F32) for i, (d, h) in enumerate(pairs)]
        qs = [jnp.dot(ld(1, d, h), Sb[i], preferred_element_type=F32) for i, (d, h) in enumerate(pairs)]
        vb = [(ld(3, d, h) - ws[i]).astype(BF16) for i, (d, h) in enumerate(pairs)]
        av = [jnp.dot(ins[d][4][0, pl.ds(rs[d], C), h * C:(h + 1) * C], vb[i], preferred_element_type=F32)
              for i, (d, h) in enumerate(pairs)]
        kv = [lax.dot_general(ld(2, d, h), vb[i], TN, preferred_element_type=F32)
              for i, (d, h) in enumerate(pairs)]
        for i, (d, h) in enumerate(pairs):
            outs[d][pl.ds(rs[d], C), sl(h)] = qs[i] + av[i]
            s_ref[d, h] = S[i] * ins[d][5][0, cs[d], h:h + 1, :] + kv[i]
        return carry

    lax.fori_loop(0, nch, chunk, 0)


def gdn_scan(w, qd, kt, u, aqk, egl, dims, L):
    BL, N, B, D = dims
    M = w.shape[1]
    H, DK, C = GDN_HEADS, GDN_DK, GDN_CHUNK
    HD = H * DK
    blk = L
    nlat = N // blk
    assert N % blk == 0 and blk % C == 0

    def rb(d):
        def index(b, s):
            lat = BL // blk + b * nlat + (s - 1 if d == 0 else nlat - s)
            return jnp.where(s == 0, b, lat)
        return index

    in_specs, args = [], []
    for d in range(2):
        r = rb(d)
        big = pl.BlockSpec((1, blk, HD), lambda b, s, r=r, d=d: (d, r(b, s), 0))
        in_specs += [big, big, big, big,
                     pl.BlockSpec((1, blk, H * C), lambda b, s, r=r, d=d: (d, r(b, s), 0)),
                     pl.BlockSpec((1, blk // C, H, LANE), lambda b, s, r=r, d=d: (d, r(b, s), 0, 0))]
        args += [w, qd, kt, u, aqk, egl]
    return pl.pallas_call(
        functools.partial(_gdn_scan_kernel, nch=blk // C),
        grid=(B, 1 + nlat),
        in_specs=in_specs,
        out_specs=[pl.BlockSpec((blk, HD), lambda b, s, r=rb(d): (r(b, s), 0)) for d in range(2)],
        out_shape=[jax.ShapeDtypeStruct((M, HD), F32)] * 2,
        scratch_shapes=[pltpu.VMEM((2, H, DK, DK), F32)],
        compiler_params=_cparams(("parallel", "arbitrary")),
        name="gdn_scan",
    )(*args)


def _gdn_out_kernel(of_ref, ob_ref, z_ref, ng_ref, w_ref, x_ref, gate_ref, o_ref, a_ref):
    DK = GDN_DK
    for h in range(GDN_HEADS):
        sl = slice(h * DK, (h + 1) * DK)
        o = of_ref[:, sl] + ob_ref[:, sl]
        y = o * lax.rsqrt(jnp.mean(o * o, axis=-1, keepdims=True) + NORM_EPS) * ng_ref[...]
        z = z_ref[:, sl].astype(F32)
        a_ref[:, sl] = (y * _silu(z)).astype(BF16)
    y = jnp.dot(a_ref[...], w_ref[...], preferred_element_type=F32)
    o_ref[...] = x_ref[...] + gate_ref[0] * y


def gdn_output(o_f, o_b, proj, norm_g, w_o, xs, gate, dims, row_start=0, tm=512):
    M, D = xs.shape
    HD = GDN_HEADS * GDN_DK
    off = row_start // tm
    return pl.pallas_call(
        _gdn_out_kernel,
        grid=(M // tm - off,),
        in_specs=[pl.BlockSpec((tm, HD), lambda i: (i + off, 0)),
                  pl.BlockSpec((tm, HD), lambda i: (i + off, 0)),
                  pl.BlockSpec((tm, HD), lambda i: (i + off, 3)),
                  pl.BlockSpec((1, GDN_DK), lambda i: (0, 0)),
                  pl.BlockSpec((HD, D), lambda i: (0, 0)),
                  pl.BlockSpec((tm, D), lambda i: (i + off, 0)),
                  _mod_spec(tm, off, dims)],
        out_specs=pl.BlockSpec((tm, D), lambda i: (i + off, 0)),
        out_shape=jax.ShapeDtypeStruct((M, D), F32),
        scratch_shapes=[pltpu.VMEM((tm, HD), BF16)],
        input_output_aliases={5: 0},
        compiler_params=_cparams(("parallel",)),
        name="gdn_output",
    )(o_f, o_b, proj, norm_g.reshape(1, GDN_DK), w_o, xs, gate)


def _pool_tables(tm, L):
    amats, invs = [], []
    t = np.arange(tm)
    for seg in (L, GRID_W):
        a_v, i_v = [], []
        tl = t % seg
        for win in POOL_WINDOWS:
            lo = np.clip(tl - win // 2, 0, seg)
            hi = np.clip(tl + win // 2, 0, seg)
            same = (t[:, None] // seg) == (t[None, :] // seg)
            a = same & (tl[None, :] >= lo[:, None]) & (tl[None, :] < hi[:, None])
            a_v.append(a.astype(np.float32))
            i_v.append(np.broadcast_to((1.0 / (hi - lo))[:, None], (tm, LANE)).astype(np.float32))
        amats.append(np.stack(a_v))
        invs.append(np.stack(i_v))
    return jnp.asarray(np.stack(amats), BF16), jnp.asarray(np.stack(invs), F32)


def _pool_kernel(x_ref, g_ref, sh_ref, sc_ref, gate_ref, a_ref, ic_ref, pw_ref, ls_ref, o_ref):
    x = x_ref[...]
    h = _prenorm(x, g_ref[...], sh_ref[0], sc_ref[0])
    G = pw_ref.shape[1]
    for gi in range(len(POOL_WINDOWS)):
        sl = slice(gi * G, (gi + 1) * G)
        hg = h[:, sl]
        hi = hg.astype(BF16)
        lo = (hg - hi.astype(F32)).astype(BF16)
        am = a_ref[0, gi]
        wsum = jnp.dot(am, hi, preferred_element_type=F32) + jnp.dot(am, lo, preferred_element_type=F32)
        ic = ic_ref[0, gi]
        mean = wsum * jnp.concatenate([ic] * (G // LANE), axis=-1)
        pooled = (mean - hg).astype(BF16)
        y = jnp.dot(pooled, pw_ref[gi], preferred_element_type=F32) * ls_ref[:, sl]
        o_ref[:, sl] = x[:, sl] + gate_ref[0][:, sl] * y


def pool_sublayer(xs, g, shift, scale, gate, pool_w, ls, dims, L, row_start=0, tm=256):
    BL, N, B, D = dims
    M = xs.shape[0]
    assert tm == L and tm % GRID_W == 0
    amat, inv = _pool_tables(tm, L)
    off = row_start // tm
    nct = BL // tm
    G = D // len(POOL_WINDOWS)
    variant = lambda i: jnp.where(i + off < nct, 0, 1)
    return pl.pallas_call(
        _pool_kernel,
        grid=(M // tm - off,),
        in_specs=[pl.BlockSpec((tm, D), lambda i: (i + off, 0)),
                  pl.BlockSpec((1, D), lambda i: (0, 0)),
                  _mod_spec(tm, off, dims), _mod_spec(tm, off, dims), _mod_spec(tm, off, dims),
                  pl.BlockSpec((1, 4, tm, tm), lambda i: (variant(i), 0, 0, 0)),
                  pl.BlockSpec((1, 4, tm, LANE), lambda i: (variant(i), 0, 0, 0)),
                  pl.BlockSpec((4, G, G), lambda i: (0, 0, 0)),
                  pl.BlockSpec((1, D), lambda i: (0, 0))],
        out_specs=pl.BlockSpec((tm, D), lambda i: (i + off, 0)),
        out_shape=jax.ShapeDtypeStruct((M, D), F32),
        input_output_aliases={0: 0},
        compiler_params=_cparams(("parallel",)),
        name="pool_sublayer",
    )(xs, g.reshape(1, D), shift, scale, gate, amat, inv, pool_w.astype(BF16), ls.reshape(1, D))


def _router_kernel(x_ref, g_ref, sh_ref, sc_ref, wr_ref, h_ref, ti_ref, tp_ref):
    h = _prenorm(x_ref[...], g_ref[...], sh_ref[0], sc_ref[0])
    h_ref[...] = h
    logits = jnp.dot(h, wr_ref[...], precision=HI, preferred_element_type=F32)
    lane = lax.broadcasted_iota(jnp.int32, logits.shape, 1)
    valid = lane < N_EXPERTS
    lg = jnp.where(valid, logits, NEG_INF)
    e = jnp.where(valid, jnp.exp(lg - jnp.max(lg, axis=-1, keepdims=True)), 0.0)
    probs = jnp.where(valid, e / jnp.sum(e, axis=-1, keepdims=True), -1.0)
    p1 = jnp.max(probs, axis=-1, keepdims=True)
    i1 = jnp.min(jnp.where(probs == p1, lane, LANE), axis=-1, keepdims=True)
    rest = jnp.where(lane == i1, -1.0, probs)
    p2 = jnp.max(rest, axis=-1, keepdims=True)
    i2 = jnp.min(jnp.where(rest == p2, lane, LANE), axis=-1, keepdims=True)
    tot = p1 + p2
    tp_ref[...] = jnp.where(lane == 0, p1 / tot, jnp.where(lane == 1, p2 / tot, 0.0))
    ti_ref[...] = jnp.where(lane == 0, i1, jnp.where(lane == 1, i2, 0))


def moe_router(xs, g, shift, scale, w_router, dims, row_start=0, tm=512):
    M, D = xs.shape
    off = row_start // tm
    wr = jnp.zeros((D, LANE), F32).at[:, :N_EXPERTS].set(w_router)
    Mo = M - row_start
    ospec = lambda w: pl.BlockSpec((tm, w), lambda i: (i, 0))
    return pl.pallas_call(
        _router_kernel,
        grid=(Mo // tm,),
        in_specs=[pl.BlockSpec((tm, D), lambda i: (i + off, 0)),
                  pl.BlockSpec((1, D), lambda i: (0, 0)),
                  _mod_spec(tm, off, dims), _mod_spec(tm, off, dims),
                  pl.BlockSpec((D, LANE), lambda i: (0, 0))],
        out_specs=[ospec(D), ospec(LANE), ospec(LANE)],
        out_shape=[jax.ShapeDtypeStruct((Mo, D), F32),
                   jax.ShapeDtypeStruct((Mo, LANE), jnp.int32),
                   jax.ShapeDtypeStruct((Mo, LANE), F32)],
        compiler_params=_cparams(("parallel",)),
        name="moe_router",
    )(xs, g.reshape(1, D), shift, scale, wr)


def _route_positions(top_e):
    n_tok = top_e.shape[0]
    e_flat = top_e.reshape(-1)
    onehot = (e_flat[:, None] == jnp.arange(N_EXPERTS, dtype=jnp.int32)[None, :]).astype(jnp.int32)
    csum = jnp.cumsum(onehot, axis=0)
    rank = jnp.sum(csum * onehot, axis=-1) - 1
    counts = csum[-1]
    padded = ((counts + MOE_BLOCK - 1) // MOE_BLOCK) * MOE_BLOCK
    ends = jnp.cumsum(padded)
    starts = ends - padded
    pos = (jnp.sum(onehot * starts[None, :], axis=-1) + rank).reshape(n_tok, TOP_K)
    n_blocks = -(-(n_tok * TOP_K) // MOE_BLOCK) + N_EXPERTS
    block_e = jnp.minimum(jnp.searchsorted(ends, jnp.arange(n_blocks) * MOE_BLOCK, side='right'),
                          N_EXPERTS - 1).astype(jnp.int32)
    pos = pos.astype(jnp.int32)
    cap = n_blocks * MOE_BLOCK
    n_pairs = n_tok * TOP_K
    pair_row = (jnp.arange(TOP_K, dtype=jnp.int32)[None, :] * n_tok
                + jnp.arange(n_tok, dtype=jnp.int32)[:, None]).reshape(-1)
    dst = (n_pairs + jnp.arange(cap, dtype=jnp.int32)).at[pos.reshape(-1)].set(pair_row, unique_indices=True)
    src = jnp.where(dst < n_pairs, dst % n_tok, 0)
    lead = n_pairs + cap + jnp.arange(MOE_BLOCK, dtype=jnp.int32)
    dst = jnp.concatenate([lead, dst])
    n_used = (ends[-1] // MOE_BLOCK).astype(jnp.int32).reshape(1)
    return src, dst, block_e, n_used, n_blocks


def _expert_kernel(be_ref, nu_ref, tokc_ref, tokn_ref, dstp_ref, dstc_ref, h_hbm, w1_ref, w3_ref, w2_ref,
                   y_hbm, xg_ref, ob_ref, a_ref, sem_g, sem_o, *, tf, n_blocks):
    del be_ref
    i = pl.program_id(0)
    n_used = nu_ref[0]
    slot = i % 2
    other = 1 - slot
    F = w1_ref.shape[2]
    nchunk = F // tf
    per = -(-MOE_BLOCK // nchunk)

    def gather_copy(tok_ref, s, t):
        return pltpu.make_async_copy(h_hbm.at[pl.ds(tok_ref[0, 0, t], 1), :],
                                     xg_ref.at[s, pl.ds(t, 1), :], sem_g.at[s])

    def scatter_copy(dst_ref, s, t):
        return pltpu.make_async_copy(ob_ref.at[s, pl.ds(t, 1), :],
                                     y_hbm.at[pl.ds(dst_ref[0, 0, t], 1), :], sem_o.at[s])

    def start_loop(make, ref, s):
        def body(t, carry):
            make(ref, s, t).start()
            return carry
        lax.fori_loop(0, MOE_BLOCK, body, 0, unroll=8)

    def wait_gather(s):
        pltpu.make_async_copy(h_hbm.at[pl.ds(0, MOE_BLOCK), :], xg_ref.at[s], sem_g.at[s]).wait()

    def wait_scatter(s):
        pltpu.make_async_copy(ob_ref.at[s], y_hbm.at[pl.ds(0, MOE_BLOCK), :], sem_o.at[s]).wait()

    @pl.when(i == 0)
    def _():
        ob_ref[...] = jnp.zeros_like(ob_ref)
        start_loop(gather_copy, tokc_ref, 0)

    wait_gather(slot)

    @pl.when(i >= 1)
    def _():
        wait_scatter(slot)

    @pl.when(i < n_used)
    def _():
        hb = xg_ref[slot].astype(BF16)
        for c in range(nchunk):
            cs = slice(c * tf, (c + 1) * tf)
            gg = jnp.dot(hb, w1_ref[0, :, cs], preferred_element_type=F32)
            uu = jnp.dot(hb, w3_ref[0, :, cs], preferred_element_type=F32)
            a_ref[:, cs] = (_silu(gg) * uu).astype(BF16)
            for t in range(c * per, min((c + 1) * per, MOE_BLOCK)):
                gather_copy(tokn_ref, other, t).start()
                scatter_copy(dstp_ref, other, t).start()
        ob_ref[slot] = jnp.dot(a_ref[...], w2_ref[0], preferred_element_type=F32)

    @pl.when(i >= n_used)
    def _():
        start_loop(gather_copy, tokn_ref, other)
        start_loop(scatter_copy, dstp_ref, other)
        ob_ref[slot] = jnp.zeros(ob_ref.shape[1:], F32)

    @pl.when(i == n_blocks - 1)
    def _():
        start_loop(scatter_copy, dstc_ref, slot)
        wait_scatter(other)
        wait_scatter(slot)
        wait_gather(other)


def moe_experts(h, src, dst, block_e, n_used, w1, w3, w2, n_blocks, tf=256):
    n_tok, D = h.shape
    F = w1.shape[2]
    last = n_blocks - 1
    smem = lambda index: pl.BlockSpec((1, 1, MOE_BLOCK), index, memory_space=pltpu.SMEM)
    resident = dict(pipeline_mode=pl.Buffered(1))
    grid_spec = pltpu.PrefetchScalarGridSpec(
        num_scalar_prefetch=2,
        grid=(n_blocks,),
        in_specs=[smem(lambda i, be, nu: (i, 0, 0)),
                  smem(lambda i, be, nu: (jnp.minimum(i + 1, last), 0, 0)),
                  smem(lambda i, be, nu: (i, 0, 0)),
                  smem(lambda i, be, nu: (i + 1, 0, 0)),
                  pl.BlockSpec(memory_space=pl.ANY),
                  pl.BlockSpec((1, D, F), lambda i, be, nu: (be[i], 0, 0), **resident),
                  pl.BlockSpec((1, D, F), lambda i, be, nu: (be[i], 0, 0), **resident),
                  pl.BlockSpec((1, F, D), lambda i, be, nu: (be[i], 0, 0), **resident)],
        out_specs=pl.BlockSpec(memory_space=pl.ANY),
        scratch_shapes=[pltpu.VMEM((2, MOE_BLOCK, D), F32), pltpu.VMEM((2, MOE_BLOCK, D), F32),
                        pltpu.VMEM((MOE_BLOCK, F), BF16),
                        pltpu.SemaphoreType.DMA((2,)), pltpu.SemaphoreType.DMA((2,))],
    )
    src3 = src.reshape(n_blocks, 1, MOE_BLOCK)
    dst3 = dst.reshape(n_blocks + 1, 1, MOE_BLOCK)
    n_rows = n_tok * TOP_K + (n_blocks + 1) * MOE_BLOCK
    return pl.pallas_call(
        functools.partial(_expert_kernel, tf=tf, n_blocks=n_blocks),
        grid_spec=grid_spec,
        out_shape=jax.ShapeDtypeStruct((n_rows, D), F32),
        compiler_params=_cparams(("arbitrary",)),
        name="moe_experts",
    )(block_e, n_used, src3, src3, dst3, dst3, h, w1, w3, w2)


def _combine_kernel(y0_ref, y1_ref, tp_ref, x_ref, gate_ref, o_ref):
    tp = tp_ref[...]
    y = tp[:, 0:1] * y0_ref[...] + tp[:, 1:2] * y1_ref[...]
    o_ref[...] = x_ref[...] + gate_ref[0] * y


def moe_combine(y, top_p, xs, gate, dims, row_start=0, tm=512):
    M, D = xs.shape
    off = row_start // tm
    nt = (M - row_start) // tm
    return pl.pallas_call(
        _combine_kernel,
        grid=(nt,),
        in_specs=[pl.BlockSpec((tm, D), lambda i: (i, 0)),
                  pl.BlockSpec((tm, D), lambda i: (nt + i, 0)),
                  pl.BlockSpec((tm, LANE), lambda i: (i, 0)),
                  pl.BlockSpec((tm, D), lambda i: (i + off, 0)),
                  _mod_spec(tm, off, dims)],
        out_specs=pl.BlockSpec((tm, D), lambda i: (i + off, 0)),
        out_shape=jax.ShapeDtypeStruct((M, D), F32),
        input_output_aliases={3: 0},
        compiler_params=_cparams(("parallel",)),
        name="moe_combine",
    )(y, y, top_p, xs, gate)


def moe_sublayer(xs, g, shift, scale, gate, w_router, w1, w3, w2, dims, row_start):
    h, top_i, top_p = moe_router(xs, g, shift, scale, w_router, dims, row_start)
    src, dst, block_e, n_used, n_blocks = _route_positions(top_i[:, :TOP_K])
    y = moe_experts(h, src, dst, block_e, n_used, w1, w3, w2, n_blocks)
    return moe_combine(y, top_p, xs, gate, dims, row_start)


def _final_norm_kernel(x_ref, g_ref, o_ref):
    x = x_ref[...]
    o_ref[...] = x * lax.rsqrt(jnp.mean(x * x, axis=-1, keepdims=True) + NORM_EPS) * g_ref[...]


def final_norm(xs, g, row_start, tm=512):
    M, D = xs.shape
    off = row_start // tm
    return pl.pallas_call(
        _final_norm_kernel,
        grid=((M - row_start) // tm,),
        in_specs=[pl.BlockSpec((tm, D), lambda i: (i + off, 0)),
                  pl.BlockSpec((1, D), lambda i: (0, 0))],
        out_specs=pl.BlockSpec((tm, D), lambda i: (i, 0)),
        out_shape=jax.ShapeDtypeStruct((M - row_start, D), F32),
        compiler_params=_cparams(("parallel",)),
        name="final_norm",
    )(xs, g.reshape(1, D))


def na_mixer(xs, g, m, w_qkv, w_o, rpb, dims, L, need_ctx):
    BL, N, B, D = dims
    col_scale = jnp.concatenate([jnp.full((D,), NA_HEAD_DIM ** -0.5 * LOG2E, F32), jnp.ones((2 * D,), F32)])
    qkv = prenorm_matmul(xs, g, m[0], m[1], w_qkv.astype(BF16), dims, col_scale=col_scale)
    bias_tab = _na_bias_table(rpb.astype(F32))
    w_o = w_o.astype(BF16)
    if need_ctx:
        xs = matmul_residual(ctx_attention(qkv, dims, L), w_o, xs, m[2], dims, row_start=0)
    return matmul_residual(na_attention(qkv, bias_tab, dims, L), w_o, xs, m[2], dims, row_start=BL)


def gdn_mixer(xs, g, m, w_in, conv_w, a_log, dt_bias, norm_g, w_o, dims, L, need_ctx):
    BL, N, B, D = dims
    H = GDN_HEADS
    HD = H * GDN_DK
    w_main = w_in[:, :4 * HD].astype(BF16)
    w_gate = w_in[:, 4 * HD:].astype(F32)
    w_ab = jnp.zeros((D, 2 * LANE), F32)
    for d in range(2):
        w_ab = w_ab.at[:, d * LANE:d * LANE + H].set(w_gate[:, d * H:(d + 1) * H])
        w_ab = w_ab.at[:, d * LANE + H:d * LANE + 2 * H].set(w_gate[:, (2 + d) * H:(3 + d) * H])
    proj, ab = prenorm_matmul(xs, g, m[0], m[1], w_main, dims, w_hi=w_ab)
    feats = gdn_features(proj, conv_w.astype(F32), dims, L)
    w, qd, kt, u, aqk, egl = gdn_intra(feats, ab, a_log, dt_bias)
    o_f, o_b = gdn_scan(w, qd, kt, u, aqk, egl, dims, L)
    return gdn_output(o_f, o_b, proj, norm_g.astype(F32), w_o.astype(BF16), xs, m[2], dims,
                      row_start=0 if need_ctx else BL)


def kernel(x, c, ctx, c_ctx, ada_w, ada_b, norm_g, final_g, na_w_qkv, na_w_o, na_rpb, gdn_w_in, gdn_conv, gdn_a_log, gdn_dt_bias, gdn_norm_g, gdn_w_o, pool_w, pool_scale, ffn_w1, ffn_w3, ffn_w2, moe_router, moe_w1, moe_w3, moe_w2):
    B, N, D = x.shape
    L = ctx.shape[1]
    depth = ada_w.shape[0]
    BL = B * L
    dims = (BL, N, B, D)
    xs = jnp.concatenate([ctx.reshape(BL, D), x.reshape(B * N, D)], axis=0)

    R = -(-(B + 1) // 8) * 8
    cvec = jnp.zeros((R, D), F32).at[:B].set(c).at[B].set(c_ctx)
    mods = ada_table(cvec, ada_w, ada_b)[:, :B + 1].reshape(depth, B + 1, 6, 1, D)

    for i in range(depth):
        last = i == depth - 1
        m = [mods[i, :, k] for k in range(6)]
        j = i // 3
        kind = i % 3
        if kind == 0:
            xs = na_mixer(xs, norm_g[i, 0], m, na_w_qkv[j], na_w_o[j], na_rpb[j], dims, L, not last)
        elif kind == 1:
            xs = gdn_mixer(xs, norm_g[i, 0], m, gdn_w_in[j], gdn_conv[j], gdn_a_log[j], gdn_dt_bias[j],
                           gdn_norm_g[j], gdn_w_o[j], dims, L, not last)
        else:
            xs = pool_sublayer(xs, norm_g[i, 0], m[0], m[1], m[2], pool_w[j], pool_scale[j], dims, L,
                               row_start=0 if not last else BL)
        row_start = BL if last else 0
        f = i // 2
        if i % 2 == 0:
            xs = ffn_sublayer(xs, norm_g[i, 1], m[3], m[4], m[5], ffn_w1[f].astype(BF16),
                              ffn_w3[f].astype(BF16), ffn_w2[f].astype(BF16), dims, row_start=row_start)
        else:
            xs = moe_sublayer(xs, norm_g[i, 1], m[3], m[4], m[5], moe_router[f], moe_w1[f].astype(BF16),
                              moe_w3[f].astype(BF16), moe_w2[f].astype(BF16), dims, row_start)
    return final_norm(xs, final_g, BL).reshape(B, N, D)
```

```python
import functools

import numpy as np
import jax
import jax.numpy as jnp
from jax import lax
from jax.experimental import pallas as pl
from jax.experimental.pallas import tpu as pltpu

F32 = jnp.float32
BF16 = jnp.bfloat16
HI = lax.Precision.HIGHEST

NORM_EPS = 1e-6
NEG_INF = -1e30
LOG2E = 1.4426950408889634
GRID_W = 64
NA_HEADS = 16
NA_HEAD_DIM = 64
NA_ROWS = 8
NA_COLS = 16
GDN_HEADS = 8
GDN_DK = 128
GDN_CHUNK = 64
POOL_WINDOWS = (2, 4, 8, 16)
N_EXPERTS = 8
TOP_K = 2
MOE_BLOCK = 512
LANE = 128
V7X_VMEM_LIMIT = 56 * 1024 * 1024

NT = (((1,), (1,)), ((), ()))
TN = (((0,), (0,)), ((), ()))


def _cparams(sem, vmem=V7X_VMEM_LIMIT):
    return pltpu.CompilerParams(dimension_semantics=sem, vmem_limit_bytes=vmem)


def _silu(v):
    return v * jax.nn.sigmoid(v)


def _prenorm(x, g, shift, scale):
    ms = jnp.mean(x * x, axis=-1, keepdims=True)
    y = x * lax.rsqrt(ms + NORM_EPS) * g
    return y * (1.0 + scale) + shift


def _mod_index(i, tm, BL, N, B):
    nct = BL // tm
    return jnp.where(i < nct, B, (i - nct) // (N // tm))


def _mod_spec(tm, off, dims):
    BL, N, B, D = dims
    return pl.BlockSpec((1, 1, D), lambda i, *_: (_mod_index(i + off, tm, BL, N, B), 0, 0))


def _ada_kernel(c_ref, w_ref, b_ref, o_ref):
    cv = c_ref[...]
    o_ref[0] = jnp.dot(_silu(cv), w_ref[0], precision=HI, preferred_element_type=F32) + b_ref[0]


def ada_table(cvec, ada_w, ada_b):
    depth, D, D6 = ada_w.shape
    R = cvec.shape[0]
    tn = 1536
    return pl.pallas_call(
        _ada_kernel,
        grid=(depth, D6 // tn),
        in_specs=[pl.BlockSpec((R, D), lambda l, j: (0, 0)),
                  pl.BlockSpec((1, D, tn), lambda l, j: (l, 0, j)),
                  pl.BlockSpec((1, 1, tn), lambda l, j: (l, 0, j))],
        out_specs=pl.BlockSpec((1, R, tn), lambda l, j: (l, 0, j)),
        out_shape=jax.ShapeDtypeStruct((depth, R, D6), F32),
        compiler_params=_cparams(("arbitrary", "arbitrary")),
        name="ada_table",
    )(cvec, ada_w, ada_b.reshape(depth, 1, D6))


def _prenorm_matmul_kernel(x_ref, g_ref, sh_ref, sc_ref, w_ref, cs_ref, o_ref):
    h = _prenorm(x_ref[...], g_ref[...], sh_ref[0], sc_ref[0]).astype(BF16)
    y = jnp.dot(h, w_ref[...], preferred_element_type=F32)
    o_ref[...] = (y * cs_ref[...]).astype(o_ref.dtype)


def _prenorm_matmul2_kernel(x_ref, g_ref, sh_ref, sc_ref, w_ref, w2_ref, o_ref, o2_ref):
    h = _prenorm(x_ref[...], g_ref[...], sh_ref[0], sc_ref[0])
    o_ref[...] = jnp.dot(h.astype(BF16), w_ref[...], preferred_element_type=F32).astype(o_ref.dtype)
    o2_ref[...] = jnp.dot(h, w2_ref[...], precision=HI, preferred_element_type=F32)


def prenorm_matmul(xs, g, shift, scale, w, dims, col_scale=None, w_hi=None, tm=512):
    M, D = xs.shape
    Nout = w.shape[1]
    specs = [pl.BlockSpec((tm, D), lambda i: (i, 0)),
             pl.BlockSpec((1, D), lambda i: (0, 0)),
             _mod_spec(tm, 0, dims), _mod_spec(tm, 0, dims),
             pl.BlockSpec((D, Nout), lambda i: (0, 0))]
    ospec = pl.BlockSpec((tm, Nout), lambda i: (i, 0))
    oshape = jax.ShapeDtypeStruct((M, Nout), BF16)
    if w_hi is None:
        cs = jnp.ones((1, Nout), F32) if col_scale is None else col_scale.reshape(1, Nout)
        return pl.pallas_call(
            _prenorm_matmul_kernel,
            grid=(M // tm,),
            in_specs=specs + [pl.BlockSpec((1, Nout), lambda i: (0, 0))],
            out_specs=ospec,
            out_shape=oshape,
            compiler_params=_cparams(("parallel",)),
            name="prenorm_matmul",
        )(xs, g.reshape(1, D), shift, scale, w, cs)
    N2 = w_hi.shape[1]
    return pl.pallas_call(
        _prenorm_matmul2_kernel,
        grid=(M // tm,),
        in_specs=specs + [pl.BlockSpec((D, N2), lambda i: (0, 0))],
        out_specs=[ospec, pl.BlockSpec((tm, N2), lambda i: (i, 0))],
        out_shape=[oshape, jax.ShapeDtypeStruct((M, N2), F32)],
        compiler_params=_cparams(("parallel",)),
        name="prenorm_matmul2",
    )(xs, g.reshape(1, D), shift, scale, w, w_hi)


def _matmul_res_kernel(a_ref, w_ref, x_ref, gate_ref, o_ref):
    y = jnp.dot(a_ref[...], w_ref[...], preferred_element_type=F32)
    o_ref[...] = x_ref[...] + gate_ref[0] * y


def matmul_residual(a, w, xs, gate, dims, row_start=0, tm=512):
    M, D = xs.shape
    K = a.shape[1]
    off = row_start // tm
    return pl.pallas_call(
        _matmul_res_kernel,
        grid=(a.shape[0] // tm,),
        in_specs=[pl.BlockSpec((tm, K), lambda i: (i, 0)),
                  pl.BlockSpec((K, D), lambda i: (0, 0)),
                  pl.BlockSpec((tm, D), lambda i: (i + off, 0)),
                  _mod_spec(tm, off, dims)],
        out_specs=pl.BlockSpec((tm, D), lambda i: (i + off, 0)),
        out_shape=jax.ShapeDtypeStruct((M, D), F32),
        input_output_aliases={2: 0},
        compiler_params=_cparams(("parallel",)),
        name="matmul_residual",
    )(a, w, xs, gate)


def _ffn_kernel(x_ref, g_ref, sh_ref, sc_ref, gate_ref, w1_ref, w3_ref, w2_ref, o_ref, a_ref, *, tf):
    x = x_ref[...]
    h = _prenorm(x, g_ref[...], sh_ref[0], sc_ref[0]).astype(BF16)
    F = w1_ref.shape[1]
    for f0 in range(0, F, tf):
        gg = jnp.dot(h, w1_ref[:, f0:f0 + tf], preferred_element_type=F32)
        uu = jnp.dot(h, w3_ref[:, f0:f0 + tf], preferred_element_type=F32)
        a_ref[:, f0:f0 + tf] = (_silu(gg) * uu).astype(BF16)
    y = jnp.dot(a_ref[...], w2_ref[...], preferred_element_type=F32)
    o_ref[...] = x + gate_ref[0] * y


def ffn_sublayer(xs, g, shift, scale, gate, w1, w3, w2, dims, row_start=0, tm=512, tf=256):
    M, D = xs.shape
    F = w1.shape[1]
    off = row_start // tm
    resident = dict(pipeline_mode=pl.Buffered(1))
    return pl.pallas_call(
        functools.partial(_ffn_kernel, tf=tf),
        grid=(M // tm - off,),
        in_specs=[pl.BlockSpec((tm, D), lambda i: (i + off, 0)),
                  pl.BlockSpec((1, D), lambda i: (0, 0)),
                  _mod_spec(tm, off, dims), _mod_spec(tm, off, dims), _mod_spec(tm, off, dims),
                  pl.BlockSpec((D, F), lambda i: (0, 0), **resident),
                  pl.BlockSpec((D, F), lambda i: (0, 0), **resident),
                  pl.BlockSpec((F, D), lambda i: (0, 0), **resident)],
        out_specs=pl.BlockSpec((tm, D), lambda i: (i + off, 0)),
        out_shape=jax.ShapeDtypeStruct((M, D), F32),
        scratch_shapes=[pltpu.VMEM((tm, F), BF16)],
        input_output_aliases={0: 0},
        compiler_params=_cparams(("parallel",)),
        name="ffn_sublayer",
    )(xs, g.reshape(1, D), shift, scale, gate, w1, w3, w2)


def _na_bias_table(rpb):
    H = rpb.shape[0]
    col = np.arange(GRID_W)
    c0 = np.clip(col - NA_COLS // 2, 0, GRID_W - NA_COLS)
    in_win = (col[None, :] >= c0[:, None]) & (col[None, :] < c0[:, None] + NA_COLS)
    dc = np.clip(col[None, :] - col[:, None], -(NA_COLS - 1), NA_COLS - 1) + NA_COLS - 1
    var = np.arange(NA_ROWS)[:, None]
    t = np.arange(NA_ROWS)[None, :]
    dr = t - var + NA_ROWS - 1
    tab = rpb[:, dr][:, :, :, dc]
    tab = jnp.where(jnp.asarray(in_win)[None, None, None], tab * LOG2E, NEG_INF)
    tab = tab.reshape(H // 2, 2, NA_ROWS, NA_ROWS, GRID_W, GRID_W)
    tab = jnp.transpose(tab, (0, 2, 3, 5, 1, 4))
    return tab.reshape(H // 2, NA_ROWS, NA_ROWS * GRID_W, 2 * GRID_W)


def _na_kernel(q_ref, k_ref, v_ref, kc_ref, vc_ref, bias_ref, o_ref, *, rows):
    W = GRID_W
    KW = NA_ROWS * W
    sub = lax.broadcasted_iota(jnp.int32, (2 * W, LANE), 0)
    lane = lax.broadcasted_iota(jnp.int32, (2 * W, LANE), 1)
    same_head = (sub < W) == (lane < W)
    lane_h = lax.broadcasted_iota(jnp.int32, (W, LANE), 1)
    kc = kc_ref[...]
    vc = vc_ref[...]

    kbase, qbase, var, scores, probs = {}, {}, {}, {}, {}

    def score_stage(rg, r):
        row = rg * NA_ROWS + r
        r0 = jnp.clip(row - NA_ROWS // 2, 0, rows - NA_ROWS)
        var[r] = row - r0
        kbase[r] = pl.multiple_of(r0 * W, W)
        qbase[r] = pl.multiple_of(row * W, W)
        qr = q_ref[pl.ds(qbase[r], W), :]
        q2 = jnp.concatenate([qr, qr], axis=0)
        q2 = jnp.where(same_head, q2, jnp.zeros_like(q2))
        kw = k_ref[pl.ds(kbase[r], KW), :]
        scores[r] = (lax.dot_general(kw, q2, NT, preferred_element_type=F32),
                     lax.dot_general(kc, q2, NT, preferred_element_type=F32))

    def softmax_stage(rg, r):
        s_loc, s_ctx = scores.pop(r)
        s_loc = s_loc + bias_ref[0, var[r]]
        m = jnp.maximum(jnp.max(s_loc, axis=0, keepdims=True), jnp.max(s_ctx, axis=0, keepdims=True))
        p_loc = jnp.exp2(s_loc - m)
        p_ctx = jnp.exp2(s_ctx - m)
        inv = 1.0 / (jnp.sum(p_loc, axis=0, keepdims=True) + jnp.sum(p_ctx, axis=0, keepdims=True))
        probs[r] = ((p_loc * inv).astype(BF16), (p_ctx * inv).astype(BF16))

    def value_stage(rg, r):
        p_loc, p_ctx = probs.pop(r)
        vw = v_ref[pl.ds(kbase[r], KW), :]
        o = (lax.dot_general(p_loc, vw, TN, preferred_element_type=F32)
             + lax.dot_general(p_ctx, vc, TN, preferred_element_type=F32))
        o_sel = jnp.where(lane_h < W, o[:W], o[W:])
        o_ref[pl.ds(qbase[r], W), :] = o_sel.astype(o_ref.dtype)

    def row_group(rg, carry):
        for stage in (score_stage, softmax_stage, value_stage):
            for r in range(NA_ROWS):
                stage(rg, r)
        return carry

    lax.fori_loop(0, rows // NA_ROWS, row_group, 0)


def na_attention(qkv, bias_tab, dims, L):
    BL, N, B, D = dims
    rows = N // GRID_W
    HP = NA_HEADS // 2
    tq = NA_ROWS * GRID_W
    assert BL % N == 0 and N % tq == 0 and rows >= NA_ROWS
    return pl.pallas_call(
        functools.partial(_na_kernel, rows=rows),
        grid=(HP, B),
        in_specs=[pl.BlockSpec((N, LANE), lambda hp, b: (BL // N + b, hp)),
                  pl.BlockSpec((N, LANE), lambda hp, b: (BL // N + b, HP + hp)),
                  pl.BlockSpec((N, LANE), lambda hp, b: (BL // N + b, 2 * HP + hp)),
                  pl.BlockSpec((L, LANE), lambda hp, b: (b, HP + hp)),
                  pl.BlockSpec((L, LANE), lambda hp, b: (b, 2 * HP + hp)),
                  pl.BlockSpec((1, NA_ROWS, tq, LANE), lambda hp, b: (hp, 0, 0, 0))],
        out_specs=pl.BlockSpec((N, LANE), lambda hp, b: (b, hp)),
        out_shape=jax.ShapeDtypeStruct((B * N, D), BF16),
        compiler_params=_cparams(("parallel", "parallel")),
        name="na_attention",
    )(qkv, qkv, qkv, qkv, qkv, bias_tab)


def _ctx_attn_kernel(q_ref, k_ref, v_ref, o_ref):
    q = q_ref[...]
    k = k_ref[...]
    v = v_ref[...]
    lane = lax.broadcasted_iota(jnp.int32, q.shape, 1)
    outs = []
    for hh in range(2):
        msk = (lane < GRID_W) if hh == 0 else (lane >= GRID_W)
        qh = jnp.where(msk, q, jnp.zeros_like(q))
        s = lax.dot_general(qh, k, NT, preferred_element_type=F32)
        m = jnp.max(s, axis=-1, keepdims=True)
        p = jnp.exp2(s - m)
        p = (p / jnp.sum(p, axis=-1, keepdims=True)).astype(BF16)
        outs.append(jnp.dot(p, v, preferred_element_type=F32))
    o_ref[...] = jnp.where(lane < GRID_W, outs[0], outs[1]).astype(o_ref.dtype)


def ctx_attention(qkv, dims, L):
    BL, N, B, D = dims
    HP = NA_HEADS // 2
    return pl.pallas_call(
        _ctx_attn_kernel,
        grid=(B, HP),
        in_specs=[pl.BlockSpec((L, LANE), lambda b, hp: (b, hp)),
                  pl.BlockSpec((L, LANE), lambda b, hp: (b, HP + hp)),
                  pl.BlockSpec((L, LANE), lambda b, hp: (b, 2 * HP + hp))],
        out_specs=pl.BlockSpec((L, LANE), lambda b, hp: (b, hp)),
        out_shape=jax.ShapeDtypeStruct((BL, D), BF16),
        compiler_params=_cparams(("parallel", "parallel")),
        name="ctx_attention",
    )(qkv, qkv, qkv)


def _gdn_feat_kernel(cur_ref, prev_ref, next_ref, cw_ref, o_ref, *, tm, BL, L, N):
    i = pl.program_id(0)
    j = pl.program_id(1)
    r0 = i * tm
    in_ctx = r0 < BL
    seg = jnp.where(in_ctx, L, N)
    off = jnp.where(in_ctx, r0, r0 - BL)
    keep_prev = jnp.where((off % seg) == 0, 0.0, 1.0)
    keep_next = jnp.where(((off + tm) % seg) == 0, 0.0, 1.0)
    row = lax.broadcasted_iota(jnp.int32, (tm, LANE), 0)
    qk_scale = jnp.where(j == 0, GDN_DK ** -0.5, 1.0)
    for c in range(cur_ref.shape[1] // LANE):
        sl = slice(c * LANE, (c + 1) * LANE)
        x = cur_ref[:, sl].astype(F32)
        pv = prev_ref[:, sl].astype(F32) * keep_prev
        nx = next_ref[:, sl].astype(F32) * keep_next
        w = cw_ref[:, sl]
        xm1 = jnp.where(row == 0, pv[15:16], pltpu.roll(x, 1, 0))
        xm2 = jnp.where(row == 0, pv[14:15], jnp.where(row == 1, pv[15:16], pltpu.roll(x, 2, 0)))
        xp1 = jnp.where(row == tm - 1, nx[0:1], pltpu.roll(x, tm - 1, 0))
        y = _silu(w[0:1] * xm2 + w[1:2] * xm1 + w[2:3] * x + w[3:4] * xp1)
        nrm = y * lax.rsqrt(jnp.sum(y * y, axis=-1, keepdims=True) + NORM_EPS) * qk_scale
        o_ref[:, sl] = jnp.where(j == 2, y, nrm).astype(o_ref.dtype)


def gdn_features(proj, conv_w, dims, L, tm=256):
    BL, N, B, D = dims
    M = proj.shape[0]
    C = GDN_HEADS * GDN_DK
    hb = tm // 16
    nhb = M // 16
    return pl.pallas_call(
        functools.partial(_gdn_feat_kernel, tm=tm, BL=BL, L=L, N=N),
        grid=(M // tm, 3),
        in_specs=[pl.BlockSpec((tm, C), lambda i, j: (i, j)),
                  pl.BlockSpec((16, C), lambda i, j: (jnp.maximum(i * hb - 1, 0), j)),
                  pl.BlockSpec((16, C), lambda i, j: (jnp.minimum((i + 1) * hb, nhb - 1), j)),
                  pl.BlockSpec((4, C), lambda i, j: (0, j))],
        out_specs=pl.BlockSpec((tm, C), lambda i, j: (i, j)),
        out_shape=jax.ShapeDtypeStruct((M, 3 * C), BF16),
        compiler_params=_cparams(("parallel", "arbitrary")),
        name="gdn_features",
    )(proj, proj, proj, conv_w)


def _gdn_intra_kernel(f_ref, ab_ref, pa_ref, pdt_ref, lm_ref, sm_ref,
                      w_ref, qd_ref, kt_ref, u_ref, aqk_ref, egl_ref, *, tm, cpg):
    d = pl.program_id(1)
    C = GDN_CHUNK
    H = GDN_HEADS
    DK = GDN_DK
    Lm = lm_ref[0]
    incl = Lm > 0.5
    strict = sm_ref[0] > 0.5
    eye = (lax.broadcasted_iota(jnp.int32, (C, C), 0) == lax.broadcasted_iota(jnp.int32, (C, C), 1)).astype(F32)
    lane = lax.broadcasted_iota(jnp.int32, (C, LANE), 1)
    pa = pa_ref[0]
    pdt = pdt_ref[0]

    def mm(a, b):
        return jnp.dot(a.astype(BF16), b.astype(BF16), preferred_element_type=F32)

    def nt(a, b):
        return lax.dot_general(a.astype(BF16), b.astype(BF16), NT, preferred_element_type=F32)

    def chunk_group(cg, carry):
        pairs = [(cc, h) for cc in range(cpg) for h in range(H)]
        rows = [pl.multiple_of((cg * cpg + cc) * C, C) for cc in range(cpg)]
        gc_all, gc_t, beta_all, g_last = [], [], [], []
        for cc in range(cpg):
            ab = ab_ref[pl.ds(rows[cc], C), :]
            sp = jnp.maximum(ab + pdt, 0.0) + jnp.log1p(jnp.exp(-jnp.abs(ab + pdt)))
            gval = jnp.where(lane < H, pa * sp, 0.0)
            ga = jnp.dot(Lm, gval, precision=HI, preferred_element_type=F32)
            gc_all.append(ga)
            gc_t.append(ga.T)
            beta_all.append(jax.nn.sigmoid(ab))
            g_last.append(jnp.where(d == 0, ga[C - 1:C], ga[0:1]))
        ld = lambda cc, col: f_ref[pl.ds(rows[cc], C), col * DK:(col + 1) * DK]
        q16 = [ld(cc, h) for cc, h in pairs]
        k16 = [ld(cc, H + h) for cc, h in pairs]
        k = [t.astype(F32) for t in k16]
        gc = [jnp.broadcast_to(gc_all[cc][:, h:h + 1], (C, DK)) for cc, h in pairs]
        beta = [jnp.broadcast_to(beta_all[cc][:, H + h:H + h + 1], (C, DK)) for cc, h in pairs]
        decay = []
        for i, (cc, h) in enumerate(pairs):
            diff = gc[i][:, :C] - gc_t[cc][h:h + 1, :]
            decay.append(jnp.where(incl, jnp.exp(jnp.where(incl, diff, 0.0)), 0.0))
        kb = [k[i] * beta[i] for i in range(len(pairs))]
        nm = [jnp.where(strict, nt(kb[i], k16[i]) * decay[i], 0.0) for i in range(len(pairs))]
        xinv = [eye - t for t in nm]
        pw = [mm(t, t) for t in nm]
        for it in range(5):
            xinv = [xinv[i] + mm(xinv[i], pw[i]) for i in range(len(pairs))]
            if it < 4:
                pw = [mm(t, t) for t in pw]
        eg = [jnp.exp(t) for t in gc]
        u = [mm(xinv[i], ld(cc, 2 * H + h).astype(F32) * beta[i]) for i, (cc, h) in enumerate(pairs)]
        w = [mm(xinv[i], kb[i] * eg[i]) for i in range(len(pairs))]
        aqk = [nt(q16[i], k16[i]) * decay[i] for i in range(len(pairs))]
        for i, (cc, h) in enumerate(pairs):
            sl = slice(h * DK, (h + 1) * DK)
            r = rows[cc]
            gl = jnp.broadcast_to(g_last[cc][:, h:h + 1], (1, DK))
            w_ref[0, pl.ds(r, C), sl] = w[i].astype(w_ref.dtype)
            u_ref[0, pl.ds(r, C), sl] = u[i]
            qd_ref[0, pl.ds(r, C), sl] = (q16[i].astype(F32) * eg[i]).astype(qd_ref.dtype)
            kt_ref[0, pl.ds(r, C), sl] = (k[i] * jnp.exp(gl - gc[i])).astype(kt_ref.dtype)
            aqk_ref[0, pl.ds(r, C), h * C:(h + 1) * C] = aqk[i].astype(aqk_ref.dtype)
            egl_ref[0, cg * cpg + cc, h:h + 1, :] = jnp.exp(gl)
        return carry

    lax.fori_loop(0, tm // (C * cpg), chunk_group, 0)


def gdn_intra(feats, ab, a_log, dt_bias, tm=256):
    M = feats.shape[0]
    H, DK, C = GDN_HEADS, GDN_DK, GDN_CHUNK
    HD = H * DK
    pa = jnp.zeros((2, 1, LANE), F32).at[:, 0, :H].set(-jnp.exp(a_log.astype(F32)))
    pdt = jnp.zeros((2, 1, LANE), F32).at[:, 0, :H].set(dt_bias.astype(F32))
    idx = np.arange(C)
    lower = idx[:, None] >= idx[None, :]
    lm = jnp.asarray(np.stack([lower, lower.T]).astype(np.float32))
    sm = jnp.asarray(np.stack([idx[:, None] > idx[None, :], idx[:, None] < idx[None, :]]).astype(np.float32))
    big = lambda dt: jax.ShapeDtypeStruct((2, M, HD), dt)
    dspec = pl.BlockSpec((1, tm, HD), lambda i, d: (d, i, 0))
    return pl.pallas_call(
        functools.partial(_gdn_intra_kernel, tm=tm, cpg=4),
        grid=(M // tm, 2),
        in_specs=[pl.BlockSpec((tm, 3 * HD), lambda i, d: (i, 0)),
                  pl.BlockSpec((tm, LANE), lambda i, d: (i, d)),
                  pl.BlockSpec((1, 1, LANE), lambda i, d: (d, 0, 0)),
                  pl.BlockSpec((1, 1, LANE), lambda i, d: (d, 0, 0)),
                  pl.BlockSpec((1, C, C), lambda i, d: (d, 0, 0)),
                  pl.BlockSpec((1, C, C), lambda i, d: (d, 0, 0))],
        out_specs=[dspec, dspec, dspec, dspec,
                   pl.BlockSpec((1, tm, H * C), lambda i, d: (d, i, 0)),
                   pl.BlockSpec((1, tm // C, H, LANE), lambda i, d: (d, i, 0, 0))],
        out_shape=[big(BF16), big(BF16), big(BF16), big(F32),
                   jax.ShapeDtypeStruct((2, M, H * C), BF16),
                   jax.ShapeDtypeStruct((2, M // C, H, LANE), F32)],
        compiler_params=_cparams(("parallel", "arbitrary")),
        name="gdn_intra",
    )(feats, ab, pa, pdt, lm, sm)


def _gdn_scan_kernel(*refs, nch):
    ins = (refs[0:6], refs[6:12])
    outs = refs[12:14]
    s_ref = refs[14]
    C = GDN_CHUNK
    DK = GDN_DK
    H = GDN_HEADS

    @pl.when(pl.program_id(1) == 0)
    def _():
        s_ref[...] = jnp.zeros_like(s_ref)

    def chunk(j, carry):
        cs = (j, nch - 1 - j)
        rs = tuple(pl.multiple_of(c * C, C) for c in cs)
        pairs = [(d, h) for d in range(2) for h in range(H)]
        sl = lambda h: slice(h * DK, (h + 1) * DK)
        ld = lambda k, d, h: ins[d][k][0, pl.ds(rs[d], C), sl(h)]
        S = [s_ref[d, h] for d, h in pairs]
        Sb = [t.astype(BF16) for t in S]
        ws = [jnp.dot(ld(0, d, h), Sb[i], preferred_element_type=F32) for i, (d, h) in enumerate(pairs)]
        qs = [jnp.dot(ld(1, d, h), Sb[i], preferred_element_type=F32) for i, (d, h) in enumerate(pairs)]
        vb = [(ld(3, d, h) - ws[i]).astype(BF16) for i, (d, h) in enumerate(pairs)]
        av = [jnp.dot(ins[d][4][0, pl.ds(rs[d], C), h * C:(h + 1) * C], vb[i], preferred_element_type=F32)
              for i, (d, h) in enumerate(pairs)]
        kv = [lax.dot_general(ld(2, d, h), vb[i], TN, preferred_element_type=F32)
              for i, (d, h) in enumerate(pairs)]
        for i, (d, h) in enumerate(pairs):
            outs[d][pl.ds(rs[d], C), sl(h)] = qs[i] + av[i]
            s_ref[d, h] = S[i] * ins[d][5][0, cs[d], h:h + 1, :] + kv[i]
        return carry

    lax.fori_loop(0, nch, chunk, 0)


def gdn_scan(w, qd, kt, u, aqk, egl, dims, L):
    BL, N, B, D = dims
    M = w.shape[1]
    H, DK, C = GDN_HEADS, GDN_DK, GDN_CHUNK
    HD = H * DK
    blk = L
    nlat = N // blk
    assert N % blk == 0 and blk % C == 0

    def rb(d):
        def index(b, s):
            lat = BL // blk + b * nlat + (s - 1 if d == 0 else nlat - s)
            return jnp.where(s == 0, b, lat)
        return index

    in_specs, args = [], []
    for d in range(2):
        r = rb(d)
        big = pl.BlockSpec((1, blk, HD), lambda b, s, r=r, d=d: (d, r(b, s), 0))
        in_specs += [big, big, big, big,
                     pl.BlockSpec((1, blk, H * C), lambda b, s, r=r, d=d: (d, r(b, s), 0)),
                     pl.BlockSpec((1, blk // C, H, LANE), lambda b, s, r=r, d=d: (d, r(b, s), 0, 0))]
        args += [w, qd, kt, u, aqk, egl]
    return pl.pallas_call(
        functools.partial(_gdn_scan_kernel, nch=blk // C),
        grid=(B, 1 + nlat),
        in_specs=in_specs,
        out_specs=[pl.BlockSpec((blk, HD), lambda b, s, r=rb(d): (r(b, s), 0)) for d in range(2)],
        out_shape=[jax.ShapeDtypeStruct((M, HD), F32)] * 2,
        scratch_shapes=[pltpu.VMEM((2, H, DK, DK), F32)],
        compiler_params=_cparams(("parallel", "arbitrary")),
        name="gdn_scan",
    )(*args)


def _gdn_out_kernel(of_ref, ob_ref, z_ref, ng_ref, w_ref, x_ref, gate_ref, o_ref, a_ref):
    DK = GDN_DK
    for h in range(GDN_HEADS):
        sl = slice(h * DK, (h + 1) * DK)
        o = of_ref[:, sl] + ob_ref[:, sl]
        y = o * lax.rsqrt(jnp.mean(o * o, axis=-1, keepdims=True) + NORM_EPS) * ng_ref[...]
        z = z_ref[:, sl].astype(F32)
        a_ref[:, sl] = (y * _silu(z)).astype(BF16)
    y = jnp.dot(a_ref[...], w_ref[...], preferred_element_type=F32)
    o_ref[...] = x_ref[...] + gate_ref[0] * y


def gdn_output(o_f, o_b, proj, norm_g, w_o, xs, gate, dims, row_start=0, tm=512):
    M, D = xs.shape
    HD = GDN_HEADS * GDN_DK
    off = row_start // tm
    return pl.pallas_call(
        _gdn_out_kernel,
        grid=(M // tm - off,),
        in_specs=[pl.BlockSpec((tm, HD), lambda i: (i + off, 0)),
                  pl.BlockSpec((tm, HD), lambda i: (i + off, 0)),
                  pl.BlockSpec((tm, HD), lambda i: (i + off, 3)),
                  pl.BlockSpec((1, GDN_DK), lambda i: (0, 0)),
                  pl.BlockSpec((HD, D), lambda i: (0, 0)),
                  pl.BlockSpec((tm, D), lambda i: (i + off, 0)),
                  _mod_spec(tm, off, dims)],
        out_specs=pl.BlockSpec((tm, D), lambda i: (i + off, 0)),
        out_shape=jax.ShapeDtypeStruct((M, D), F32),
        scratch_shapes=[pltpu.VMEM((tm, HD), BF16)],
        input_output_aliases={5: 0},
        compiler_params=_cparams(("parallel",)),
        name="gdn_output",
    )(o_f, o_b, proj, norm_g.reshape(1, GDN_DK), w_o, xs, gate)


def _pool_tables(tm, L):
    amats, invs = [], []
    t = np.arange(tm)
    for seg in (L, GRID_W):
        a_v, i_v = [], []
        tl = t % seg
        for win in POOL_WINDOWS:
            lo = np.clip(tl - win // 2, 0, seg)
            hi = np.clip(tl + win // 2, 0, seg)
            same = (t[:, None] // seg) == (t[None, :] // seg)
            a = same & (tl[None, :] >= lo[:, None]) & (tl[None, :] < hi[:, None])
            a_v.append(a.astype(np.float32))
            i_v.append(np.broadcast_to((1.0 / (hi - lo))[:, None], (tm, LANE)).astype(np.float32))
        amats.append(np.stack(a_v))
        invs.append(np.stack(i_v))
    return jnp.asarray(np.stack(amats), BF16), jnp.asarray(np.stack(invs), F32)


def _pool_kernel(x_ref, g_ref, sh_ref, sc_ref, gate_ref, a_ref, ic_ref, pw_ref, ls_ref, o_ref):
    x = x_ref[...]
    h = _prenorm(x, g_ref[...], sh_ref[0], sc_ref[0])
    G = pw_ref.shape[1]
    for gi in range(len(POOL_WINDOWS)):
        sl = slice(gi * G, (gi + 1) * G)
        hg = h[:, sl]
        hi = hg.astype(BF16)
        lo = (hg - hi.astype(F32)).astype(BF16)
        am = a_ref[0, gi]
        wsum = jnp.dot(am, hi, preferred_element_type=F32) + jnp.dot(am, lo, preferred_element_type=F32)
        ic = ic_ref[0, gi]
        mean = wsum * jnp.concatenate([ic] * (G // LANE), axis=-1)
        pooled = (mean - hg).astype(BF16)
        y = jnp.dot(pooled, pw_ref[gi], preferred_element_type=F32) * ls_ref[:, sl]
        o_ref[:, sl] = x[:, sl] + gate_ref[0][:, sl] * y


def pool_sublayer(xs, g, shift, scale, gate, pool_w, ls, dims, L, row_start=0, tm=256):
    BL, N, B, D = dims
    M = xs.shape[0]
    assert tm == L and tm % GRID_W == 0
    amat, inv = _pool_tables(tm, L)
    off = row_start // tm
    nct = BL // tm
    G = D // len(POOL_WINDOWS)
    variant = lambda i: jnp.where(i + off < nct, 0, 1)
    return pl.pallas_call(
        _pool_kernel,
        grid=(M // tm - off,),
        in_specs=[pl.BlockSpec((tm, D), lambda i: (i + off, 0)),
                  pl.BlockSpec((1, D), lambda i: (0, 0)),
                  _mod_spec(tm, off, dims), _mod_spec(tm, off, dims), _mod_spec(tm, off, dims),
                  pl.BlockSpec((1, 4, tm, tm), lambda i: (variant(i), 0, 0, 0)),
                  pl.BlockSpec((1, 4, tm, LANE), lambda i: (variant(i), 0, 0, 0)),
                  pl.BlockSpec((4, G, G), lambda i: (0, 0, 0)),
                  pl.BlockSpec((1, D), lambda i: (0, 0))],
        out_specs=pl.BlockSpec((tm, D), lambda i: (i + off, 0)),
        out_shape=jax.ShapeDtypeStruct((M, D), F32),
        input_output_aliases={0: 0},
        compiler_params=_cparams(("parallel",)),
        name="pool_sublayer",
    )(xs, g.reshape(1, D), shift, scale, gate, amat, inv, pool_w.astype(BF16), ls.reshape(1, D))


def _router_kernel(x_ref, g_ref, sh_ref, sc_ref, wr_ref, h_ref, ti_ref, tp_ref):
    h = _prenorm(x_ref[...], g_ref[...], sh_ref[0], sc_ref[0])
    h_ref[...] = h
    logits = jnp.dot(h, wr_ref[...], precision=HI, preferred_element_type=F32)
    lane = lax.broadcasted_iota(jnp.int32, logits.shape, 1)
    valid = lane < N_EXPERTS
    lg = jnp.where(valid, logits, NEG_INF)
    e = jnp.where(valid, jnp.exp(lg - jnp.max(lg, axis=-1, keepdims=True)), 0.0)
    probs = jnp.where(valid, e / jnp.sum(e, axis=-1, keepdims=True), -1.0)
    p1 = jnp.max(probs, axis=-1, keepdims=True)
    i1 = jnp.min(jnp.where(probs == p1, lane, LANE), axis=-1, keepdims=True)
    rest = jnp.where(lane == i1, -1.0, probs)
    p2 = jnp.max(rest, axis=-1, keepdims=True)
    i2 = jnp.min(jnp.where(rest == p2, lane, LANE), axis=-1, keepdims=True)
    tot = p1 + p2
    tp_ref[...] = jnp.where(lane == 0, p1 / tot, jnp.where(lane == 1, p2 / tot, 0.0))
    ti_ref[...] = jnp.where(lane == 0, i1, jnp.where(lane == 1, i2, 0))


def moe_router(xs, g, shift, scale, w_router, dims, row_start=0, tm=512):
    M, D = xs.shape
    off = row_start // tm
    wr = jnp.zeros((D, LANE), F32).at[:, :N_EXPERTS].set(w_router)
    Mo = M - row_start
    ospec = lambda w: pl.BlockSpec((tm, w), lambda i: (i, 0))
    return pl.pallas_call(
        _router_kernel,
        grid=(Mo // tm,),
        in_specs=[pl.BlockSpec((tm, D), lambda i: (i + off, 0)),
                  pl.BlockSpec((1, D), lambda i: (0, 0)),
                  _mod_spec(tm, off, dims), _mod_spec(tm, off, dims),
                  pl.BlockSpec((D, LANE), lambda i: (0, 0))],
        out_specs=[ospec(D), ospec(LANE), ospec(LANE)],
        out_shape=[jax.ShapeDtypeStruct((Mo, D), F32),
                   jax.ShapeDtypeStruct((Mo, LANE), jnp.int32),
                   jax.ShapeDtypeStruct((Mo, LANE), F32)],
        compiler_params=_cparams(("parallel",)),
        name="moe_router",
    )(xs, g.reshape(1, D), shift, scale, wr)


def _route_positions(top_e):
    n_tok = top_e.shape[0]
    e_flat = top_e.reshape(-1)
    onehot = (e_flat[:, None] == jnp.arange(N_EXPERTS, dtype=jnp.int32)[None, :]).astype(jnp.int32)
    csum = jnp.cumsum(onehot, axis=0)
    rank = jnp.sum(csum * onehot, axis=-1) - 1
    counts = csum[-1]
    padded = ((counts + MOE_BLOCK - 1) // MOE_BLOCK) * MOE_BLOCK
    ends = jnp.cumsum(padded)
    starts = ends - padded
    pos = (jnp.sum(onehot * starts[None, :], axis=-1) + rank).reshape(n_tok, TOP_K)
    n_blocks = -(-(n_tok * TOP_K) // MOE_BLOCK) + N_EXPERTS
    block_e = jnp.minimum(jnp.searchsorted(ends, jnp.arange(n_blocks) * MOE_BLOCK, side='right'),
                          N_EXPERTS - 1).astype(jnp.int32)
    pos = pos.astype(jnp.int32)
    cap = n_blocks * MOE_BLOCK
    n_pairs = n_tok * TOP_K
    pair_row = (jnp.arange(TOP_K, dtype=jnp.int32)[None, :] * n_tok
                + jnp.arange(n_tok, dtype=jnp.int32)[:, None]).reshape(-1)
    dst = (n_pairs + jnp.arange(cap, dtype=jnp.int32)).at[pos.reshape(-1)].set(pair_row, unique_indices=True)
    src = jnp.where(dst < n_pairs, dst % n_tok, 0)
    lead = n_pairs + cap + jnp.arange(MOE_BLOCK, dtype=jnp.int32)
    dst = jnp.concatenate([lead, dst])
    n_used = (ends[-1] // MOE_BLOCK).astype(jnp.int32).reshape(1)
    return src, dst, block_e, n_used, n_blocks


def _expert_kernel(be_ref, nu_ref, tokc_ref, tokn_ref, dstp_ref, dstc_ref, h_hbm, w1_ref, w3_ref, w2_ref,
                   y_hbm, xg_ref, ob_ref, a_ref, sem_g, sem_o, *, tf, n_blocks):
    del be_ref
    i = pl.program_id(0)
    n_used = nu_ref[0]
    slot = i % 2
    other = 1 - slot
    F = w1_ref.shape[2]
    nchunk = F // tf
    per = -(-MOE_BLOCK // nchunk)

    def gather_copy(tok_ref, s, t):
        return pltpu.make_async_copy(h_hbm.at[pl.ds(tok_ref[0, 0, t], 1), :],
                                     xg_ref.at[s, pl.ds(t, 1), :], sem_g.at[s])

    def scatter_copy(dst_ref, s, t):
        return pltpu.make_async_copy(ob_ref.at[s, pl.ds(t, 1), :],
                                     y_hbm.at[pl.ds(dst_ref[0, 0, t], 1), :], sem_o.at[s])

    def start_loop(make, ref, s):
        def body(t, carry):
            make(ref, s, t).start()
            return carry
        lax.fori_loop(0, MOE_BLOCK, body, 0, unroll=8)

    def wait_gather(s):
        pltpu.make_async_copy(h_hbm.at[pl.ds(0, MOE_BLOCK), :], xg_ref.at[s], sem_g.at[s]).wait()

    def wait_scatter(s):
        pltpu.make_async_copy(ob_ref.at[s], y_hbm.at[pl.ds(0, MOE_BLOCK), :], sem_o.at[s]).wait()

    @pl.when(i == 0)
    def _():
        ob_ref[...] = jnp.zeros_like(ob_ref)
        start_loop(gather_copy, tokc_ref, 0)

    wait_gather(slot)

    @pl.when(i >= 1)
    def _():
        wait_scatter(slot)

    @pl.when(i < n_used)
    def _():
        hb = xg_ref[slot].astype(BF16)
        for c in range(nchunk):
            cs = slice(c * tf, (c + 1) * tf)
            gg = jnp.dot(hb, w1_ref[0, :, cs], preferred_element_type=F32)
            uu = jnp.dot(hb, w3_ref[0, :, cs], preferred_element_type=F32)
            a_ref[:, cs] = (_silu(gg) * uu).astype(BF16)
            for t in range(c * per, min((c + 1) * per, MOE_BLOCK)):
                gather_copy(tokn_ref, other, t).start()
                scatter_copy(dstp_ref, other, t).start()
        ob_ref[slot] = jnp.dot(a_ref[...], w2_ref[0], preferred_element_type=F32)

    @pl.when(i >= n_used)
    def _():
        start_loop(gather_copy, tokn_ref, other)
        start_loop(scatter_copy, dstp_ref, other)
        ob_ref[slot] = jnp.zeros(ob_ref.shape[1:], F32)

    @pl.when(i == n_blocks - 1)
    def _():
        start_loop(scatter_copy, dstc_ref, slot)
        wait_scatter(other)
        wait_scatter(slot)
        wait_gather(other)


def moe_experts(h, src, dst, block_e, n_used, w1, w3, w2, n_blocks, tf=256):
    n_tok, D = h.shape
    F = w1.shape[2]
    last = n_blocks - 1
    smem = lambda index: pl.BlockSpec((1, 1, MOE_BLOCK), index, memory_space=pltpu.SMEM)
    resident = dict(pipeline_mode=pl.Buffered(1))
    grid_spec = pltpu.PrefetchScalarGridSpec(
        num_scalar_prefetch=2,
        grid=(n_blocks,),
        in_specs=[smem(lambda i, be, nu: (i, 0, 0)),
                  smem(lambda i, be, nu: (jnp.minimum(i + 1, last), 0, 0)),
                  smem(lambda i, be, nu: (i, 0, 0)),
                  smem(lambda i, be, nu: (i + 1, 0, 0)),
                  pl.BlockSpec(memory_space=pl.ANY),
                  pl.BlockSpec((1, D, F), lambda i, be, nu: (be[i], 0, 0), **resident),
                  pl.BlockSpec((1, D, F), lambda i, be, nu: (be[i], 0, 0), **resident),
                  pl.BlockSpec((1, F, D), lambda i, be, nu: (be[i], 0, 0), **resident)],
        out_specs=pl.BlockSpec(memory_space=pl.ANY),
        scratch_shapes=[pltpu.VMEM((2, MOE_BLOCK, D), F32), pltpu.VMEM((2, MOE_BLOCK, D), F32),
                        pltpu.VMEM((MOE_BLOCK, F), BF16),
                        pltpu.SemaphoreType.DMA((2,)), pltpu.SemaphoreType.DMA((2,))],
    )
    src3 = src.reshape(n_blocks, 1, MOE_BLOCK)
    dst3 = dst.reshape(n_blocks + 1, 1, MOE_BLOCK)
    n_rows = n_tok * TOP_K + (n_blocks + 1) * MOE_BLOCK
    return pl.pallas_call(
        functools.partial(_expert_kernel, tf=tf, n_blocks=n_blocks),
        grid_spec=grid_spec,
        out_shape=jax.ShapeDtypeStruct((n_rows, D), F32),
        compiler_params=_cparams(("arbitrary",)),
        name="moe_experts",
    )(block_e, n_used, src3, src3, dst3, dst3, h, w1, w3, w2)


def _combine_kernel(y0_ref, y1_ref, tp_ref, x_ref, gate_ref, o_ref):
    tp = tp_ref[...]
    y = tp[:, 0:1] * y0_ref[...] + tp[:, 1:2] * y1_ref[...]
    o_ref[...] = x_ref[...] + gate_ref[0] * y


def _combine_norm_kernel(y0_ref, y1_ref, tp_ref, x_ref, gate_ref, fg_ref, o_ref):
    tp = tp_ref[...]
    y = tp[:, 0:1] * y0_ref[...] + tp[:, 1:2] * y1_ref[...]
    x = x_ref[...] + gate_ref[0] * y
    o_ref[...] = x * lax.rsqrt(jnp.mean(x * x, axis=-1, keepdims=True) + NORM_EPS) * fg_ref[...]


def moe_combine(y, top_p, xs, gate, dims, row_start=0, final_g=None, tm=512):
    M, D = xs.shape
    off = row_start // tm
    nt = (M - row_start) // tm
    in_specs = [pl.BlockSpec((tm, D), lambda i: (i, 0)),
                pl.BlockSpec((tm, D), lambda i: (nt + i, 0)),
                pl.BlockSpec((tm, LANE), lambda i: (i, 0)),
                pl.BlockSpec((tm, D), lambda i: (i + off, 0)),
                _mod_spec(tm, off, dims)]
    if final_g is None:
        return pl.pallas_call(
            _combine_kernel,
            grid=(nt,),
            in_specs=in_specs,
            out_specs=pl.BlockSpec((tm, D), lambda i: (i + off, 0)),
            out_shape=jax.ShapeDtypeStruct((M, D), F32),
            input_output_aliases={3: 0},
            compiler_params=_cparams(("parallel",)),
            name="moe_combine",
        )(y, y, top_p, xs, gate)
    return pl.pallas_call(
        _combine_norm_kernel,
        grid=(nt,),
        in_specs=in_specs + [pl.BlockSpec((1, D), lambda i: (0, 0))],
        out_specs=pl.BlockSpec((tm, D), lambda i: (i, 0)),
        out_shape=jax.ShapeDtypeStruct((M - row_start, D), F32),
        compiler_params=_cparams(("parallel",)),
        name="moe_combine_norm",
    )(y, y, top_p, xs, gate, final_g.reshape(1, D))


def moe_sublayer(xs, g, shift, scale, gate, w_router, w1, w3, w2, dims, row_start, final_g=None):
    h, top_i, top_p = moe_router(xs, g, shift, scale, w_router, dims, row_start)
    src, dst, block_e, n_used, n_blocks = _route_positions(top_i[:, :TOP_K])
    y = moe_experts(h, src, dst, block_e, n_used, w1, w3, w2, n_blocks)
    return moe_combine(y, top_p, xs, gate, dims, row_start, final_g)


def _final_norm_kernel(x_ref, g_ref, o_ref):
    x = x_ref[...]
    o_ref[...] = x * lax.rsqrt(jnp.mean(x * x, axis=-1, keepdims=True) + NORM_EPS) * g_ref[...]


def final_norm(xs, g, row_start, tm=512):
    M, D = xs.shape
    off = row_start // tm
    return pl.pallas_call(
        _final_norm_kernel,
        grid=((M - row_start) // tm,),
        in_specs=[pl.BlockSpec((tm, D), lambda i: (i + off, 0)),
                  pl.BlockSpec((1, D), lambda i: (0, 0))],
        out_specs=pl.BlockSpec((tm, D), lambda i: (i, 0)),
        out_shape=jax.ShapeDtypeStruct((M - row_start, D), F32),
        compiler_params=_cparams(("parallel",)),
        name="final_norm",
    )(xs, g.reshape(1, D))


def na_mixer(xs, g, m, w_qkv, w_o, rpb, dims, L, need_ctx):
    BL, N, B, D = dims
    col_scale = jnp.concatenate([jnp.full((D,), NA_HEAD_DIM ** -0.5 * LOG2E, F32), jnp.ones((2 * D,), F32)])
    qkv = prenorm_matmul(xs, g, m[0], m[1], w_qkv.astype(BF16), dims, col_scale=col_scale)
    bias_tab = _na_bias_table(rpb.astype(F32))
    w_o = w_o.astype(BF16)
    if need_ctx:
        xs = matmul_residual(ctx_attention(qkv, dims, L), w_o, xs, m[2], dims, row_start=0)
    return matmul_residual(na_attention(qkv, bias_tab, dims, L), w_o, xs, m[2], dims, row_start=BL)


def gdn_mixer(xs, g, m, w_in, conv_w, a_log, dt_bias, norm_g, w_o, dims, L, need_ctx):
    BL, N, B, D = dims
    H = GDN_HEADS
    HD = H * GDN_DK
    w_main = w_in[:, :4 * HD].astype(BF16)
    w_gate = w_in[:, 4 * HD:].astype(F32)
    w_ab = jnp.zeros((D, 2 * LANE), F32)
    for d in range(2):
        w_ab = w_ab.at[:, d * LANE:d * LANE + H].set(w_gate[:, d * H:(d + 1) * H])
        w_ab = w_ab.at[:, d * LANE + H:d * LANE + 2 * H].set(w_gate[:, (2 + d) * H:(3 + d) * H])
    proj, ab = prenorm_matmul(xs, g, m[0], m[1], w_main, dims, w_hi=w_ab)
    feats = gdn_features(proj, conv_w.astype(F32), dims, L)
    w, qd, kt, u, aqk, egl = gdn_intra(feats, ab, a_log, dt_bias)
    o_f, o_b = gdn_scan(w, qd, kt, u, aqk, egl, dims, L)
    return gdn_output(o_f, o_b, proj, norm_g.astype(F32), w_o.astype(BF16), xs, m[2], dims,
                      row_start=0 if need_ctx else BL)


def kernel(x, c, ctx, c_ctx, ada_w, ada_b, norm_g, final_g, na_w_qkv, na_w_o, na_rpb, gdn_w_in, gdn_conv, gdn_a_log, gdn_dt_bias, gdn_norm_g, gdn_w_o, pool_w, pool_scale, ffn_w1, ffn_w3, ffn_w2, moe_router, moe_w1, moe_w3, moe_w2):
    B, N, D = x.shape
    L = ctx.shape[1]
    depth = ada_w.shape[0]
    BL = B * L
    dims = (BL, N, B, D)
    xs = jnp.concatenate([ctx.reshape(BL, D), x.reshape(B * N, D)], axis=0)

    R = -(-(B + 1) // 8) * 8
    cvec = jnp.zeros((R, D), F32).at[:B].set(c).at[B].set(c_ctx)
    mods = ada_table(cvec, ada_w, ada_b)[:, :B + 1].reshape(depth, B + 1, 6, 1, D)

    for i in range(depth):
        last = i == depth - 1
        m = [mods[i, :, k] for k in range(6)]
        j = i // 3
        kind = i % 3
        if kind == 0:
            xs = na_mixer(xs, norm_g[i, 0], m, na_w_qkv[j], na_w_o[j], na_rpb[j], dims, L, not last)
        elif kind == 1:
            xs = gdn_mixer(xs, norm_g[i, 0], m, gdn_w_in[j], gdn_conv[j], gdn_a_log[j], gdn_dt_bias[j],
                           gdn_norm_g[j], gdn_w_o[j], dims, L, not last)
        else:
            xs = pool_sublayer(xs, norm_g[i, 0], m[0], m[1], m[2], pool_w[j], pool_scale[j], dims, L,
                               row_start=0 if not last else BL)
        row_start = BL if last else 0
        f = i // 2
        if i % 2 == 0:
            xs = ffn_sublayer(xs, norm_g[i, 1], m[3], m[4], m[5], ffn_w1[f].astype(BF16),
                              ffn_w3[f].astype(BF16), ffn_w2[f].astype(BF16), dims, row_start=row_start)
        else:
            xs = moe_sublayer(xs, norm_g[i, 1], m[3], m[4], m[5], moe_router[f], moe_w1[f].astype(BF16),
                              moe_w3[f].astype(BF16), moe_w2[f].astype(BF16), dims, row_start,
                              final_g=final_g if last else None)
            if last:
                return xs.reshape(B, N, D)
    return final_norm(xs, final_g, BL).reshape(B, N, D)
```

```python
import functools

import numpy as np
import jax
import jax.numpy as jnp
from jax import lax
from jax.experimental import pallas as pl
from jax.experimental.pallas import tpu as pltpu

F32 = jnp.float32
BF16 = jnp.bfloat16
HI = lax.Precision.HIGHEST

NORM_EPS = 1e-6
NEG_INF = -1e30
LOG2E = 1.4426950408889634
GRID_W = 64
NA_HEADS = 16
NA_HEAD_DIM = 64
NA_ROWS = 8
NA_COLS = 16
GDN_HEADS = 8
GDN_DK = 128
GDN_CHUNK = 64
POOL_WINDOWS = (2, 4, 8, 16)
N_EXPERTS = 8
TOP_K = 2
MOE_BLOCK = 512
LANE = 128
V7X_VMEM_LIMIT = 56 * 1024 * 1024

NT = (((1,), (1,)), ((), ()))
TN = (((0,), (0,)), ((), ()))


def _cparams(sem, vmem=V7X_VMEM_LIMIT):
    return pltpu.CompilerParams(dimension_semantics=sem, vmem_limit_bytes=vmem)


def _silu(v):
    return v * jax.nn.sigmoid(v)


def _prenorm(x, g, shift, scale):
    ms = jnp.mean(x * x, axis=-1, keepdims=True)
    y = x * lax.rsqrt(ms + NORM_EPS) * g
    return y * (1.0 + scale) + shift


def _mod_index(i, tm, BL, N, B):
    nct = BL // tm
    return jnp.where(i < nct, B, (i - nct) // (N // tm))


def _mod_spec(tm, off, dims):
    BL, N, B, D = dims
    return pl.BlockSpec((1, 1, D), lambda i, *_: (_mod_index(i + off, tm, BL, N, B), 0, 0))


def _ada_kernel(c_ref, w_ref, b_ref, o_ref):
    cv = c_ref[...]
    o_ref[0] = jnp.dot(_silu(cv), w_ref[0], precision=HI, preferred_element_type=F32) + b_ref[0]


def ada_table(cvec, ada_w, ada_b):
    depth, D, D6 = ada_w.shape
    R = cvec.shape[0]
    tn = 1536
    return pl.pallas_call(
        _ada_kernel,
        grid=(depth, D6 // tn),
        in_specs=[pl.BlockSpec((R, D), lambda l, j: (0, 0)),
                  pl.BlockSpec((1, D, tn), lambda l, j: (l, 0, j)),
                  pl.BlockSpec((1, 1, tn), lambda l, j: (l, 0, j))],
        out_specs=pl.BlockSpec((1, R, tn), lambda l, j: (l, 0, j)),
        out_shape=jax.ShapeDtypeStruct((depth, R, D6), F32),
        compiler_params=_cparams(("arbitrary", "arbitrary")),
        name="ada_table",
    )(cvec, ada_w, ada_b.reshape(depth, 1, D6))


def _prenorm_matmul_kernel(x_ref, g_ref, sh_ref, sc_ref, w_ref, cs_ref, o_ref):
    h = _prenorm(x_ref[...], g_ref[...], sh_ref[0], sc_ref[0]).astype(BF16)
    y = jnp.dot(h, w_ref[...], preferred_element_type=F32)
    o_ref[...] = (y * cs_ref[...]).astype(o_ref.dtype)


def _prenorm_matmul2_kernel(x_ref, g_ref, sh_ref, sc_ref, w_ref, w2_ref, o_ref, o2_ref):
    h = _prenorm(x_ref[...], g_ref[...], sh_ref[0], sc_ref[0])
    o_ref[...] = jnp.dot(h.astype(BF16), w_ref[...], preferred_element_type=F32).astype(o_ref.dtype)
    o2_ref[...] = jnp.dot(h, w2_ref[...], precision=HI, preferred_element_type=F32)


def prenorm_matmul(xs, g, shift, scale, w, dims, col_scale=None, w_hi=None, tm=512):
    M, D = xs.shape
    Nout = w.shape[1]
    specs = [pl.BlockSpec((tm, D), lambda i: (i, 0)),
             pl.BlockSpec((1, D), lambda i: (0, 0)),
             _mod_spec(tm, 0, dims), _mod_spec(tm, 0, dims),
             pl.BlockSpec((D, Nout), lambda i: (0, 0))]
    ospec = pl.BlockSpec((tm, Nout), lambda i: (i, 0))
    oshape = jax.ShapeDtypeStruct((M, Nout), BF16)
    if w_hi is None:
        cs = jnp.ones((1, Nout), F32) if col_scale is None else col_scale.reshape(1, Nout)
        return pl.pallas_call(
            _prenorm_matmul_kernel,
            grid=(M // tm,),
            in_specs=specs + [pl.BlockSpec((1, Nout), lambda i: (0, 0))],
            out_specs=ospec,
            out_shape=oshape,
            compiler_params=_cparams(("parallel",)),
            name="prenorm_matmul",
        )(xs, g.reshape(1, D), shift, scale, w, cs)
    N2 = w_hi.shape[1]
    return pl.pallas_call(
        _prenorm_matmul2_kernel,
        grid=(M // tm,),
        in_specs=specs + [pl.BlockSpec((D, N2), lambda i: (0, 0))],
        out_specs=[ospec, pl.BlockSpec((tm, N2), lambda i: (i, 0))],
        out_shape=[oshape, jax.ShapeDtypeStruct((M, N2), F32)],
        compiler_params=_cparams(("parallel",)),
        name="prenorm_matmul2",
    )(xs, g.reshape(1, D), shift, scale, w, w_hi)


def _matmul_res_kernel(a_ref, w_ref, x_ref, gate_ref, o_ref):
    y = jnp.dot(a_ref[...], w_ref[...], preferred_element_type=F32)
    o_ref[...] = x_ref[...] + gate_ref[0] * y


def matmul_residual(a, w, xs, gate, dims, row_start=0, tm=512):
    M, D = xs.shape
    K = a.shape[1]
    off = row_start // tm
    return pl.pallas_call(
        _matmul_res_kernel,
        grid=(a.shape[0] // tm,),
        in_specs=[pl.BlockSpec((tm, K), lambda i: (i, 0)),
                  pl.BlockSpec((K, D), lambda i: (0, 0)),
                  pl.BlockSpec((tm, D), lambda i: (i + off, 0)),
                  _mod_spec(tm, off, dims)],
        out_specs=pl.BlockSpec((tm, D), lambda i: (i + off, 0)),
        out_shape=jax.ShapeDtypeStruct((M, D), F32),
        input_output_aliases={2: 0},
        compiler_params=_cparams(("parallel",)),
        name="matmul_residual",
    )(a, w, xs, gate)


def _ffn_kernel(x_ref, g_ref, sh_ref, sc_ref, gate_ref, w1_ref, w3_ref, w2_ref, o_ref, a_ref, *, tf):
    x = x_ref[...]
    h = _prenorm(x, g_ref[...], sh_ref[0], sc_ref[0]).astype(BF16)
    F = w1_ref.shape[1]
    for f0 in range(0, F, tf):
        gg = jnp.dot(h, w1_ref[:, f0:f0 + tf], preferred_element_type=F32)
        uu = jnp.dot(h, w3_ref[:, f0:f0 + tf], preferred_element_type=F32)
        a_ref[:, f0:f0 + tf] = (_silu(gg) * uu).astype(BF16)
    y = jnp.dot(a_ref[...], w2_ref[...], preferred_element_type=F32)
    o_ref[...] = x + gate_ref[0] * y


def ffn_sublayer(xs, g, shift, scale, gate, w1, w3, w2, dims, row_start=0, tm=512, tf=256):
    M, D = xs.shape
    F = w1.shape[1]
    off = row_start // tm
    resident = dict(pipeline_mode=pl.Buffered(1))
    return pl.pallas_call(
        functools.partial(_ffn_kernel, tf=tf),
        grid=(M // tm - off,),
        in_specs=[pl.BlockSpec((tm, D), lambda i: (i + off, 0)),
                  pl.BlockSpec((1, D), lambda i: (0, 0)),
                  _mod_spec(tm, off, dims), _mod_spec(tm, off, dims), _mod_spec(tm, off, dims),
                  pl.BlockSpec((D, F), lambda i: (0, 0), **resident),
                  pl.BlockSpec((D, F), lambda i: (0, 0), **resident),
                  pl.BlockSpec((F, D), lambda i: (0, 0), **resident)],
        out_specs=pl.BlockSpec((tm, D), lambda i: (i + off, 0)),
        out_shape=jax.ShapeDtypeStruct((M, D), F32),
        scratch_shapes=[pltpu.VMEM((tm, F), BF16)],
        input_output_aliases={0: 0},
        compiler_params=_cparams(("parallel",)),
        name="ffn_sublayer",
    )(xs, g.reshape(1, D), shift, scale, gate, w1, w3, w2)


def _na_bias_table(rpb):
    H = rpb.shape[0]
    col = np.arange(GRID_W)
    c0 = np.clip(col - NA_COLS // 2, 0, GRID_W - NA_COLS)
    in_win = (col[None, :] >= c0[:, None]) & (col[None, :] < c0[:, None] + NA_COLS)
    dc = np.clip(col[None, :] - col[:, None], -(NA_COLS - 1), NA_COLS - 1) + NA_COLS - 1
    var = np.arange(NA_ROWS)[:, None]
    t = np.arange(NA_ROWS)[None, :]
    dr = t - var + NA_ROWS - 1
    tab = rpb[:, dr][:, :, :, dc]
    tab = jnp.where(jnp.asarray(in_win)[None, None, None], tab * LOG2E, NEG_INF)
    tab = tab.reshape(H // 2, 2, NA_ROWS, NA_ROWS, GRID_W, GRID_W)
    tab = jnp.transpose(tab, (0, 2, 3, 5, 1, 4))
    return tab.reshape(H // 2, NA_ROWS, NA_ROWS * GRID_W, 2 * GRID_W)


def _na_kernel(q_ref, k_ref, v_ref, kc_ref, vc_ref, bias_ref, o_ref, *, rows):
    W = GRID_W
    KW = NA_ROWS * W
    sub = lax.broadcasted_iota(jnp.int32, (2 * W, LANE), 0)
    lane = lax.broadcasted_iota(jnp.int32, (2 * W, LANE), 1)
    same_head = (sub < W) == (lane < W)
    lane_h = lax.broadcasted_iota(jnp.int32, (W, LANE), 1)
    kc = kc_ref[...]
    vc = vc_ref[...]

    kbase, qbase, var, scores, probs = {}, {}, {}, {}, {}

    def score_stage(rg, r):
        row = rg * NA_ROWS + r
        r0 = jnp.clip(row - NA_ROWS // 2, 0, rows - NA_ROWS)
        var[r] = row - r0
        kbase[r] = pl.multiple_of(r0 * W, W)
        qbase[r] = pl.multiple_of(row * W, W)
        qr = q_ref[pl.ds(qbase[r], W), :]
        q2 = jnp.concatenate([qr, qr], axis=0)
        q2 = jnp.where(same_head, q2, jnp.zeros_like(q2))
        kw = k_ref[pl.ds(kbase[r], KW), :]
        scores[r] = (lax.dot_general(kw, q2, NT, preferred_element_type=F32),
                     lax.dot_general(kc, q2, NT, preferred_element_type=F32))

    def softmax_stage(rg, r):
        s_loc, s_ctx = scores.pop(r)
        s_loc = s_loc + bias_ref[0, var[r]]
        m = jnp.maximum(jnp.max(s_loc, axis=0, keepdims=True), jnp.max(s_ctx, axis=0, keepdims=True))
        p_loc = jnp.exp2(s_loc - m)
        p_ctx = jnp.exp2(s_ctx - m)
        inv = 1.0 / (jnp.sum(p_loc, axis=0, keepdims=True) + jnp.sum(p_ctx, axis=0, keepdims=True))
        probs[r] = ((p_loc * inv).astype(BF16), (p_ctx * inv).astype(BF16))

    def value_stage(rg, r):
        p_loc, p_ctx = probs.pop(r)
        vw = v_ref[pl.ds(kbase[r], KW), :]
        o = (lax.dot_general(p_loc, vw, TN, preferred_element_type=F32)
             + lax.dot_general(p_ctx, vc, TN, preferred_element_type=F32))
        o_sel = jnp.where(lane_h < W, o[:W], o[W:])
        o_ref[pl.ds(qbase[r], W), :] = o_sel.astype(o_ref.dtype)

    def row_group(rg, carry):
        for stage in (score_stage, softmax_stage, value_stage):
            for r in range(NA_ROWS):
                stage(rg, r)
        return carry

    lax.fori_loop(0, rows // NA_ROWS, row_group, 0)


def na_attention(qkv, bias_tab, dims, L):
    BL, N, B, D = dims
    rows = N // GRID_W
    HP = NA_HEADS // 2
    tq = NA_ROWS * GRID_W
    assert BL % N == 0 and N % tq == 0 and rows >= NA_ROWS
    return pl.pallas_call(
        functools.partial(_na_kernel, rows=rows),
        grid=(HP, B),
        in_specs=[pl.BlockSpec((N, LANE), lambda hp, b: (BL // N + b, hp)),
                  pl.BlockSpec((N, LANE), lambda hp, b: (BL // N + b, HP + hp)),
                  pl.BlockSpec((N, LANE), lambda hp, b: (BL // N + b, 2 * HP + hp)),
                  pl.BlockSpec((L, LANE), lambda hp, b: (b, HP + hp)),
                  pl.BlockSpec((L, LANE), lambda hp, b: (b, 2 * HP + hp)),
                  pl.BlockSpec((1, NA_ROWS, tq, LANE), lambda hp, b: (hp, 0, 0, 0))],
        out_specs=pl.BlockSpec((N, LANE), lambda hp, b: (b, hp)),
        out_shape=jax.ShapeDtypeStruct((B * N, D), BF16),
        compiler_params=_cparams(("parallel", "parallel")),
        name="na_attention",
    )(qkv, qkv, qkv, qkv, qkv, bias_tab)


def _ctx_attn_kernel(q_ref, k_ref, v_ref, o_ref):
    q = q_ref[...]
    k = k_ref[...]
    v = v_ref[...]
    lane = lax.broadcasted_iota(jnp.int32, q.shape, 1)
    outs = []
    for hh in range(2):
        msk = (lane < GRID_W) if hh == 0 else (lane >= GRID_W)
        qh = jnp.where(msk, q, jnp.zeros_like(q))
        s = lax.dot_general(qh, k, NT, preferred_element_type=F32)
        m = jnp.max(s, axis=-1, keepdims=True)
        p = jnp.exp2(s - m)
        p = (p / jnp.sum(p, axis=-1, keepdims=True)).astype(BF16)
        outs.append(jnp.dot(p, v, preferred_element_type=F32))
    o_ref[...] = jnp.where(lane < GRID_W, outs[0], outs[1]).astype(o_ref.dtype)


def ctx_attention(qkv, dims, L):
    BL, N, B, D = dims
    HP = NA_HEADS // 2
    return pl.pallas_call(
        _ctx_attn_kernel,
        grid=(B, HP),
        in_specs=[pl.BlockSpec((L, LANE), lambda b, hp: (b, hp)),
                  pl.BlockSpec((L, LANE), lambda b, hp: (b, HP + hp)),
                  pl.BlockSpec((L, LANE), lambda b, hp: (b, 2 * HP + hp))],
        out_specs=pl.BlockSpec((L, LANE), lambda b, hp: (b, hp)),
        out_shape=jax.ShapeDtypeStruct((BL, D), BF16),
        compiler_params=_cparams(("parallel", "parallel")),
        name="ctx_attention",
    )(qkv, qkv, qkv)


def _gdn_feat_kernel(cur_ref, prev_ref, next_ref, cw_ref, o_ref, *, tm, BL, L, N):
    i = pl.program_id(0)
    j = pl.program_id(1)
    r0 = i * tm
    in_ctx = r0 < BL
    seg = jnp.where(in_ctx, L, N)
    off = jnp.where(in_ctx, r0, r0 - BL)
    keep_prev = jnp.where((off % seg) == 0, 0.0, 1.0)
    keep_next = jnp.where(((off + tm) % seg) == 0, 0.0, 1.0)
    row = lax.broadcasted_iota(jnp.int32, (tm, LANE), 0)
    qk_scale = jnp.where(j == 0, GDN_DK ** -0.5, 1.0)
    for c in range(cur_ref.shape[1] // LANE):
        sl = slice(c * LANE, (c + 1) * LANE)
        x = cur_ref[:, sl].astype(F32)
        pv = prev_ref[:, sl].astype(F32) * keep_prev
        nx = next_ref[:, sl].astype(F32) * keep_next
        w = cw_ref[:, sl]
        xm1 = jnp.where(row == 0, pv[15:16], pltpu.roll(x, 1, 0))
        xm2 = jnp.where(row == 0, pv[14:15], jnp.where(row == 1, pv[15:16], pltpu.roll(x, 2, 0)))
        xp1 = jnp.where(row == tm - 1, nx[0:1], pltpu.roll(x, tm - 1, 0))
        y = _silu(w[0:1] * xm2 + w[1:2] * xm1 + w[2:3] * x + w[3:4] * xp1)
        nrm = y * lax.rsqrt(jnp.sum(y * y, axis=-1, keepdims=True) + NORM_EPS) * qk_scale
        o_ref[:, sl] = jnp.where(j == 2, y, nrm).astype(o_ref.dtype)


def gdn_features(proj, conv_w, dims, L, tm=256):
    BL, N, B, D = dims
    M = proj.shape[0]
    C = GDN_HEADS * GDN_DK
    hb = tm // 16
    nhb = M // 16
    return pl.pallas_call(
        functools.partial(_gdn_feat_kernel, tm=tm, BL=BL, L=L, N=N),
        grid=(M // tm, 3),
        in_specs=[pl.BlockSpec((tm, C), lambda i, j: (i, j)),
                  pl.BlockSpec((16, C), lambda i, j: (jnp.maximum(i * hb - 1, 0), j)),
                  pl.BlockSpec((16, C), lambda i, j: (jnp.minimum((i + 1) * hb, nhb - 1), j)),
                  pl.BlockSpec((4, C), lambda i, j: (0, j))],
        out_specs=pl.BlockSpec((tm, C), lambda i, j: (i, j)),
        out_shape=jax.ShapeDtypeStruct((M, 3 * C), BF16),
        compiler_params=_cparams(("parallel", "arbitrary")),
        name="gdn_features",
    )(proj, proj, proj, conv_w)


def _gdn_intra_kernel(f_ref, ab_ref, pa_ref, pdt_ref, lm_ref, sm_ref,
                      w_ref, qd_ref, kt_ref, u_ref, aqk_ref, egl_ref, *, tm, cpg):
    d = pl.program_id(1)
    C = GDN_CHUNK
    H = GDN_HEADS
    DK = GDN_DK
    Lm = lm_ref[0]
    incl = Lm > 0.5
    strict = sm_ref[0] > 0.5
    eye = (lax.broadcasted_iota(jnp.int32, (C, C), 0) == lax.broadcasted_iota(jnp.int32, (C, C), 1)).astype(F32)
    lane = lax.broadcasted_iota(jnp.int32, (C, LANE), 1)
    pa = pa_ref[0]
    pdt = pdt_ref[0]

    def mm(a, b):
        return jnp.dot(a.astype(BF16), b.astype(BF16), preferred_element_type=F32)

    def nt(a, b):
        return lax.dot_general(a.astype(BF16), b.astype(BF16), NT, preferred_element_type=F32)

    def chunk_group(cg, carry):
        pairs = [(cc, h) for cc in range(cpg) for h in range(H)]
        rows = [pl.multiple_of((cg * cpg + cc) * C, C) for cc in range(cpg)]
        gc_all, gc_t, beta_all, g_last = [], [], [], []
        for cc in range(cpg):
            ab = ab_ref[pl.ds(rows[cc], C), :]
            sp = jnp.maximum(ab + pdt, 0.0) + jnp.log1p(jnp.exp(-jnp.abs(ab + pdt)))
            gval = jnp.where(lane < H, pa * sp, 0.0)
            ga = jnp.dot(Lm, gval, precision=HI, preferred_element_type=F32)
            gc_all.append(ga)
            gc_t.append(ga.T)
            beta_all.append(jax.nn.sigmoid(ab))
            g_last.append(jnp.where(d == 0, ga[C - 1:C], ga[0:1]))
        ld = lambda cc, col: f_ref[pl.ds(rows[cc], C), col * DK:(col + 1) * DK]
        q16 = [ld(cc, h) for cc, h in pairs]
        k16 = [ld(cc, H + h) for cc, h in pairs]
        k = [t.astype(F32) for t in k16]
        gc = [jnp.broadcast_to(gc_all[cc][:, h:h + 1], (C, DK)) for cc, h in pairs]
        beta = [jnp.broadcast_to(beta_all[cc][:, H + h:H + h + 1], (C, DK)) for cc, h in pairs]
        decay = []
        for i, (cc, h) in enumerate(pairs):
            diff = gc[i][:, :C] - gc_t[cc][h:h + 1, :]
            decay.append(jnp.where(incl, jnp.exp(jnp.where(incl, diff, 0.0)), 0.0))
        kb = [k[i] * beta[i] for i in range(len(pairs))]
        nm = [jnp.where(strict, nt(kb[i], k16[i]) * decay[i], 0.0) for i in range(len(pairs))]
        xinv = [eye - t for t in nm]
        pw = [mm(t, t) for t in nm]
        for it in range(5):
            if it < 4:
                z = [mm(jnp.concatenate([xinv[i], pw[i]], axis=0), pw[i]) for i in range(len(pairs))]
                xinv = [xinv[i] + z[i][:C] for i in range(len(pairs))]
                pw = [t[C:] for t in z]
            else:
                xinv = [xinv[i] + mm(xinv[i], pw[i]) for i in range(len(pairs))]
        eg = [jnp.exp(t) for t in gc]
        uw = [mm(xinv[i], jnp.concatenate([ld(cc, 2 * H + h).astype(F32) * beta[i], kb[i] * eg[i]], axis=1))
              for i, (cc, h) in enumerate(pairs)]
        u = [t[:, :DK] for t in uw]
        w = [t[:, DK:] for t in uw]
        aqk = [nt(q16[i], k16[i]) * decay[i] for i in range(len(pairs))]
        for i, (cc, h) in enumerate(pairs):
            sl = slice(h * DK, (h + 1) * DK)
            r = rows[cc]
            gl = jnp.broadcast_to(g_last[cc][:, h:h + 1], (1, DK))
            w_ref[0, pl.ds(r, C), sl] = w[i].astype(w_ref.dtype)
            u_ref[0, pl.ds(r, C), sl] = u[i]
            qd_ref[0, pl.ds(r, C), sl] = (q16[i].astype(F32) * eg[i]).astype(qd_ref.dtype)
            kt_ref[0, pl.ds(r, C), sl] = (k[i] * jnp.exp(gl - gc[i])).astype(kt_ref.dtype)
            aqk_ref[0, pl.ds(r, C), h * C:(h + 1) * C] = aqk[i].astype(aqk_ref.dtype)
            egl_ref[0, cg * cpg + cc, h:h + 1, :] = jnp.exp(gl)
        return carry

    lax.fori_loop(0, tm // (C * cpg), chunk_group, 0)


def gdn_intra(feats, ab, a_log, dt_bias, tm=256):
    M = feats.shape[0]
    H, DK, C = GDN_HEADS, GDN_DK, GDN_CHUNK
    HD = H * DK
    pa = jnp.zeros((2, 1, LANE), F32).at[:, 0, :H].set(-jnp.exp(a_log.astype(F32)))
    pdt = jnp.zeros((2, 1, LANE), F32).at[:, 0, :H].set(dt_bias.astype(F32))
    idx = np.arange(C)
    lower = idx[:, None] >= idx[None, :]
    lm = jnp.asarray(np.stack([lower, lower.T]).astype(np.float32))
    sm = jnp.asarray(np.stack([idx[:, None] > idx[None, :], idx[:, None] < idx[None, :]]).astype(np.float32))
    big = lambda dt: jax.ShapeDtypeStruct((2, M, HD), dt)
    dspec = pl.BlockSpec((1, tm, HD), lambda i, d: (d, i, 0))
    return pl.pallas_call(
        functools.partial(_gdn_intra_kernel, tm=tm, cpg=4),
        grid=(M // tm, 2),
        in_specs=[pl.BlockSpec((tm, 3 * HD), lambda i, d: (i, 0)),
                  pl.BlockSpec((tm, LANE), lambda i, d: (i, d)),
                  pl.BlockSpec((1, 1, LANE), lambda i, d: (d, 0, 0)),
                  pl.BlockSpec((1, 1, LANE), lambda i, d: (d, 0, 0)),
                  pl.BlockSpec((1, C, C), lambda i, d: (d, 0, 0)),
                  pl.BlockSpec((1, C, C), lambda i, d: (d, 0, 0))],
        out_specs=[dspec, dspec, dspec, dspec,
                   pl.BlockSpec((1, tm, H * C), lambda i, d: (d, i, 0)),
                   pl.BlockSpec((1, tm // C, H, LANE), lambda i, d: (d, i, 0, 0))],
        out_shape=[big(BF16), big(BF16), big(BF16), big(F32),
                   jax.ShapeDtypeStruct((2, M, H * C), BF16),
                   jax.ShapeDtypeStruct((2, M // C, H, LANE), F32)],
        compiler_params=_cparams(("parallel", "arbitrary")),
        name="gdn_intra",
    )(feats, ab, pa, pdt, lm, sm)


def _gdn_scan_kernel(*refs, nch):
    ins = (refs[0:6], refs[6:12])
    outs = refs[12:14]
    s_ref = refs[14]
    C = GDN_CHUNK
    DK = GDN_DK
    H = GDN_HEADS

    @pl.when(pl.program_id(1) == 0)
    def _():
        s_ref[...] = jnp.zeros_like(s_ref)

    def chunk(j, carry):
        cs = (j, nch - 1 - j)
        rs = tuple(pl.multiple_of(c * C, C) for c in cs)
        pairs = [(d, h) for d in range(2) for h in range(H)]
        sl = lambda h: slice(h * DK, (h + 1) * DK)
        ld = lambda k, d, h: ins[d][k][0, pl.ds(rs[d], C), sl(h)]
        S = [s_ref[d, h] for d, h in pairs]
        Sb = [t.astype(BF16) for t in S]
        ws = [jnp.dot(ld(0, d, h), Sb[i], preferred_element_type=F32) for i, (d, h) in enumerate(pairs)]
        qs = [jnp.dot(ld(1, d, h), Sb[i], preferred_element_type=F32) for i, (d, h) in enumerate(pairs)]
        vb = [(ld(3, d, h) - ws[i]).astype(BF16) for i, (d, h) in enumerate(pairs)]
        av = [jnp.dot(ins[d][4][0, pl.ds(rs[d], C), h * C:(h + 1) * C], vb[i], preferred_element_type=F32)
              for i, (d, h) in enumerate(pairs)]
        kv = [lax.dot_general(ld(2, d, h), vb[i], TN, preferred_element_type=F32)
              for i, (d, h) in enumerate(pairs)]
        for i, (d, h) in enumerate(pairs):
            outs[d][pl.ds(rs[d], C), sl(h)] = qs[i] + av[i]
            s_ref[d, h] = S[i] * ins[d][5][0, cs[d], h:h + 1, :] + kv[i]
        return carry

    lax.fori_loop(0, nch, chunk, 0)


def gdn_scan(w, qd, kt, u, aqk, egl, dims, L):
    BL, N, B, D = dims
    M = w.shape[1]
    H, DK, C = GDN_HEADS, GDN_DK, GDN_CHUNK
    HD = H * DK
    blk = L
    nlat = N // blk
    assert N % blk == 0 and blk % C == 0

    def rb(d):
        def index(b, s):
            lat = BL // blk + b * nlat + (s - 1 if d == 0 else nlat - s)
            return jnp.where(s == 0, b, lat)
        return index

    in_specs, args = [], []
    for d in range(2):
        r = rb(d)
        big = pl.BlockSpec((1, blk, HD), lambda b, s, r=r, d=d: (d, r(b, s), 0))
        in_specs += [big, big, big, big,
                     pl.BlockSpec((1, blk, H * C), lambda b, s, r=r, d=d: (d, r(b, s), 0)),
                     pl.BlockSpec((1, blk // C, H, LANE), lambda b, s, r=r, d=d: (d, r(b, s), 0, 0))]
        args += [w, qd, kt, u, aqk, egl]
    return pl.pallas_call(
        functools.partial(_gdn_scan_kernel, nch=blk // C),
        grid=(B, 1 + nlat),
        in_specs=in_specs,
        out_specs=[pl.BlockSpec((blk, HD), lambda b, s, r=rb(d): (r(b, s), 0)) for d in range(2)],
        out_shape=[jax.ShapeDtypeStruct((M, HD), F32)] * 2,
        scratch_shapes=[pltpu.VMEM((2, H, DK, DK), F32)],
        compiler_params=_cparams(("parallel", "arbitrary")),
        name="gdn_scan",
    )(*args)


def _gdn_out_kernel(of_ref, ob_ref, z_ref, ng_ref, w_ref, x_ref, gate_ref, o_ref, a_ref):
    DK = GDN_DK
    for h in range(GDN_HEADS):
        sl = slice(h * DK, (h + 1) * DK)
        o = of_ref[:, sl] + ob_ref[:, sl]
        y = o * lax.rsqrt(jnp.mean(o * o, axis=-1, keepdims=True) + NORM_EPS) * ng_ref[...]
        z = z_ref[:, sl].astype(F32)
        a_ref[:, sl] = (y * _silu(z)).astype(BF16)
    y = jnp.dot(a_ref[...], w_ref[...], preferred_element_type=F32)
    o_ref[...] = x_ref[...] + gate_ref[0] * y


def gdn_output(o_f, o_b, proj, norm_g, w_o, xs, gate, dims, row_start=0, tm=512):
    M, D = xs.shape
    HD = GDN_HEADS * GDN_DK
    off = row_start // tm
    return pl.pallas_call(
        _gdn_out_kernel,
        grid=(M // tm - off,),
        in_specs=[pl.BlockSpec((tm, HD), lambda i: (i + off, 0)),
                  pl.BlockSpec((tm, HD), lambda i: (i + off, 0)),
                  pl.BlockSpec((tm, HD), lambda i: (i + off, 3)),
                  pl.BlockSpec((1, GDN_DK), lambda i: (0, 0)),
                  pl.BlockSpec((HD, D), lambda i: (0, 0)),
                  pl.BlockSpec((tm, D), lambda i: (i + off, 0)),
                  _mod_spec(tm, off, dims)],
        out_specs=pl.BlockSpec((tm, D), lambda i: (i + off, 0)),
        out_shape=jax.ShapeDtypeStruct((M, D), F32),
        scratch_shapes=[pltpu.VMEM((tm, HD), BF16)],
        input_output_aliases={5: 0},
        compiler_params=_cparams(("parallel",)),
        name="gdn_output",
    )(o_f, o_b, proj, norm_g.reshape(1, GDN_DK), w_o, xs, gate)


def _pool_tables(tm, L):
    amats, invs = [], []
    t = np.arange(tm)
    for seg in (L, GRID_W):
        a_v, i_v = [], []
        tl = t % seg
        for win in POOL_WINDOWS:
            lo = np.clip(tl - win // 2, 0, seg)
            hi = np.clip(tl + win // 2, 0, seg)
            same = (t[:, None] // seg) == (t[None, :] // seg)
            a = same & (tl[None, :] >= lo[:, None]) & (tl[None, :] < hi[:, None])
            a_v.append(a.astype(np.float32))
            i_v.append(np.broadcast_to((1.0 / (hi - lo))[:, None], (tm, LANE)).astype(np.float32))
        amats.append(np.stack(a_v))
        invs.append(np.stack(i_v))
    return jnp.asarray(np.stack(amats), BF16), jnp.asarray(np.stack(invs), F32)


def _pool_kernel(x_ref, g_ref, sh_ref, sc_ref, gate_ref, a_ref, ic_ref, pw_ref, ls_ref, o_ref):
    x = x_ref[...]
    h = _prenorm(x, g_ref[...], sh_ref[0], sc_ref[0])
    G = pw_ref.shape[1]
    for gi in range(len(POOL_WINDOWS)):
        sl = slice(gi * G, (gi + 1) * G)
        hg = h[:, sl]
        hi = hg.astype(BF16)
        lo = (hg - hi.astype(F32)).astype(BF16)
        am = a_ref[0, gi]
        wsum = jnp.dot(am, hi, preferred_element_type=F32) + jnp.dot(am, lo, preferred_element_type=F32)
        ic = ic_ref[0, gi]
        mean = wsum * jnp.concatenate([ic] * (G // LANE), axis=-1)
        pooled = (mean - hg).astype(BF16)
        y = jnp.dot(pooled, pw_ref[gi], preferred_element_type=F32) * ls_ref[:, sl]
        o_ref[:, sl] = x[:, sl] + gate_ref[0][:, sl] * y


def pool_sublayer(xs, g, shift, scale, gate, pool_w, ls, dims, L, row_start=0, tm=256):
    BL, N, B, D = dims
    M = xs.shape[0]
    assert tm == L and tm % GRID_W == 0
    amat, inv = _pool_tables(tm, L)
    off = row_start // tm
    nct = BL // tm
    G = D // len(POOL_WINDOWS)
    variant = lambda i: jnp.where(i + off < nct, 0, 1)
    return pl.pallas_call(
        _pool_kernel,
        grid=(M // tm - off,),
        in_specs=[pl.BlockSpec((tm, D), lambda i: (i + off, 0)),
                  pl.BlockSpec((1, D), lambda i: (0, 0)),
                  _mod_spec(tm, off, dims), _mod_spec(tm, off, dims), _mod_spec(tm, off, dims),
                  pl.BlockSpec((1, 4, tm, tm), lambda i: (variant(i), 0, 0, 0)),
                  pl.BlockSpec((1, 4, tm, LANE), lambda i: (variant(i), 0, 0, 0)),
                  pl.BlockSpec((4, G, G), lambda i: (0, 0, 0)),
                  pl.BlockSpec((1, D), lambda i: (0, 0))],
        out_specs=pl.BlockSpec((tm, D), lambda i: (i + off, 0)),
        out_shape=jax.ShapeDtypeStruct((M, D), F32),
        input_output_aliases={0: 0},
        compiler_params=_cparams(("parallel",)),
        name="pool_sublayer",
    )(xs, g.reshape(1, D), shift, scale, gate, amat, inv, pool_w.astype(BF16), ls.reshape(1, D))


def _router_kernel(x_ref, g_ref, sh_ref, sc_ref, wr_ref, h_ref, ti_ref, tp_ref):
    h = _prenorm(x_ref[...], g_ref[...], sh_ref[0], sc_ref[0])
    h_ref[...] = h
    h_hi = h.astype(BF16)
    h_lo = (h - h_hi.astype(F32)).astype(BF16)
    wr = wr_ref[...]
    both = jnp.dot(h_hi, wr, preferred_element_type=F32)
    logits = both[:, :LANE] + both[:, LANE:] + jnp.dot(h_lo, wr[:, :LANE], preferred_element_type=F32)
    lane = lax.broadcasted_iota(jnp.int32, logits.shape, 1)
    valid = lane < N_EXPERTS
    lg = jnp.where(valid, logits, NEG_INF)
    e = jnp.where(valid, jnp.exp(lg - jnp.max(lg, axis=-1, keepdims=True)), 0.0)
    probs = jnp.where(valid, e / jnp.sum(e, axis=-1, keepdims=True), -1.0)
    p1 = jnp.max(probs, axis=-1, keepdims=True)
    i1 = jnp.min(jnp.where(probs == p1, lane, LANE), axis=-1, keepdims=True)
    rest = jnp.where(lane == i1, -1.0, probs)
    p2 = jnp.max(rest, axis=-1, keepdims=True)
    i2 = jnp.min(jnp.where(rest == p2, lane, LANE), axis=-1, keepdims=True)
    tot = p1 + p2
    tp_ref[...] = jnp.where(lane == 0, p1 / tot, jnp.where(lane == 1, p2 / tot, 0.0))
    ti_ref[...] = jnp.where(lane == 0, i1, jnp.where(lane == 1, i2, 0))


def moe_router(xs, g, shift, scale, w_router, dims, row_start=0, tm=512):
    M, D = xs.shape
    off = row_start // tm
    wr = jnp.zeros((D, LANE), F32).at[:, :N_EXPERTS].set(w_router)
    wr_hi = wr.astype(BF16)
    wr = jnp.concatenate([wr_hi, (wr - wr_hi.astype(F32)).astype(BF16)], axis=1)
    Mo = M - row_start
    ospec = lambda w: pl.BlockSpec((tm, w), lambda i: (i, 0))
    return pl.pallas_call(
        _router_kernel,
        grid=(Mo // tm,),
        in_specs=[pl.BlockSpec((tm, D), lambda i: (i + off, 0)),
                  pl.BlockSpec((1, D), lambda i: (0, 0)),
                  _mod_spec(tm, off, dims), _mod_spec(tm, off, dims),
                  pl.BlockSpec((D, 2 * LANE), lambda i: (0, 0))],
        out_specs=[ospec(D), ospec(LANE), ospec(LANE)],
        out_shape=[jax.ShapeDtypeStruct((Mo, D), F32),
                   jax.ShapeDtypeStruct((Mo, LANE), jnp.int32),
                   jax.ShapeDtypeStruct((Mo, LANE), F32)],
        compiler_params=_cparams(("parallel",)),
        name="moe_router",
    )(xs, g.reshape(1, D), shift, scale, wr)


def _route_positions(top_e):
    n_tok = top_e.shape[0]
    e_flat = top_e.reshape(-1)
    onehot = (e_flat[:, None] == jnp.arange(N_EXPERTS, dtype=jnp.int32)[None, :]).astype(jnp.int32)
    csum = jnp.cumsum(onehot, axis=0)
    rank = jnp.sum(csum * onehot, axis=-1) - 1
    counts = csum[-1]
    padded = ((counts + MOE_BLOCK - 1) // MOE_BLOCK) * MOE_BLOCK
    ends = jnp.cumsum(padded)
    starts = ends - padded
    pos = (jnp.sum(onehot * starts[None, :], axis=-1) + rank).reshape(n_tok, TOP_K)
    n_blocks = -(-(n_tok * TOP_K) // MOE_BLOCK) + N_EXPERTS
    block_e = jnp.minimum(jnp.searchsorted(ends, jnp.arange(n_blocks) * MOE_BLOCK, side='right'),
                          N_EXPERTS - 1).astype(jnp.int32)
    pos = pos.astype(jnp.int32)
    cap = n_blocks * MOE_BLOCK
    n_pairs = n_tok * TOP_K
    pair_row = (jnp.arange(TOP_K, dtype=jnp.int32)[None, :] * n_tok
                + jnp.arange(n_tok, dtype=jnp.int32)[:, None]).reshape(-1)
    dst = (n_pairs + jnp.arange(cap, dtype=jnp.int32)).at[pos.reshape(-1)].set(pair_row, unique_indices=True)
    src = jnp.where(dst < n_pairs, dst % n_tok, 0)
    lead = n_pairs + cap + jnp.arange(MOE_BLOCK, dtype=jnp.int32)
    dst = jnp.concatenate([lead, dst])
    n_used = (ends[-1] // MOE_BLOCK).astype(jnp.int32).reshape(1)
    return src, dst, block_e, n_used, n_blocks


def _expert_kernel(be_ref, nu_ref, tokc_ref, tokn_ref, dstp_ref, dstc_ref, h_hbm, w1_ref, w3_ref, w2_ref,
                   y_hbm, xg_ref, ob_ref, a_ref, sem_g, sem_o, *, tf, n_blocks):
    del be_ref
    i = pl.program_id(0)
    n_used = nu_ref[0]
    slot = i % 2
    other = 1 - slot
    F = w1_ref.shape[2]
    nchunk = F // tf
    per = -(-MOE_BLOCK // nchunk)

    def gather_copy(tok_ref, s, t):
        return pltpu.make_async_copy(h_hbm.at[pl.ds(tok_ref[0, 0, t], 1), :],
                                     xg_ref.at[s, pl.ds(t, 1), :], sem_g.at[s])

    def scatter_copy(dst_ref, s, t):
        return pltpu.make_async_copy(ob_ref.at[s, pl.ds(t, 1), :],
                                     y_hbm.at[pl.ds(dst_ref[0, 0, t], 1), :], sem_o.at[s])

    def start_loop(make, ref, s):
        def body(t, carry):
            make(ref, s, t).start()
            return carry
        lax.fori_loop(0, MOE_BLOCK, body, 0, unroll=8)

    def wait_gather(s):
        pltpu.make_async_copy(h_hbm.at[pl.ds(0, MOE_BLOCK), :], xg_ref.at[s], sem_g.at[s]).wait()

    def wait_scatter(s):
        pltpu.make_async_copy(ob_ref.at[s], y_hbm.at[pl.ds(0, MOE_BLOCK), :], sem_o.at[s]).wait()

    @pl.when(i == 0)
    def _():
        ob_ref[...] = jnp.zeros_like(ob_ref)
        start_loop(gather_copy, tokc_ref, 0)

    wait_gather(slot)

    @pl.when(i >= 1)
    def _():
        wait_scatter(slot)

    @pl.when(i < n_used)
    def _():
        hb = xg_ref[slot].astype(BF16)
        for c in range(nchunk):
            cs = slice(c * tf, (c + 1) * tf)
            gg = jnp.dot(hb, w1_ref[0, :, cs], preferred_element_type=F32)
            uu = jnp.dot(hb, w3_ref[0, :, cs], preferred_element_type=F32)
            a_ref[:, cs] = (_silu(gg) * uu).astype(BF16)
            for t in range(c * per, min((c + 1) * per, MOE_BLOCK)):
                gather_copy(tokn_ref, other, t).start()
                scatter_copy(dstp_ref, other, t).start()
        ob_ref[slot] = jnp.dot(a_ref[...], w2_ref[0], preferred_element_type=F32)

    @pl.when(i >= n_used)
    def _():
        start_loop(gather_copy, tokn_ref, other)
        start_loop(scatter_copy, dstp_ref, other)
        ob_ref[slot] = jnp.zeros(ob_ref.shape[1:], F32)

    @pl.when(i == n_blocks - 1)
    def _():
        start_loop(scatter_copy, dstc_ref, slot)
        wait_scatter(other)
        wait_scatter(slot)
        wait_gather(other)


def moe_experts(h, src, dst, block_e, n_used, w1, w3, w2, n_blocks, tf=256):
    n_tok, D = h.shape
    F = w1.shape[2]
    last = n_blocks - 1
    smem = lambda index: pl.BlockSpec((1, 1, MOE_BLOCK), index, memory_space=pltpu.SMEM)
    resident = dict(pipeline_mode=pl.Buffered(1))
    grid_spec = pltpu.PrefetchScalarGridSpec(
        num_scalar_prefetch=2,
        grid=(n_blocks,),
        in_specs=[smem(lambda i, be, nu: (i, 0, 0)),
                  smem(lambda i, be, nu: (jnp.minimum(i + 1, last), 0, 0)),
                  smem(lambda i, be, nu: (i, 0, 0)),
                  smem(lambda i, be, nu: (i + 1, 0, 0)),
                  pl.BlockSpec(memory_space=pl.ANY),
                  pl.BlockSpec((1, D, F), lambda i, be, nu: (be[i], 0, 0), **resident),
                  pl.BlockSpec((1, D, F), lambda i, be, nu: (be[i], 0, 0), **resident),
                  pl.BlockSpec((1, F, D), lambda i, be, nu: (be[i], 0, 0), **resident)],
        out_specs=pl.BlockSpec(memory_space=pl.ANY),
        scratch_shapes=[pltpu.VMEM((2, MOE_BLOCK, D), F32), pltpu.VMEM((2, MOE_BLOCK, D), F32),
                        pltpu.VMEM((MOE_BLOCK, F), BF16),
                        pltpu.SemaphoreType.DMA((2,)), pltpu.SemaphoreType.DMA((2,))],
    )
    src3 = src.reshape(n_blocks, 1, MOE_BLOCK)
    dst3 = dst.reshape(n_blocks + 1, 1, MOE_BLOCK)
    n_rows = n_tok * TOP_K + (n_blocks + 1) * MOE_BLOCK
    return pl.pallas_call(
        functools.partial(_expert_kernel, tf=tf, n_blocks=n_blocks),
        grid_spec=grid_spec,
        out_shape=jax.ShapeDtypeStruct((n_rows, D), F32),
        compiler_params=_cparams(("arbitrary",)),
        name="moe_experts",
    )(block_e, n_used, src3, src3, dst3, dst3, h, w1, w3, w2)


def _combine_kernel(y0_ref, y1_ref, tp_ref, x_ref, gate_ref, o_ref):
    tp = tp_ref[...]
    y = tp[:, 0:1] * y0_ref[...] + tp[:, 1:2] * y1_ref[...]
    o_ref[...] = x_ref[...] + gate_ref[0] * y


def _combine_norm_kernel(y0_ref, y1_ref, tp_ref, x_ref, gate_ref, fg_ref, o_ref):
    tp = tp_ref[...]
    y = tp[:, 0:1] * y0_ref[...] + tp[:, 1:2] * y1_ref[...]
    x = x_ref[...] + gate_ref[0] * y
    o_ref[...] = x * lax.rsqrt(jnp.mean(x * x, axis=-1, keepdims=True) + NORM_EPS) * fg_ref[...]


def moe_combine(y, top_p, xs, gate, dims, row_start=0, final_g=None, tm=512):
    M, D = xs.shape
    off = row_start // tm
    nt = (M - row_start) // tm
    in_specs = [pl.BlockSpec((tm, D), lambda i: (i, 0)),
                pl.BlockSpec((tm, D), lambda i: (nt + i, 0)),
                pl.BlockSpec((tm, LANE), lambda i: (i, 0)),
                pl.BlockSpec((tm, D), lambda i: (i + off, 0)),
                _mod_spec(tm, off, dims)]
    if final_g is None:
        return pl.pallas_call(
            _combine_kernel,
            grid=(nt,),
            in_specs=in_specs,
            out_specs=pl.BlockSpec((tm, D), lambda i: (i + off, 0)),
            out_shape=jax.ShapeDtypeStruct((M, D), F32),
            input_output_aliases={3: 0},
            compiler_params=_cparams(("parallel",)),
            name="moe_combine",
        )(y, y, top_p, xs, gate)
    return pl.pallas_call(
        _combine_norm_kernel,
        grid=(nt,),
        in_specs=in_specs + [pl.BlockSpec((1, D), lambda i: (0, 0))],
        out_specs=pl.BlockSpec((tm, D), lambda i: (i, 0)),
        out_shape=jax.ShapeDtypeStruct((M - row_start, D), F32),
        compiler_params=_cparams(("parallel",)),
        name="moe_combine_norm",
    )(y, y, top_p, xs, gate, final_g.reshape(1, D))


def moe_sublayer(xs, g, shift, scale, gate, w_router, w1, w3, w2, dims, row_start, final_g=None):
    h, top_i, top_p = moe_router(xs, g, shift, scale, w_router, dims, row_start)
    src, dst, block_e, n_used, n_blocks = _route_positions(top_i[:, :TOP_K])
    y = moe_experts(h, src, dst, block_e, n_used, w1, w3, w2, n_blocks)
    return moe_combine(y, top_p, xs, gate, dims, row_start, final_g)


def _final_norm_kernel(x_ref, g_ref, o_ref):
    x = x_ref[...]
    o_ref[...] = x * lax.rsqrt(jnp.mean(x * x, axis=-1, keepdims=True) + NORM_EPS) * g_ref[...]


def final_norm(xs, g, row_start, tm=512):
    M, D = xs.shape
    off = row_start // tm
    return pl.pallas_call(
        _final_norm_kernel,
        grid=((M - row_start) // tm,),
        in_specs=[pl.BlockSpec((tm, D), lambda i: (i + off, 0)),
                  pl.BlockSpec((1, D), lambda i: (0, 0))],
        out_specs=pl.BlockSpec((tm, D), lambda i: (i, 0)),
        out_shape=jax.ShapeDtypeStruct((M - row_start, D), F32),
        compiler_params=_cparams(("parallel",)),
        name="final_norm",
    )(xs, g.reshape(1, D))


def na_mixer(xs, g, m, w_qkv, w_o, rpb, dims, L, need_ctx):
    BL, N, B, D = dims
    col_scale = jnp.concatenate([jnp.full((D,), NA_HEAD_DIM ** -0.5 * LOG2E, F32), jnp.ones((2 * D,), F32)])
    qkv = prenorm_matmul(xs, g, m[0], m[1], w_qkv.astype(BF16), dims, col_scale=col_scale)
    bias_tab = _na_bias_table(rpb.astype(F32))
    w_o = w_o.astype(BF16)
    if need_ctx:
        xs = matmul_residual(ctx_attention(qkv, dims, L), w_o, xs, m[2], dims, row_start=0)
    return matmul_residual(na_attention(qkv, bias_tab, dims, L), w_o, xs, m[2], dims, row_start=BL)


def gdn_mixer(xs, g, m, w_in, conv_w, a_log, dt_bias, norm_g, w_o, dims, L, need_ctx):
    BL, N, B, D = dims
    H = GDN_HEADS
    HD = H * GDN_DK
    w_main = w_in[:, :4 * HD].astype(BF16)
    w_gate = w_in[:, 4 * HD:].astype(F32)
    w_ab = jnp.zeros((D, 2 * LANE), F32)
    for d in range(2):
        w_ab = w_ab.at[:, d * LANE:d * LANE + H].set(w_gate[:, d * H:(d + 1) * H])
        w_ab = w_ab.at[:, d * LANE + H:d * LANE + 2 * H].set(w_gate[:, (2 + d) * H:(3 + d) * H])
    proj, ab = prenorm_matmul(xs, g, m[0], m[1], w_main, dims, w_hi=w_ab)
    feats = gdn_features(proj, conv_w.astype(F32), dims, L)
    w, qd, kt, u, aqk, egl = gdn_intra(feats, ab, a_log, dt_bias)
    o_f, o_b = gdn_scan(w, qd, kt, u, aqk, egl, dims, L)
    return gdn_output(o_f, o_b, proj, norm_g.astype(F32), w_o.astype(BF16), xs, m[2], dims,
                      row_start=0 if need_ctx else BL)


def kernel(x, c, ctx, c_ctx, ada_w, ada_b, norm_g, final_g, na_w_qkv, na_w_o, na_rpb, gdn_w_in, gdn_conv, gdn_a_log, gdn_dt_bias, gdn_norm_g, gdn_w_o, pool_w, pool_scale, ffn_w1, ffn_w3, ffn_w2, moe_router, moe_w1, moe_w3, moe_w2):
    B, N, D = x.shape
    L = ctx.shape[1]
    depth = ada_w.shape[0]
    BL = B * L
    dims = (BL, N, B, D)
    xs = jnp.concatenate([ctx.reshape(BL, D), x.reshape(B * N, D)], axis=0)

    R = -(-(B + 1) // 8) * 8
    cvec = jnp.zeros((R, D), F32).at[:B].set(c).at[B].set(c_ctx)
    mods = ada_table(cvec, ada_w, ada_b)[:, :B + 1].reshape(depth, B + 1, 6, 1, D)

    for i in range(depth):
        last = i == depth - 1
        m = [mods[i, :, k] for k in range(6)]
        j = i // 3
        kind = i % 3
        if kind == 0:
            xs = na_mixer(xs, norm_g[i, 0], m, na_w_qkv[j], na_w_o[j], na_rpb[j], dims, L, not last)
        elif kind == 1:
            xs = gdn_mixer(xs, norm_g[i, 0], m, gdn_w_in[j], gdn_conv[j], gdn_a_log[j], gdn_dt_bias[j],
                           gdn_norm_g[j], gdn_w_o[j], dims, L, not last)
        else:
            xs = pool_sublayer(xs, norm_g[i, 0], m[0], m[1], m[2], pool_w[j], pool_scale[j], dims, L,
                               row_start=0 if not last else BL)
        row_start = BL if last else 0
        f = i // 2
        if i % 2 == 0:
            xs = ffn_sublayer(xs, norm_g[i, 1], m[3], m[4], m[5], ffn_w1[f].astype(BF16),
                              ffn_w3[f].astype(BF16), ffn_w2[f].astype(BF16), dims, row_start=row_start)
        else:
            xs = moe_sublayer(xs, norm_g[i, 1], m[3], m[4], m[5], moe_router[f], moe_w1[f].astype(BF16),
                              moe_w3[f].astype(BF16), moe_w2[f].astype(BF16), dims, row_start,
                              final_g=final_g if last else None)
            if last:
                return xs.reshape(B, N, D)
    return final_norm(xs, final_g, BL).reshape(B, N, D)
```

```python
import functools

import numpy as np
import jax
import jax.numpy as jnp
from jax import lax
from jax.experimental import pallas as pl
from jax.experimental.pallas import tpu as pltpu

F32 = jnp.float32
BF16 = jnp.bfloat16
HI = lax.Precision.HIGHEST

NORM_EPS = 1e-6
NEG_INF = -1e30
LOG2E = 1.4426950408889634
GRID_W = 64
NA_HEADS = 16
NA_HEAD_DIM = 64
NA_ROWS = 8
NA_COLS = 16
GDN_HEADS = 8
GDN_DK = 128
GDN_CHUNK = 64
POOL_WINDOWS = (2, 4, 8, 16)
N_EXPERTS = 8
TOP_K = 2
MOE_BLOCK = 512
LANE = 128
V7X_VMEM_LIMIT = 56 * 1024 * 1024

NT = (((1,), (1,)), ((), ()))
TN = (((0,), (0,)), ((), ()))


def _cparams(sem, vmem=V7X_VMEM_LIMIT):
    return pltpu.CompilerParams(dimension_semantics=sem, vmem_limit_bytes=vmem)


def _silu(v):
    return v * jax.nn.sigmoid(v)


def _prenorm(x, g, shift, scale):
    ms = jnp.mean(x * x, axis=-1, keepdims=True)
    y = x * lax.rsqrt(ms + NORM_EPS) * g
    return y * (1.0 + scale) + shift


def _mod_index(i, tm, BL, N, B):
    nct = BL // tm
    return jnp.where(i < nct, B, (i - nct) // (N // tm))


def _mod_spec(tm, off, dims):
    BL, N, B, D = dims
    return pl.BlockSpec((1, 1, D), lambda i, *_: (_mod_index(i + off, tm, BL, N, B), 0, 0))


def _ada_kernel(c_ref, w_ref, b_ref, o_ref):
    cv = c_ref[...]
    o_ref[0] = jnp.dot(_silu(cv), w_ref[0], precision=HI, preferred_element_type=F32) + b_ref[0]


def ada_table(cvec, ada_w, ada_b):
    depth, D, D6 = ada_w.shape
    R = cvec.shape[0]
    tn = 1536
    return pl.pallas_call(
        _ada_kernel,
        grid=(depth, D6 // tn),
        in_specs=[pl.BlockSpec((R, D), lambda l, j: (0, 0)),
                  pl.BlockSpec((1, D, tn), lambda l, j: (l, 0, j)),
                  pl.BlockSpec((1, 1, tn), lambda l, j: (l, 0, j))],
        out_specs=pl.BlockSpec((1, R, tn), lambda l, j: (l, 0, j)),
        out_shape=jax.ShapeDtypeStruct((depth, R, D6), F32),
        compiler_params=_cparams(("arbitrary", "arbitrary")),
        name="ada_table",
    )(cvec, ada_w, ada_b.reshape(depth, 1, D6))


def _prenorm_matmul_kernel(x_ref, g_ref, sh_ref, sc_ref, w_ref, cs_ref, o_ref):
    h = _prenorm(x_ref[...], g_ref[...], sh_ref[0], sc_ref[0]).astype(BF16)
    y = jnp.dot(h, w_ref[...], preferred_element_type=F32)
    o_ref[...] = (y * cs_ref[...]).astype(o_ref.dtype)


def _prenorm_matmul2_kernel(x_ref, g_ref, sh_ref, sc_ref, w_ref, w2_ref, o_ref, o2_ref):
    h = _prenorm(x_ref[...], g_ref[...], sh_ref[0], sc_ref[0])
    o_ref[...] = jnp.dot(h.astype(BF16), w_ref[...], preferred_element_type=F32).astype(o_ref.dtype)
    o2_ref[...] = jnp.dot(h, w2_ref[...], precision=HI, preferred_element_type=F32)


def prenorm_matmul(xs, g, shift, scale, w, dims, col_scale=None, w_hi=None, tm=512):
    M, D = xs.shape
    Nout = w.shape[1]
    specs = [pl.BlockSpec((tm, D), lambda i: (i, 0)),
             pl.BlockSpec((1, D), lambda i: (0, 0)),
             _mod_spec(tm, 0, dims), _mod_spec(tm, 0, dims),
             pl.BlockSpec((D, Nout), lambda i: (0, 0))]
    ospec = pl.BlockSpec((tm, Nout), lambda i: (i, 0))
    oshape = jax.ShapeDtypeStruct((M, Nout), BF16)
    if w_hi is None:
        cs = jnp.ones((1, Nout), F32) if col_scale is None else col_scale.reshape(1, Nout)
        return pl.pallas_call(
            _prenorm_matmul_kernel,
            grid=(M // tm,),
            in_specs=specs + [pl.BlockSpec((1, Nout), lambda i: (0, 0))],
            out_specs=ospec,
            out_shape=oshape,
            compiler_params=_cparams(("parallel",)),
            name="prenorm_matmul",
        )(xs, g.reshape(1, D), shift, scale, w, cs)
    N2 = w_hi.shape[1]
    return pl.pallas_call(
        _prenorm_matmul2_kernel,
        grid=(M // tm,),
        in_specs=specs + [pl.BlockSpec((D, N2), lambda i: (0, 0))],
        out_specs=[ospec, pl.BlockSpec((tm, N2), lambda i: (i, 0))],
        out_shape=[oshape, jax.ShapeDtypeStruct((M, N2), F32)],
        compiler_params=_cparams(("parallel",)),
        name="prenorm_matmul2",
    )(xs, g.reshape(1, D), shift, scale, w, w_hi)


def _matmul_res_kernel(a_ref, w_ref, x_ref, gate_ref, o_ref):
    y = jnp.dot(a_ref[...], w_ref[...], preferred_element_type=F32)
    o_ref[...] = x_ref[...] + gate_ref[0] * y


def matmul_residual(a, w, xs, gate, dims, row_start=0, tm=512):
    M, D = xs.shape
    K = a.shape[1]
    off = row_start // tm
    return pl.pallas_call(
        _matmul_res_kernel,
        grid=(a.shape[0] // tm,),
        in_specs=[pl.BlockSpec((tm, K), lambda i: (i, 0)),
                  pl.BlockSpec((K, D), lambda i: (0, 0)),
                  pl.BlockSpec((tm, D), lambda i: (i + off, 0)),
                  _mod_spec(tm, off, dims)],
        out_specs=pl.BlockSpec((tm, D), lambda i: (i + off, 0)),
        out_shape=jax.ShapeDtypeStruct((M, D), F32),
        input_output_aliases={2: 0},
        compiler_params=_cparams(("parallel",)),
        name="matmul_residual",
    )(a, w, xs, gate)


def _ffn_kernel(x_ref, g_ref, sh_ref, sc_ref, gate_ref, w1_ref, w3_ref, w2_ref, o_ref, a_ref, *, tf):
    x = x_ref[...]
    h = _prenorm(x, g_ref[...], sh_ref[0], sc_ref[0]).astype(BF16)
    F = w1_ref.shape[1]
    for f0 in range(0, F, tf):
        gg = jnp.dot(h, w1_ref[:, f0:f0 + tf], preferred_element_type=F32)
        uu = jnp.dot(h, w3_ref[:, f0:f0 + tf], preferred_element_type=F32)
        a_ref[:, f0:f0 + tf] = (_silu(gg) * uu).astype(BF16)
    y = jnp.dot(a_ref[...], w2_ref[...], preferred_element_type=F32)
    o_ref[...] = x + gate_ref[0] * y


def ffn_sublayer(xs, g, shift, scale, gate, w1, w3, w2, dims, row_start=0, tm=512, tf=256):
    M, D = xs.shape
    F = w1.shape[1]
    off = row_start // tm
    resident = dict(pipeline_mode=pl.Buffered(1))
    return pl.pallas_call(
        functools.partial(_ffn_kernel, tf=tf),
        grid=(M // tm - off,),
        in_specs=[pl.BlockSpec((tm, D), lambda i: (i + off, 0)),
                  pl.BlockSpec((1, D), lambda i: (0, 0)),
                  _mod_spec(tm, off, dims), _mod_spec(tm, off, dims), _mod_spec(tm, off, dims),
                  pl.BlockSpec((D, F), lambda i: (0, 0), **resident),
                  pl.BlockSpec((D, F), lambda i: (0, 0), **resident),
                  pl.BlockSpec((F, D), lambda i: (0, 0), **resident)],
        out_specs=pl.BlockSpec((tm, D), lambda i: (i + off, 0)),
        out_shape=jax.ShapeDtypeStruct((M, D), F32),
        scratch_shapes=[pltpu.VMEM((tm, F), BF16)],
        input_output_aliases={0: 0},
        compiler_params=_cparams(("parallel",)),
        name="ffn_sublayer",
    )(xs, g.reshape(1, D), shift, scale, gate, w1, w3, w2)


def _na_bias_table(rpb):
    H, n_dr, n_dc = rpb.shape
    col = np.arange(GRID_W)
    c0 = np.clip(col - NA_COLS // 2, 0, GRID_W - NA_COLS)
    in_win = (col[None, :] >= c0[:, None]) & (col[None, :] < c0[:, None] + NA_COLS)
    dc = np.clip(col[None, :] - col[:, None], -(NA_COLS - 1), NA_COLS - 1) + NA_COLS - 1
    var = np.arange(NA_ROWS)[:, None]
    t = np.arange(NA_ROWS)[None, :]
    dr = t - var + NA_ROWS - 1
    head = (2 * np.arange(H // 2)[:, None] + np.arange(2)[None, :])
    idx = (head[:, None, None, None, :, None] * n_dr + dr[None, :, :, None, None, None]) * n_dc \
        + dc.T[None, None, None, :, None, :]
    mask = np.broadcast_to(in_win.T[None, None, None, :, None, :], idx.shape)
    shape = (H // 2, NA_ROWS, NA_ROWS * GRID_W, 2 * GRID_W)
    tab = jnp.take(rpb.reshape(-1) * LOG2E, jnp.asarray(idx.reshape(shape), jnp.int32))
    return jnp.where(jnp.asarray(mask.reshape(shape)), tab, NEG_INF)


def _na_kernel(q_ref, k_ref, v_ref, kc_ref, vc_ref, bias_ref, o_ref, *, rows):
    W = GRID_W
    KW = NA_ROWS * W
    sub = lax.broadcasted_iota(jnp.int32, (2 * W, LANE), 0)
    lane = lax.broadcasted_iota(jnp.int32, (2 * W, LANE), 1)
    same_head = (sub < W) == (lane < W)
    lane_h = lax.broadcasted_iota(jnp.int32, (W, LANE), 1)
    kc = kc_ref[...]
    vc = vc_ref[...]

    kbase, qbase, var, scores, probs = {}, {}, {}, {}, {}

    def score_stage(rg, r):
        row = rg * NA_ROWS + r
        r0 = jnp.clip(row - NA_ROWS // 2, 0, rows - NA_ROWS)
        var[r] = row - r0
        kbase[r] = pl.multiple_of(r0 * W, W)
        qbase[r] = pl.multiple_of(row * W, W)
        qr = q_ref[pl.ds(qbase[r], W), :]
        q2 = jnp.concatenate([qr, qr], axis=0)
        q2 = jnp.where(same_head, q2, jnp.zeros_like(q2))
        kw = k_ref[pl.ds(kbase[r], KW), :]
        scores[r] = (lax.dot_general(kw, q2, NT, preferred_element_type=F32),
                     lax.dot_general(kc, q2, NT, preferred_element_type=F32))

    def softmax_stage(rg, r):
        s_loc, s_ctx = scores.pop(r)
        s_loc = s_loc + bias_ref[0, var[r]]
        m = jnp.maximum(jnp.max(s_loc, axis=0, keepdims=True), jnp.max(s_ctx, axis=0, keepdims=True))
        p_loc = jnp.exp2(s_loc - m)
        p_ctx = jnp.exp2(s_ctx - m)
        inv = 1.0 / (jnp.sum(p_loc, axis=0, keepdims=True) + jnp.sum(p_ctx, axis=0, keepdims=True))
        probs[r] = ((p_loc * inv).astype(BF16), (p_ctx * inv).astype(BF16))

    def value_stage(rg, r):
        p_loc, p_ctx = probs.pop(r)
        vw = v_ref[pl.ds(kbase[r], KW), :]
        o = (lax.dot_general(p_loc, vw, TN, preferred_element_type=F32)
             + lax.dot_general(p_ctx, vc, TN, preferred_element_type=F32))
        o_sel = jnp.where(lane_h < W, o[:W], o[W:])
        o_ref[pl.ds(qbase[r], W), :] = o_sel.astype(o_ref.dtype)

    def row_group(rg, carry):
        for stage in (score_stage, softmax_stage, value_stage):
            for r in range(NA_ROWS):
                stage(rg, r)
        return carry

    lax.fori_loop(0, rows // NA_ROWS, row_group, 0)


def na_attention(qkv, bias_tab, dims, L):
    BL, N, B, D = dims
    rows = N // GRID_W
    HP = NA_HEADS // 2
    tq = NA_ROWS * GRID_W
    assert BL % N == 0 and N % tq == 0 and rows >= NA_ROWS
    return pl.pallas_call(
        functools.partial(_na_kernel, rows=rows),
        grid=(HP, B),
        in_specs=[pl.BlockSpec((N, LANE), lambda hp, b: (BL // N + b, hp)),
                  pl.BlockSpec((N, LANE), lambda hp, b: (BL // N + b, HP + hp)),
                  pl.BlockSpec((N, LANE), lambda hp, b: (BL // N + b, 2 * HP + hp)),
                  pl.BlockSpec((L, LANE), lambda hp, b: (b, HP + hp)),
                  pl.BlockSpec((L, LANE), lambda hp, b: (b, 2 * HP + hp)),
                  pl.BlockSpec((1, NA_ROWS, tq, LANE), lambda hp, b: (hp, 0, 0, 0))],
        out_specs=pl.BlockSpec((N, LANE), lambda hp, b: (b, hp)),
        out_shape=jax.ShapeDtypeStruct((B * N, D), BF16),
        compiler_params=_cparams(("parallel", "parallel")),
        name="na_attention",
    )(qkv, qkv, qkv, qkv, qkv, bias_tab)


def _ctx_attn_kernel(q_ref, k_ref, v_ref, o_ref):
    q = q_ref[...]
    k = k_ref[...]
    v = v_ref[...]
    lane = lax.broadcasted_iota(jnp.int32, q.shape, 1)
    outs = []
    for hh in range(2):
        msk = (lane < GRID_W) if hh == 0 else (lane >= GRID_W)
        qh = jnp.where(msk, q, jnp.zeros_like(q))
        s = lax.dot_general(qh, k, NT, preferred_element_type=F32)
        m = jnp.max(s, axis=-1, keepdims=True)
        p = jnp.exp2(s - m)
        p = (p / jnp.sum(p, axis=-1, keepdims=True)).astype(BF16)
        outs.append(jnp.dot(p, v, preferred_element_type=F32))
    o_ref[...] = jnp.where(lane < GRID_W, outs[0], outs[1]).astype(o_ref.dtype)


def ctx_attention(qkv, dims, L):
    BL, N, B, D = dims
    HP = NA_HEADS // 2
    return pl.pallas_call(
        _ctx_attn_kernel,
        grid=(B, HP),
        in_specs=[pl.BlockSpec((L, LANE), lambda b, hp: (b, hp)),
                  pl.BlockSpec((L, LANE), lambda b, hp: (b, HP + hp)),
                  pl.BlockSpec((L, LANE), lambda b, hp: (b, 2 * HP + hp))],
        out_specs=pl.BlockSpec((L, LANE), lambda b, hp: (b, hp)),
        out_shape=jax.ShapeDtypeStruct((BL, D), BF16),
        compiler_params=_cparams(("parallel", "parallel")),
        name="ctx_attention",
    )(qkv, qkv, qkv)


def _gdn_feat_kernel(cur_ref, prev_ref, next_ref, cw_ref, o_ref, *, tm, BL, L, N):
    i = pl.program_id(0)
    j = pl.program_id(1)
    r0 = i * tm
    in_ctx = r0 < BL
    seg = jnp.where(in_ctx, L, N)
    off = jnp.where(in_ctx, r0, r0 - BL)
    keep_prev = jnp.where((off % seg) == 0, 0.0, 1.0)
    keep_next = jnp.where(((off + tm) % seg) == 0, 0.0, 1.0)
    row = lax.broadcasted_iota(jnp.int32, (tm, LANE), 0)
    qk_scale = jnp.where(j == 0, GDN_DK ** -0.5, 1.0)
    for c in range(cur_ref.shape[1] // LANE):
        sl = slice(c * LANE, (c + 1) * LANE)
        x = cur_ref[:, sl].astype(F32)
        pv = prev_ref[:, sl].astype(F32) * keep_prev
        nx = next_ref[:, sl].astype(F32) * keep_next
        w = cw_ref[:, sl]
        xm1 = jnp.where(row == 0, pv[15:16], pltpu.roll(x, 1, 0))
        xm2 = jnp.where(row == 0, pv[14:15], jnp.where(row == 1, pv[15:16], pltpu.roll(x, 2, 0)))
        xp1 = jnp.where(row == tm - 1, nx[0:1], pltpu.roll(x, tm - 1, 0))
        y = _silu(w[0:1] * xm2 + w[1:2] * xm1 + w[2:3] * x + w[3:4] * xp1)
        nrm = y * lax.rsqrt(jnp.sum(y * y, axis=-1, keepdims=True) + NORM_EPS) * qk_scale
        o_ref[:, sl] = jnp.where(j == 2, y, nrm).astype(o_ref.dtype)


def gdn_features(proj, conv_w, dims, L, tm=256):
    BL, N, B, D = dims
    M = proj.shape[0]
    C = GDN_HEADS * GDN_DK
    hb = tm // 16
    nhb = M // 16
    return pl.pallas_call(
        functools.partial(_gdn_feat_kernel, tm=tm, BL=BL, L=L, N=N),
        grid=(M // tm, 3),
        in_specs=[pl.BlockSpec((tm, C), lambda i, j: (i, j)),
                  pl.BlockSpec((16, C), lambda i, j: (jnp.maximum(i * hb - 1, 0), j)),
                  pl.BlockSpec((16, C), lambda i, j: (jnp.minimum((i + 1) * hb, nhb - 1), j)),
                  pl.BlockSpec((4, C), lambda i, j: (0, j))],
        out_specs=pl.BlockSpec((tm, C), lambda i, j: (i, j)),
        out_shape=jax.ShapeDtypeStruct((M, 3 * C), BF16),
        compiler_params=_cparams(("parallel", "arbitrary")),
        name="gdn_features",
    )(proj, proj, proj, conv_w)


def _gdn_intra_kernel(f_ref, ab_ref, pa_ref, pdt_ref, lm_ref, sm_ref,
                      w_ref, qd_ref, kt_ref, u_ref, aqk_ref, egl_ref, *, tm, cpg):
    d = pl.program_id(1)
    C = GDN_CHUNK
    H = GDN_HEADS
    DK = GDN_DK
    Lm = lm_ref[0]
    incl = Lm > 0.5
    strict = sm_ref[0] > 0.5
    eye = (lax.broadcasted_iota(jnp.int32, (C, C), 0) == lax.broadcasted_iota(jnp.int32, (C, C), 1)).astype(F32)
    lane = lax.broadcasted_iota(jnp.int32, (C, LANE), 1)
    pa = pa_ref[0]
    pdt = pdt_ref[0]

    def mm(a, b):
        return jnp.dot(a.astype(BF16), b.astype(BF16), preferred_element_type=F32)

    def nt(a, b):
        return lax.dot_general(a.astype(BF16), b.astype(BF16), NT, preferred_element_type=F32)

    def chunk_group(cg, carry):
        pairs = [(cc, h) for cc in range(cpg) for h in range(H)]
        rows = [pl.multiple_of((cg * cpg + cc) * C, C) for cc in range(cpg)]
        gc_all, gc_t, beta_all, g_last = [], [], [], []
        for cc in range(cpg):
            ab = ab_ref[pl.ds(rows[cc], C), :]
            sp = jnp.maximum(ab + pdt, 0.0) + jnp.log1p(jnp.exp(-jnp.abs(ab + pdt)))
            gval = jnp.where(lane < H, pa * sp, 0.0)
            ga = jnp.dot(Lm, gval, precision=HI, preferred_element_type=F32)
            gc_all.append(ga)
            gc_t.append(ga.T)
            beta_all.append(jax.nn.sigmoid(ab))
            g_last.append(jnp.where(d == 0, ga[C - 1:C], ga[0:1]))
        ld = lambda cc, col: f_ref[pl.ds(rows[cc], C), col * DK:(col + 1) * DK]
        q16 = [ld(cc, h) for cc, h in pairs]
        k16 = [ld(cc, H + h) for cc, h in pairs]
        k = [t.astype(F32) for t in k16]
        gc = [jnp.broadcast_to(gc_all[cc][:, h:h + 1], (C, DK)) for cc, h in pairs]
        beta = [jnp.broadcast_to(beta_all[cc][:, H + h:H + h + 1], (C, DK)) for cc, h in pairs]
        decay = []
        for i, (cc, h) in enumerate(pairs):
            diff = gc[i][:, :C] - gc_t[cc][h:h + 1, :]
            decay.append(jnp.where(incl, jnp.exp(jnp.where(incl, diff, 0.0)), 0.0))
        kb = [k[i] * beta[i] for i in range(len(pairs))]
        nm = [jnp.where(strict, nt(kb[i], k16[i]) * decay[i], 0.0) for i in range(len(pairs))]
        xinv = [eye - t for t in nm]
        pw = [mm(t, t) for t in nm]
        for it in range(5):
            if it < 4:
                z = [mm(jnp.concatenate([xinv[i], pw[i]], axis=0), pw[i]) for i in range(len(pairs))]
                xinv = [xinv[i] + z[i][:C] for i in range(len(pairs))]
                pw = [t[C:] for t in z]
            else:
                xinv = [xinv[i] + mm(xinv[i], pw[i]) for i in range(len(pairs))]
        eg = [jnp.exp(t) for t in gc]
        uw = [mm(xinv[i], jnp.concatenate([ld(cc, 2 * H + h).astype(F32) * beta[i], kb[i] * eg[i]], axis=1))
              for i, (cc, h) in enumerate(pairs)]
        u = [t[:, :DK] for t in uw]
        w = [t[:, DK:] for t in uw]
        aqk = [nt(q16[i], k16[i]) * decay[i] for i in range(len(pairs))]
        for i, (cc, h) in enumerate(pairs):
            sl = slice(h * DK, (h + 1) * DK)
            r = rows[cc]
            gl = jnp.broadcast_to(g_last[cc][:, h:h + 1], (1, DK))
            w_ref[0, pl.ds(r, C), sl] = w[i].astype(w_ref.dtype)
            u_ref[0, pl.ds(r, C), sl] = u[i]
            qd_ref[0, pl.ds(r, C), sl] = (q16[i].astype(F32) * eg[i]).astype(qd_ref.dtype)
            kt_ref[0, pl.ds(r, C), sl] = (k[i] * jnp.exp(gl - gc[i])).astype(kt_ref.dtype)
            aqk_ref[0, pl.ds(r, C), h * C:(h + 1) * C] = aqk[i].astype(aqk_ref.dtype)
            egl_ref[0, cg * cpg + cc, h:h + 1, :] = jnp.exp(gl)
        return carry

    lax.fori_loop(0, tm // (C * cpg), chunk_group, 0)


def gdn_intra(feats, ab, a_log, dt_bias, tm=512):
    M = feats.shape[0]
    H, DK, C = GDN_HEADS, GDN_DK, GDN_CHUNK
    HD = H * DK
    pa = jnp.zeros((2, 1, LANE), F32).at[:, 0, :H].set(-jnp.exp(a_log.astype(F32)))
    pdt = jnp.zeros((2, 1, LANE), F32).at[:, 0, :H].set(dt_bias.astype(F32))
    idx = np.arange(C)
    lower = idx[:, None] >= idx[None, :]
    lm = jnp.asarray(np.stack([lower, lower.T]).astype(np.float32))
    sm = jnp.asarray(np.stack([idx[:, None] > idx[None, :], idx[:, None] < idx[None, :]]).astype(np.float32))
    big = lambda dt: jax.ShapeDtypeStruct((2, M, HD), dt)
    dspec = pl.BlockSpec((1, tm, HD), lambda i, d: (d, i, 0))
    return pl.pallas_call(
        functools.partial(_gdn_intra_kernel, tm=tm, cpg=4),
        grid=(M // tm, 2),
        in_specs=[pl.BlockSpec((tm, 3 * HD), lambda i, d: (i, 0)),
                  pl.BlockSpec((tm, LANE), lambda i, d: (i, d)),
                  pl.BlockSpec((1, 1, LANE), lambda i, d: (d, 0, 0)),
                  pl.BlockSpec((1, 1, LANE), lambda i, d: (d, 0, 0)),
                  pl.BlockSpec((1, C, C), lambda i, d: (d, 0, 0)),
                  pl.BlockSpec((1, C, C), lambda i, d: (d, 0, 0))],
        out_specs=[dspec, dspec, dspec, dspec,
                   pl.BlockSpec((1, tm, H * C), lambda i, d: (d, i, 0)),
                   pl.BlockSpec((1, tm // C, H, LANE), lambda i, d: (d, i, 0, 0))],
        out_shape=[big(BF16), big(BF16), big(BF16), big(F32),
                   jax.ShapeDtypeStruct((2, M, H * C), BF16),
                   jax.ShapeDtypeStruct((2, M // C, H, LANE), F32)],
        compiler_params=_cparams(("parallel", "arbitrary")),
        name="gdn_intra",
    )(feats, ab, pa, pdt, lm, sm)


def _gdn_scan_kernel(*refs, nch):
    ins = (refs[0:6], refs[6:12])
    outs = refs[12:14]
    s_ref = refs[14]
    C = GDN_CHUNK
    DK = GDN_DK
    H = GDN_HEADS

    @pl.when(pl.program_id(1) == 0)
    def _():
        s_ref[...] = jnp.zeros_like(s_ref)

    def chunk(j, carry):
        cs = (j, nch - 1 - j)
        rs = tuple(pl.multiple_of(c * C, C) for c in cs)
        pairs = [(d, h) for d in range(2) for h in range(H)]
        sl = lambda h: slice(h * DK, (h + 1) * DK)
        ld = lambda k, d, h: ins[d][k][0, pl.ds(rs[d], C), sl(h)]
        S = [s_ref[d, h] for d, h in pairs]
        Sb = [t.astype(BF16) for t in S]
        ws = [jnp.dot(ld(0, d, h), Sb[i], preferred_element_type=F32) for i, (d, h) in enumerate(pairs)]
        qs = [jnp.dot(ld(1, d, h), Sb[i], preferred_element_type=F32) for i, (d, h) in enumerate(pairs)]
        vb = [(ld(3, d, h) - ws[i]).astype(BF16) for i, (d, h) in enumerate(pairs)]
        av = [jnp.dot(ins[d][4][0, pl.ds(rs[d], C), h * C:(h + 1) * C], vb[i], preferred_element_type=F32)
              for i, (d, h) in enumerate(pairs)]
        kv = [lax.dot_general(ld(2, d, h), vb[i], TN, preferred_element_type=F32)
              for i, (d, h) in enumerate(pairs)]
        for i, (d, h) in enumerate(pairs):
            outs[d][pl.ds(rs[d], C), sl(h)] = qs[i] + av[i]
            s_ref[d, h] = S[i] * ins[d][5][0, cs[d], h:h + 1, :] + kv[i]
        return carry

    lax.fori_loop(0, nch, chunk, 0)


def gdn_scan(w, qd, kt, u, aqk, egl, dims, L):
    BL, N, B, D = dims
    M = w.shape[1]
    H, DK, C = GDN_HEADS, GDN_DK, GDN_CHUNK
    HD = H * DK
    blk = L
    nlat = N // blk
    assert N % blk == 0 and blk % C == 0

    def rb(d):
        def index(b, s):
            lat = BL // blk + b * nlat + (s - 1 if d == 0 else nlat - s)
            return jnp.where(s == 0, b, lat)
        return index

    in_specs, args = [], []
    for d in range(2):
        r = rb(d)
        big = pl.BlockSpec((1, blk, HD), lambda b, s, r=r, d=d: (d, r(b, s), 0))
        in_specs += [big, big, big, big,
                     pl.BlockSpec((1, blk, H * C), lambda b, s, r=r, d=d: (d, r(b, s), 0)),
                     pl.BlockSpec((1, blk // C, H, LANE), lambda b, s, r=r, d=d: (d, r(b, s), 0, 0))]
        args += [w, qd, kt, u, aqk, egl]
    return pl.pallas_call(
        functools.partial(_gdn_scan_kernel, nch=blk // C),
        grid=(B, 1 + nlat),
        in_specs=in_specs,
        out_specs=[pl.BlockSpec((blk, HD), lambda b, s, r=rb(d): (r(b, s), 0)) for d in range(2)],
        out_shape=[jax.ShapeDtypeStruct((M, HD), F32)] * 2,
        scratch_shapes=[pltpu.VMEM((2, H, DK, DK), F32)],
        compiler_params=_cparams(("parallel", "arbitrary")),
        name="gdn_scan",
    )(*args)


def _gdn_out_kernel(of_ref, ob_ref, z_ref, ng_ref, w_ref, x_ref, gate_ref, o_ref, a_ref):
    DK = GDN_DK
    for h in range(GDN_HEADS):
        sl = slice(h * DK, (h + 1) * DK)
        o = of_ref[:, sl] + ob_ref[:, sl]
        y = o * lax.rsqrt(jnp.mean(o * o, axis=-1, keepdims=True) + NORM_EPS) * ng_ref[...]
        z = z_ref[:, sl].astype(F32)
        a_ref[:, sl] = (y * _silu(z)).astype(BF16)
    y = jnp.dot(a_ref[...], w_ref[...], preferred_element_type=F32)
    o_ref[...] = x_ref[...] + gate_ref[0] * y


def gdn_output(o_f, o_b, proj, norm_g, w_o, xs, gate, dims, row_start=0, tm=512):
    M, D = xs.shape
    HD = GDN_HEADS * GDN_DK
    off = row_start // tm
    return pl.pallas_call(
        _gdn_out_kernel,
        grid=(M // tm - off,),
        in_specs=[pl.BlockSpec((tm, HD), lambda i: (i + off, 0)),
                  pl.BlockSpec((tm, HD), lambda i: (i + off, 0)),
                  pl.BlockSpec((tm, HD), lambda i: (i + off, 3)),
                  pl.BlockSpec((1, GDN_DK), lambda i: (0, 0)),
                  pl.BlockSpec((HD, D), lambda i: (0, 0)),
                  pl.BlockSpec((tm, D), lambda i: (i + off, 0)),
                  _mod_spec(tm, off, dims)],
        out_specs=pl.BlockSpec((tm, D), lambda i: (i + off, 0)),
        out_shape=jax.ShapeDtypeStruct((M, D), F32),
        scratch_shapes=[pltpu.VMEM((tm, HD), BF16)],
        input_output_aliases={5: 0},
        compiler_params=_cparams(("parallel",)),
        name="gdn_output",
    )(o_f, o_b, proj, norm_g.reshape(1, GDN_DK), w_o, xs, gate)


def _pool_tables(tm, L):
    amats, invs = [], []
    t = np.arange(tm)
    for seg in (L, GRID_W):
        a_v, i_v = [], []
        tl = t % seg
        for win in POOL_WINDOWS:
            lo = np.clip(tl - win // 2, 0, seg)
            hi = np.clip(tl + win // 2, 0, seg)
            same = (t[:, None] // seg) == (t[None, :] // seg)
            a = same & (tl[None, :] >= lo[:, None]) & (tl[None, :] < hi[:, None])
            a_v.append(a.astype(np.float32))
            i_v.append(np.broadcast_to((1.0 / (hi - lo))[:, None], (tm, LANE)).astype(np.float32))
        amats.append(np.stack(a_v))
        invs.append(np.stack(i_v))
    return jnp.asarray(np.stack(amats), BF16), jnp.asarray(np.stack(invs), F32)


def _pool_kernel(x_ref, g_ref, sh_ref, sc_ref, gate_ref, a_ref, ic_ref, pw_ref, ls_ref, o_ref):
    x = x_ref[...]
    h = _prenorm(x, g_ref[...], sh_ref[0], sc_ref[0])
    G = pw_ref.shape[1]
    for gi in range(len(POOL_WINDOWS)):
        sl = slice(gi * G, (gi + 1) * G)
        hg = h[:, sl]
        hi = hg.astype(BF16)
        lo = (hg - hi.astype(F32)).astype(BF16)
        am = a_ref[0, gi]
        wsum = jnp.dot(am, hi, preferred_element_type=F32) + jnp.dot(am, lo, preferred_element_type=F32)
        ic = ic_ref[0, gi]
        mean = wsum * jnp.concatenate([ic] * (G // LANE), axis=-1)
        pooled = (mean - hg).astype(BF16)
        y = jnp.dot(pooled, pw_ref[gi], preferred_element_type=F32) * ls_ref[:, sl]
        o_ref[:, sl] = x[:, sl] + gate_ref[0][:, sl] * y


def pool_sublayer(xs, g, shift, scale, gate, pool_w, ls, dims, L, row_start=0, tm=256):
    BL, N, B, D = dims
    M = xs.shape[0]
    assert tm == L and tm % GRID_W == 0
    amat, inv = _pool_tables(tm, L)
    off = row_start // tm
    nct = BL // tm
    G = D // len(POOL_WINDOWS)
    variant = lambda i: jnp.where(i + off < nct, 0, 1)
    return pl.pallas_call(
        _pool_kernel,
        grid=(M // tm - off,),
        in_specs=[pl.BlockSpec((tm, D), lambda i: (i + off, 0)),
                  pl.BlockSpec((1, D), lambda i: (0, 0)),
                  _mod_spec(tm, off, dims), _mod_spec(tm, off, dims), _mod_spec(tm, off, dims),
                  pl.BlockSpec((1, 4, tm, tm), lambda i: (variant(i), 0, 0, 0)),
                  pl.BlockSpec((1, 4, tm, LANE), lambda i: (variant(i), 0, 0, 0)),
                  pl.BlockSpec((4, G, G), lambda i: (0, 0, 0)),
                  pl.BlockSpec((1, D), lambda i: (0, 0))],
        out_specs=pl.BlockSpec((tm, D), lambda i: (i + off, 0)),
        out_shape=jax.ShapeDtypeStruct((M, D), F32),
        input_output_aliases={0: 0},
        compiler_params=_cparams(("parallel",)),
        name="pool_sublayer",
    )(xs, g.reshape(1, D), shift, scale, gate, amat, inv, pool_w.astype(BF16), ls.reshape(1, D))


def _router_kernel(x_ref, g_ref, sh_ref, sc_ref, wr_ref, h_ref, ti_ref, tp_ref):
    h = _prenorm(x_ref[...], g_ref[...], sh_ref[0], sc_ref[0])
    h_ref[...] = h
    h_hi = h.astype(BF16)
    h_lo = (h - h_hi.astype(F32)).astype(BF16)
    wr = wr_ref[...]
    both = jnp.dot(h_hi, wr, preferred_element_type=F32)
    logits = both[:, :LANE] + both[:, LANE:] + jnp.dot(h_lo, wr[:, :LANE], preferred_element_type=F32)
    lane = lax.broadcasted_iota(jnp.int32, logits.shape, 1)
    valid = lane < N_EXPERTS
    lg = jnp.where(valid, logits, NEG_INF)
    e = jnp.where(valid, jnp.exp(lg - jnp.max(lg, axis=-1, keepdims=True)), 0.0)
    probs = jnp.where(valid, e / jnp.sum(e, axis=-1, keepdims=True), -1.0)
    p1 = jnp.max(probs, axis=-1, keepdims=True)
    i1 = jnp.min(jnp.where(probs == p1, lane, LANE), axis=-1, keepdims=True)
    rest = jnp.where(lane == i1, -1.0, probs)
    p2 = jnp.max(rest, axis=-1, keepdims=True)
    i2 = jnp.min(jnp.where(rest == p2, lane, LANE), axis=-1, keepdims=True)
    tot = p1 + p2
    tp_ref[...] = jnp.where(lane == 0, p1 / tot, jnp.where(lane == 1, p2 / tot, 0.0))
    ti_ref[...] = jnp.where(lane == 0, i1, jnp.where(lane == 1, i2, 0))


def moe_router(xs, g, shift, scale, w_router, dims, row_start=0, tm=512):
    M, D = xs.shape
    off = row_start // tm
    wr = jnp.zeros((D, LANE), F32).at[:, :N_EXPERTS].set(w_router)
    wr_hi = wr.astype(BF16)
    wr = jnp.concatenate([wr_hi, (wr - wr_hi.astype(F32)).astype(BF16)], axis=1)
    Mo = M - row_start
    ospec = lambda w: pl.BlockSpec((tm, w), lambda i: (i, 0))
    return pl.pallas_call(
        _router_kernel,
        grid=(Mo // tm,),
        in_specs=[pl.BlockSpec((tm, D), lambda i: (i + off, 0)),
                  pl.BlockSpec((1, D), lambda i: (0, 0)),
                  _mod_spec(tm, off, dims), _mod_spec(tm, off, dims),
                  pl.BlockSpec((D, 2 * LANE), lambda i: (0, 0))],
        out_specs=[ospec(D), ospec(LANE), ospec(LANE)],
        out_shape=[jax.ShapeDtypeStruct((Mo, D), F32),
                   jax.ShapeDtypeStruct((Mo, LANE), jnp.int32),
                   jax.ShapeDtypeStruct((Mo, LANE), F32)],
        compiler_params=_cparams(("parallel",)),
        name="moe_router",
    )(xs, g.reshape(1, D), shift, scale, wr)


def _route_positions(top_e):
    n_tok = top_e.shape[0]
    e_flat = top_e.reshape(-1)
    onehot = (e_flat[:, None] == jnp.arange(N_EXPERTS, dtype=jnp.int32)[None, :]).astype(jnp.int32)
    csum = jnp.cumsum(onehot, axis=0)
    rank = jnp.sum(csum * onehot, axis=-1) - 1
    counts = csum[-1]
    padded = ((counts + MOE_BLOCK - 1) // MOE_BLOCK) * MOE_BLOCK
    ends = jnp.cumsum(padded)
    starts = ends - padded
    pos = (jnp.sum(onehot * starts[None, :], axis=-1) + rank).reshape(n_tok, TOP_K)
    n_blocks = -(-(n_tok * TOP_K) // MOE_BLOCK) + N_EXPERTS
    block_e = jnp.minimum(jnp.searchsorted(ends, jnp.arange(n_blocks) * MOE_BLOCK, side='right'),
                          N_EXPERTS - 1).astype(jnp.int32)
    pos = pos.astype(jnp.int32)
    cap = n_blocks * MOE_BLOCK
    n_pairs = n_tok * TOP_K
    pair_row = (jnp.arange(TOP_K, dtype=jnp.int32)[None, :] * n_tok
                + jnp.arange(n_tok, dtype=jnp.int32)[:, None]).reshape(-1)
    dst = (n_pairs + jnp.arange(cap, dtype=jnp.int32)).at[pos.reshape(-1)].set(pair_row, unique_indices=True)
    src = jnp.where(dst < n_pairs, dst % n_tok, 0)
    lead = n_pairs + cap + jnp.arange(MOE_BLOCK, dtype=jnp.int32)
    dst = jnp.concatenate([lead, dst])
    n_used = (ends[-1] // MOE_BLOCK).astype(jnp.int32).reshape(1)
    return src, dst, block_e, n_used, n_blocks


def _expert_kernel(be_ref, nu_ref, tokc_ref, tokn_ref, dstp_ref, dstc_ref, h_hbm, w1_ref, w3_ref, w2_ref,
                   y_hbm, xg_ref, ob_ref, a_ref, sem_g, sem_o, *, tf, n_blocks):
    del be_ref
    i = pl.program_id(0)
    n_used = nu_ref[0]
    slot = i % 2
    other = 1 - slot
    F = w1_ref.shape[2]
    nchunk = F // tf
    per = -(-MOE_BLOCK // nchunk)

    def gather_copy(tok_ref, s, t):
        return pltpu.make_async_copy(h_hbm.at[pl.ds(tok_ref[0, 0, t], 1), :],
                                     xg_ref.at[s, pl.ds(t, 1), :], sem_g.at[s])

    def scatter_copy(dst_ref, s, t):
        return pltpu.make_async_copy(ob_ref.at[s, pl.ds(t, 1), :],
                                     y_hbm.at[pl.ds(dst_ref[0, 0, t], 1), :], sem_o.at[s])

    def start_loop(make, ref, s):
        def body(t, carry):
            make(ref, s, t).start()
            return carry
        lax.fori_loop(0, MOE_BLOCK, body, 0, unroll=8)

    def wait_gather(s):
        pltpu.make_async_copy(h_hbm.at[pl.ds(0, MOE_BLOCK), :], xg_ref.at[s], sem_g.at[s]).wait()

    def wait_scatter(s):
        pltpu.make_async_copy(ob_ref.at[s], y_hbm.at[pl.ds(0, MOE_BLOCK), :], sem_o.at[s]).wait()

    @pl.when(i == 0)
    def _():
        ob_ref[...] = jnp.zeros_like(ob_ref)
        start_loop(gather_copy, tokc_ref, 0)

    wait_gather(slot)

    @pl.when(i >= 1)
    def _():
        wait_scatter(slot)

    @pl.when(i < n_used)
    def _():
        hb = xg_ref[slot].astype(BF16)
        for c in range(nchunk):
            cs = slice(c * tf, (c + 1) * tf)
            gg = jnp.dot(hb, w1_ref[0, :, cs], preferred_element_type=F32)
            uu = jnp.dot(hb, w3_ref[0, :, cs], preferred_element_type=F32)
            a_ref[:, cs] = (_silu(gg) * uu).astype(BF16)
            for t in range(c * per, min((c + 1) * per, MOE_BLOCK)):
                gather_copy(tokn_ref, other, t).start()
                scatter_copy(dstp_ref, other, t).start()
        ob_ref[slot] = jnp.dot(a_ref[...], w2_ref[0], preferred_element_type=F32)

    @pl.when(i >= n_used)
    def _():
        start_loop(gather_copy, tokn_ref, other)
        start_loop(scatter_copy, dstp_ref, other)
        ob_ref[slot] = jnp.zeros(ob_ref.shape[1:], F32)

    @pl.when(i == n_blocks - 1)
    def _():
        start_loop(scatter_copy, dstc_ref, slot)
        wait_scatter(other)
        wait_scatter(slot)
        wait_gather(other)


def moe_experts(h, src, dst, block_e, n_used, w1, w3, w2, n_blocks, tf=256):
    n_tok, D = h.shape
    F = w1.shape[2]
    last = n_blocks - 1
    smem = lambda index: pl.BlockSpec((1, 1, MOE_BLOCK), index, memory_space=pltpu.SMEM)
    resident = dict(pipeline_mode=pl.Buffered(1))
    grid_spec = pltpu.PrefetchScalarGridSpec(
        num_scalar_prefetch=2,
        grid=(n_blocks,),
        in_specs=[smem(lambda i, be, nu: (i, 0, 0)),
                  smem(lambda i, be, nu: (jnp.minimum(i + 1, last), 0, 0)),
                  smem(lambda i, be, nu: (i, 0, 0)),
                  smem(lambda i, be, nu: (i + 1, 0, 0)),
                  pl.BlockSpec(memory_space=pl.ANY),
                  pl.BlockSpec((1, D, F), lambda i, be, nu: (be[i], 0, 0), **resident),
                  pl.BlockSpec((1, D, F), lambda i, be, nu: (be[i], 0, 0), **resident),
                  pl.BlockSpec((1, F, D), lambda i, be, nu: (be[i], 0, 0), **resident)],
        out_specs=pl.BlockSpec(memory_space=pl.ANY),
        scratch_shapes=[pltpu.VMEM((2, MOE_BLOCK, D), F32), pltpu.VMEM((2, MOE_BLOCK, D), F32),
                        pltpu.VMEM((MOE_BLOCK, F), BF16),
                        pltpu.SemaphoreType.DMA((2,)), pltpu.SemaphoreType.DMA((2,))],
    )
    src3 = src.reshape(n_blocks, 1, MOE_BLOCK)
    dst3 = dst.reshape(n_blocks + 1, 1, MOE_BLOCK)
    n_rows = n_tok * TOP_K + (n_blocks + 1) * MOE_BLOCK
    return pl.pallas_call(
        functools.partial(_expert_kernel, tf=tf, n_blocks=n_blocks),
        grid_spec=grid_spec,
        out_shape=jax.ShapeDtypeStruct((n_rows, D), F32),
        compiler_params=_cparams(("arbitrary",)),
        name="moe_experts",
    )(block_e, n_used, src3, src3, dst3, dst3, h, w1, w3, w2)


def _combine_kernel(y0_ref, y1_ref, tp_ref, x_ref, gate_ref, o_ref):
    tp = tp_ref[...]
    y = tp[:, 0:1] * y0_ref[...] + tp[:, 1:2] * y1_ref[...]
    o_ref[...] = x_ref[...] + gate_ref[0] * y


def _combine_norm_kernel(y0_ref, y1_ref, tp_ref, x_ref, gate_ref, fg_ref, o_ref):
    tp = tp_ref[...]
    y = tp[:, 0:1] * y0_ref[...] + tp[:, 1:2] * y1_ref[...]
    x = x_ref[...] + gate_ref[0] * y
    o_ref[...] = x * lax.rsqrt(jnp.mean(x * x, axis=-1, keepdims=True) + NORM_EPS) * fg_ref[...]


def moe_combine(y, top_p, xs, gate, dims, row_start=0, final_g=None, tm=512):
    M, D = xs.shape
    off = row_start // tm
    nt = (M - row_start) // tm
    in_specs = [pl.BlockSpec((tm, D), lambda i: (i, 0)),
                pl.BlockSpec((tm, D), lambda i: (nt + i, 0)),
                pl.BlockSpec((tm, LANE), lambda i: (i, 0)),
                pl.BlockSpec((tm, D), lambda i: (i + off, 0)),
                _mod_spec(tm, off, dims)]
    if final_g is None:
        return pl.pallas_call(
            _combine_kernel,
            grid=(nt,),
            in_specs=in_specs,
            out_specs=pl.BlockSpec((tm, D), lambda i: (i + off, 0)),
            out_shape=jax.ShapeDtypeStruct((M, D), F32),
            input_output_aliases={3: 0},
            compiler_params=_cparams(("parallel",)),
            name="moe_combine",
        )(y, y, top_p, xs, gate)
    return pl.pallas_call(
        _combine_norm_kernel,
        grid=(nt,),
        in_specs=in_specs + [pl.BlockSpec((1, D), lambda i: (0, 0))],
        out_specs=pl.BlockSpec((tm, D), lambda i: (i, 0)),
        out_shape=jax.ShapeDtypeStruct((M - row_start, D), F32),
        compiler_params=_cparams(("parallel",)),
        name="moe_combine_norm",
    )(y, y, top_p, xs, gate, final_g.reshape(1, D))


def moe_sublayer(xs, g, shift, scale, gate, w_router, w1, w3, w2, dims, row_start, final_g=None):
    h, top_i, top_p = moe_router(xs, g, shift, scale, w_router, dims, row_start)
    src, dst, block_e, n_used, n_blocks = _route_positions(top_i[:, :TOP_K])
    y = moe_experts(h, src, dst, block_e, n_used, w1, w3, w2, n_blocks)
    return moe_combine(y, top_p, xs, gate, dims, row_start, final_g)


def _final_norm_kernel(x_ref, g_ref, o_ref):
    x = x_ref[...]
    o_ref[...] = x * lax.rsqrt(jnp.mean(x * x, axis=-1, keepdims=True) + NORM_EPS) * g_ref[...]


def final_norm(xs, g, row_start, tm=512):
    M, D = xs.shape
    off = row_start // tm
    return pl.pallas_call(
        _final_norm_kernel,
        grid=((M - row_start) // tm,),
        in_specs=[pl.BlockSpec((tm, D), lambda i: (i + off, 0)),
                  pl.BlockSpec((1, D), lambda i: (0, 0))],
        out_specs=pl.BlockSpec((tm, D), lambda i: (i, 0)),
        out_shape=jax.ShapeDtypeStruct((M - row_start, D), F32),
        compiler_params=_cparams(("parallel",)),
        name="final_norm",
    )(xs, g.reshape(1, D))


def na_mixer(xs, g, m, w_qkv, w_o, rpb, dims, L, need_ctx):
    BL, N, B, D = dims
    col_scale = jnp.concatenate([jnp.full((D,), NA_HEAD_DIM ** -0.5 * LOG2E, F32), jnp.ones((2 * D,), F32)])
    qkv = prenorm_matmul(xs, g, m[0], m[1], w_qkv.astype(BF16), dims, col_scale=col_scale)
    bias_tab = _na_bias_table(rpb.astype(F32))
    w_o = w_o.astype(BF16)
    if need_ctx:
        xs = matmul_residual(ctx_attention(qkv, dims, L), w_o, xs, m[2], dims, row_start=0)
    return matmul_residual(na_attention(qkv, bias_tab, dims, L), w_o, xs, m[2], dims, row_start=BL)


def gdn_mixer(xs, g, m, w_in, conv_w, a_log, dt_bias, norm_g, w_o, dims, L, need_ctx):
    BL, N, B, D = dims
    H = GDN_HEADS
    HD = H * GDN_DK
    w_main = w_in[:, :4 * HD].astype(BF16)
    w_gate = w_in[:, 4 * HD:].astype(F32)
    w_ab = jnp.zeros((D, 2 * LANE), F32)
    for d in range(2):
        w_ab = w_ab.at[:, d * LANE:d * LANE + H].set(w_gate[:, d * H:(d + 1) * H])
        w_ab = w_ab.at[:, d * LANE + H:d * LANE + 2 * H].set(w_gate[:, (2 + d) * H:(3 + d) * H])
    proj, ab = prenorm_matmul(xs, g, m[0], m[1], w_main, dims, w_hi=w_ab)
    feats = gdn_features(proj, conv_w.astype(F32), dims, L)
    w, qd, kt, u, aqk, egl = gdn_intra(feats, ab, a_log, dt_bias)
    o_f, o_b = gdn_scan(w, qd, kt, u, aqk, egl, dims, L)
    return gdn_output(o_f, o_b, proj, norm_g.astype(F32), w_o.astype(BF16), xs, m[2], dims,
                      row_start=0 if need_ctx else BL)


def kernel(x, c, ctx, c_ctx, ada_w, ada_b, norm_g, final_g, na_w_qkv, na_w_o, na_rpb, gdn_w_in, gdn_conv, gdn_a_log, gdn_dt_bias, gdn_norm_g, gdn_w_o, pool_w, pool_scale, ffn_w1, ffn_w3, ffn_w2, moe_router, moe_w1, moe_w3, moe_w2):
    B, N, D = x.shape
    L = ctx.shape[1]
    depth = ada_w.shape[0]
    BL = B * L
    dims = (BL, N, B, D)
    xs = jnp.concatenate([ctx.reshape(BL, D), x.reshape(B * N, D)], axis=0)

    R = -(-(B + 1) // 8) * 8
    cvec = jnp.zeros((R, D), F32).at[:B].set(c).at[B].set(c_ctx)
    mods = ada_table(cvec, ada_w, ada_b)[:, :B + 1].reshape(depth, B + 1, 6, 1, D)

    for i in range(depth):
        last = i == depth - 1
        m = [mods[i, :, k] for k in range(6)]
        j = i // 3
        kind = i % 3
        if kind == 0:
            xs = na_mixer(xs, norm_g[i, 0], m, na_w_qkv[j], na_w_o[j], na_rpb[j], dims, L, not last)
        elif kind == 1:
            xs = gdn_mixer(xs, norm_g[i, 0], m, gdn_w_in[j], gdn_conv[j], gdn_a_log[j], gdn_dt_bias[j],
                           gdn_norm_g[j], gdn_w_o[j], dims, L, not last)
        else:
            xs = pool_sublayer(xs, norm_g[i, 0], m[0], m[1], m[2], pool_w[j], pool_scale[j], dims, L,
                               row_start=0 if not last else BL)
        row_start = BL if last else 0
        f = i // 2
        if i % 2 == 0:
            xs = ffn_sublayer(xs, norm_g[i, 1], m[3], m[4], m[5], ffn_w1[f].astype(BF16),
                              ffn_w3[f].astype(BF16), ffn_w2[f].astype(BF16), dims, row_start=row_start)
        else:
            xs = moe_sublayer(xs, norm_g[i, 1], m[3], m[4], m[5], moe_router[f], moe_w1[f].astype(BF16),
                              moe_w3[f].astype(BF16), moe_w2[f].astype(BF16), dims, row_start,
                              final_g=final_g if last else None)
            if last:
                return xs.reshape(B, N, D)
    return final_norm(xs, final_g, BL).reshape(B, N, D)
```

```python
import functools

import numpy as np
import jax
import jax.numpy as jnp
from jax import lax
from jax.experimental import pallas as pl
from jax.experimental.pallas import tpu as pltpu

F32 = jnp.float32
BF16 = jnp.bfloat16
HI = lax.Precision.HIGHEST

NORM_EPS = 1e-6
NEG_INF = -1e30
LOG2E = 1.4426950408889634
GRID_W = 64
NA_HEADS = 16
NA_HEAD_DIM = 64
NA_ROWS = 8
NA_COLS = 16
GDN_HEADS = 8
GDN_DK = 128
GDN_CHUNK = 64
POOL_WINDOWS = (2, 4, 8, 16)
N_EXPERTS = 8
TOP_K = 2
MOE_BLOCK = 512
LANE = 128
V7X_VMEM_LIMIT = 56 * 1024 * 1024

NT = (((1,), (1,)), ((), ()))
TN = (((0,), (0,)), ((), ()))


def _cparams(sem, vmem=V7X_VMEM_LIMIT):
    return pltpu.CompilerParams(dimension_semantics=sem, vmem_limit_bytes=vmem)


def _silu(v):
    return v * jax.nn.sigmoid(v)


def _prenorm(x, g, shift, scale):
    ms = jnp.mean(x * x, axis=-1, keepdims=True)
    y = x * lax.rsqrt(ms + NORM_EPS) * g
    return y * (1.0 + scale) + shift


def _mod_index(i, tm, BL, N, B):
    nct = BL // tm
    return jnp.where(i < nct, B, (i - nct) // (N // tm))


def _mod_spec(tm, off, dims):
    BL, N, B, D = dims
    return pl.BlockSpec((1, 1, D), lambda i, *_: (_mod_index(i + off, tm, BL, N, B), 0, 0))


def _ada_kernel(c_ref, w_ref, b_ref, o_ref):
    cv = c_ref[...]
    o_ref[0] = jnp.dot(_silu(cv), w_ref[0], precision=HI, preferred_element_type=F32) + b_ref[0]


def ada_table(cvec, ada_w, ada_b):
    depth, D, D6 = ada_w.shape
    R = cvec.shape[0]
    tn = 1536
    return pl.pallas_call(
        _ada_kernel,
        grid=(depth, D6 // tn),
        in_specs=[pl.BlockSpec((R, D), lambda l, j: (0, 0)),
                  pl.BlockSpec((1, D, tn), lambda l, j: (l, 0, j)),
                  pl.BlockSpec((1, 1, tn), lambda l, j: (l, 0, j))],
        out_specs=pl.BlockSpec((1, R, tn), lambda l, j: (l, 0, j)),
        out_shape=jax.ShapeDtypeStruct((depth, R, D6), F32),
        compiler_params=_cparams(("arbitrary", "arbitrary")),
        name="ada_table",
    )(cvec, ada_w, ada_b.reshape(depth, 1, D6))


def _prenorm_matmul_kernel(x_ref, g_ref, sh_ref, sc_ref, w_ref, cs_ref, o_ref):
    h = _prenorm(x_ref[...], g_ref[...], sh_ref[0], sc_ref[0]).astype(BF16)
    y = jnp.dot(h, w_ref[...], preferred_element_type=F32)
    o_ref[...] = (y * cs_ref[...]).astype(o_ref.dtype)


def _prenorm_matmul2_kernel(x_ref, g_ref, sh_ref, sc_ref, w_ref, w2_ref, o_ref, o2_ref):
    h = _prenorm(x_ref[...], g_ref[...], sh_ref[0], sc_ref[0])
    o_ref[...] = jnp.dot(h.astype(BF16), w_ref[...], preferred_element_type=F32).astype(o_ref.dtype)
    o2_ref[...] = jnp.dot(h, w2_ref[...], precision=HI, preferred_element_type=F32)


def prenorm_matmul(xs, g, shift, scale, w, dims, col_scale=None, w_hi=None, tm=512):
    M, D = xs.shape
    Nout = w.shape[1]
    specs = [pl.BlockSpec((tm, D), lambda i: (i, 0)),
             pl.BlockSpec((1, D), lambda i: (0, 0)),
             _mod_spec(tm, 0, dims), _mod_spec(tm, 0, dims),
             pl.BlockSpec((D, Nout), lambda i: (0, 0))]
    ospec = pl.BlockSpec((tm, Nout), lambda i: (i, 0))
    oshape = jax.ShapeDtypeStruct((M, Nout), BF16)
    if w_hi is None:
        cs = jnp.ones((1, Nout), F32) if col_scale is None else col_scale.reshape(1, Nout)
        return pl.pallas_call(
            _prenorm_matmul_kernel,
            grid=(M // tm,),
            in_specs=specs + [pl.BlockSpec((1, Nout), lambda i: (0, 0))],
            out_specs=ospec,
            out_shape=oshape,
            compiler_params=_cparams(("parallel",)),
            name="prenorm_matmul",
        )(xs, g.reshape(1, D), shift, scale, w, cs)
    N2 = w_hi.shape[1]
    return pl.pallas_call(
        _prenorm_matmul2_kernel,
        grid=(M // tm,),
        in_specs=specs + [pl.BlockSpec((D, N2), lambda i: (0, 0))],
        out_specs=[ospec, pl.BlockSpec((tm, N2), lambda i: (i, 0))],
        out_shape=[oshape, jax.ShapeDtypeStruct((M, N2), F32)],
        compiler_params=_cparams(("parallel",)),
        name="prenorm_matmul2",
    )(xs, g.reshape(1, D), shift, scale, w, w_hi)


def _matmul_res_kernel(a_ref, w_ref, x_ref, gate_ref, o_ref):
    y = jnp.dot(a_ref[...], w_ref[...], preferred_element_type=F32)
    o_ref[...] = x_ref[...] + gate_ref[0] * y


def matmul_residual(a, w, xs, gate, dims, row_start=0, tm=512):
    M, D = xs.shape
    K = a.shape[1]
    off = row_start // tm
    return pl.pallas_call(
        _matmul_res_kernel,
        grid=(a.shape[0] // tm,),
        in_specs=[pl.BlockSpec((tm, K), lambda i: (i, 0)),
                  pl.BlockSpec((K, D), lambda i: (0, 0)),
                  pl.BlockSpec((tm, D), lambda i: (i + off, 0)),
                  _mod_spec(tm, off, dims)],
        out_specs=pl.BlockSpec((tm, D), lambda i: (i + off, 0)),
        out_shape=jax.ShapeDtypeStruct((M, D), F32),
        input_output_aliases={2: 0},
        compiler_params=_cparams(("parallel",)),
        name="matmul_residual",
    )(a, w, xs, gate)


def _ffn_kernel(x_ref, g_ref, sh_ref, sc_ref, gate_ref, w1_ref, w3_ref, w2_ref, o_ref, a_ref, *, tf):
    x = x_ref[...]
    h = _prenorm(x, g_ref[...], sh_ref[0], sc_ref[0]).astype(BF16)
    F = w1_ref.shape[1]
    for f0 in range(0, F, tf):
        gg = jnp.dot(h, w1_ref[:, f0:f0 + tf], preferred_element_type=F32)
        uu = jnp.dot(h, w3_ref[:, f0:f0 + tf], preferred_element_type=F32)
        a_ref[:, f0:f0 + tf] = (_silu(gg) * uu).astype(BF16)
    y = jnp.dot(a_ref[...], w2_ref[...], preferred_element_type=F32)
    o_ref[...] = x + gate_ref[0] * y


def ffn_sublayer(xs, g, shift, scale, gate, w1, w3, w2, dims, row_start=0, tm=512, tf=256):
    M, D = xs.shape
    F = w1.shape[1]
    off = row_start // tm
    resident = dict(pipeline_mode=pl.Buffered(1))
    return pl.pallas_call(
        functools.partial(_ffn_kernel, tf=tf),
        grid=(M // tm - off,),
        in_specs=[pl.BlockSpec((tm, D), lambda i: (i + off, 0)),
                  pl.BlockSpec((1, D), lambda i: (0, 0)),
                  _mod_spec(tm, off, dims), _mod_spec(tm, off, dims), _mod_spec(tm, off, dims),
                  pl.BlockSpec((D, F), lambda i: (0, 0), **resident),
                  pl.BlockSpec((D, F), lambda i: (0, 0), **resident),
                  pl.BlockSpec((F, D), lambda i: (0, 0), **resident)],
        out_specs=pl.BlockSpec((tm, D), lambda i: (i + off, 0)),
        out_shape=jax.ShapeDtypeStruct((M, D), F32),
        scratch_shapes=[pltpu.VMEM((tm, F), BF16)],
        input_output_aliases={0: 0},
        compiler_params=_cparams(("parallel",)),
        name="ffn_sublayer",
    )(xs, g.reshape(1, D), shift, scale, gate, w1, w3, w2)


def _na_bias_table(rpb):
    H = rpb.shape[0]
    col = np.arange(GRID_W)
    c0 = np.clip(col - NA_COLS // 2, 0, GRID_W - NA_COLS)
    in_win = (col[None, :] >= c0[:, None]) & (col[None, :] < c0[:, None] + NA_COLS)
    dc = np.clip(col[None, :] - col[:, None], -(NA_COLS - 1), NA_COLS - 1) + NA_COLS - 1
    var = np.arange(NA_ROWS)[:, None]
    t = np.arange(NA_ROWS)[None, :]
    dr = t - var + NA_ROWS - 1
    tab = rpb[:, dr][:, :, :, dc]
    tab = jnp.where(jnp.asarray(in_win)[None, None, None], tab * LOG2E, NEG_INF)
    tab = tab.reshape(H // 2, 2, NA_ROWS, NA_ROWS, GRID_W, GRID_W)
    tab = jnp.transpose(tab, (0, 2, 3, 5, 1, 4))
    return tab.reshape(H // 2, NA_ROWS, NA_ROWS * GRID_W, 2 * GRID_W)


def _na_kernel(q_ref, k_ref, v_ref, kc_ref, vc_ref, bias_ref, o_ref, *, rows):
    W = GRID_W
    KW = NA_ROWS * W
    sub = lax.broadcasted_iota(jnp.int32, (2 * W, LANE), 0)
    lane = lax.broadcasted_iota(jnp.int32, (2 * W, LANE), 1)
    same_head = (sub < W) == (lane < W)
    lane_h = lax.broadcasted_iota(jnp.int32, (W, LANE), 1)
    kc = kc_ref[...]
    vc = vc_ref[...]

    kbase, qbase, var, scores, probs = {}, {}, {}, {}, {}

    def score_stage(rg, r):
        row = rg * NA_ROWS + r
        r0 = jnp.clip(row - NA_ROWS // 2, 0, rows - NA_ROWS)
        var[r] = row - r0
        kbase[r] = pl.multiple_of(r0 * W, W)
        qbase[r] = pl.multiple_of(row * W, W)
        qr = q_ref[pl.ds(qbase[r], W), :]
        q2 = jnp.concatenate([qr, qr], axis=0)
        q2 = jnp.where(same_head, q2, jnp.zeros_like(q2))
        kw = k_ref[pl.ds(kbase[r], KW), :]
        scores[r] = (lax.dot_general(kw, q2, NT, preferred_element_type=F32),
                     lax.dot_general(kc, q2, NT, preferred_element_type=F32))

    def softmax_stage(rg, r):
        s_loc, s_ctx = scores.pop(r)
        s_loc = s_loc + bias_ref[0, var[r]]
        m = jnp.maximum(jnp.max(s_loc, axis=0, keepdims=True), jnp.max(s_ctx, axis=0, keepdims=True))
        p_loc = jnp.exp2(s_loc - m)
        p_ctx = jnp.exp2(s_ctx - m)
        inv = 1.0 / (jnp.sum(p_loc, axis=0, keepdims=True) + jnp.sum(p_ctx, axis=0, keepdims=True))
        probs[r] = ((p_loc * inv).astype(BF16), (p_ctx * inv).astype(BF16))

    def value_stage(rg, r):
        p_loc, p_ctx = probs.pop(r)
        vw = v_ref[pl.ds(kbase[r], KW), :]
        o = (lax.dot_general(p_loc, vw, TN, preferred_element_type=F32)
             + lax.dot_general(p_ctx, vc, TN, preferred_element_type=F32))
        o_sel = jnp.where(lane_h < W, o[:W], o[W:])
        o_ref[pl.ds(qbase[r], W), :] = o_sel.astype(o_ref.dtype)

    def row_group(rg, carry):
        for stage in (score_stage, softmax_stage, value_stage):
            for r in range(NA_ROWS):
                stage(rg, r)
        return carry

    lax.fori_loop(0, rows // NA_ROWS, row_group, 0)


def na_attention(qkv, bias_tab, dims, L):
    BL, N, B, D = dims
    rows = N // GRID_W
    HP = NA_HEADS // 2
    tq = NA_ROWS * GRID_W
    assert BL % N == 0 and N % tq == 0 and rows >= NA_ROWS
    return pl.pallas_call(
        functools.partial(_na_kernel, rows=rows),
        grid=(HP, B),
        in_specs=[pl.BlockSpec((N, LANE), lambda hp, b: (BL // N + b, hp)),
                  pl.BlockSpec((N, LANE), lambda hp, b: (BL // N + b, HP + hp)),
                  pl.BlockSpec((N, LANE), lambda hp, b: (BL // N + b, 2 * HP + hp)),
                  pl.BlockSpec((L, LANE), lambda hp, b: (b, HP + hp)),
                  pl.BlockSpec((L, LANE), lambda hp, b: (b, 2 * HP + hp)),
                  pl.BlockSpec((1, NA_ROWS, tq, LANE), lambda hp, b: (hp, 0, 0, 0))],
        out_specs=pl.BlockSpec((N, LANE), lambda hp, b: (b, hp)),
        out_shape=jax.ShapeDtypeStruct((B * N, D), BF16),
        compiler_params=_cparams(("parallel", "parallel")),
        name="na_attention",
    )(qkv, qkv, qkv, qkv, qkv, bias_tab)


def _ctx_attn_kernel(q_ref, k_ref, v_ref, o_ref):
    q = q_ref[...]
    k = k_ref[...]
    v = v_ref[...]
    lane = lax.broadcasted_iota(jnp.int32, q.shape, 1)
    outs = []
    for hh in range(2):
        msk = (lane < GRID_W) if hh == 0 else (lane >= GRID_W)
        qh = jnp.where(msk, q, jnp.zeros_like(q))
        s = lax.dot_general(qh, k, NT, preferred_element_type=F32)
        m = jnp.max(s, axis=-1, keepdims=True)
        p = jnp.exp2(s - m)
        p = (p / jnp.sum(p, axis=-1, keepdims=True)).astype(BF16)
        outs.append(jnp.dot(p, v, preferred_element_type=F32))
    o_ref[...] = jnp.where(lane < GRID_W, outs[0], outs[1]).astype(o_ref.dtype)


def ctx_attention(qkv, dims, L):
    BL, N, B, D = dims
    HP = NA_HEADS // 2
    return pl.pallas_call(
        _ctx_attn_kernel,
        grid=(B, HP),
        in_specs=[pl.BlockSpec((L, LANE), lambda b, hp: (b, hp)),
                  pl.BlockSpec((L, LANE), lambda b, hp: (b, HP + hp)),
                  pl.BlockSpec((L, LANE), lambda b, hp: (b, 2 * HP + hp))],
        out_specs=pl.BlockSpec((L, LANE), lambda b, hp: (b, hp)),
        out_shape=jax.ShapeDtypeStruct((BL, D), BF16),
        compiler_params=_cparams(("parallel", "parallel")),
        name="ctx_attention",
    )(qkv, qkv, qkv)


def _gdn_feat_kernel(cur_ref, prev_ref, next_ref, cw_ref, o_ref, *, tm, BL, L, N):
    i = pl.program_id(0)
    j = pl.program_id(1)
    r0 = i * tm
    in_ctx = r0 < BL
    seg = jnp.where(in_ctx, L, N)
    off = jnp.where(in_ctx, r0, r0 - BL)
    keep_prev = jnp.where((off % seg) == 0, 0.0, 1.0)
    keep_next = jnp.where(((off + tm) % seg) == 0, 0.0, 1.0)
    row = lax.broadcasted_iota(jnp.int32, (tm, LANE), 0)
    qk_scale = jnp.where(j == 0, GDN_DK ** -0.5, 1.0)
    for c in range(cur_ref.shape[1] // LANE):
        sl = slice(c * LANE, (c + 1) * LANE)
        x = cur_ref[:, sl].astype(F32)
        pv = prev_ref[:, sl].astype(F32) * keep_prev
        nx = next_ref[:, sl].astype(F32) * keep_next
        w = cw_ref[:, sl]
        xm1 = jnp.where(row == 0, pv[15:16], pltpu.roll(x, 1, 0))
        xm2 = jnp.where(row == 0, pv[14:15], jnp.where(row == 1, pv[15:16], pltpu.roll(x, 2, 0)))
        xp1 = jnp.where(row == tm - 1, nx[0:1], pltpu.roll(x, tm - 1, 0))
        y = _silu(w[0:1] * xm2 + w[1:2] * xm1 + w[2:3] * x + w[3:4] * xp1)
        fac = lax.rsqrt(jnp.sum(y * y, axis=-1, keepdims=True) + NORM_EPS) * qk_scale
        o_ref[:, sl] = (y * jnp.where(j == 2, 1.0, fac)).astype(o_ref.dtype)


def gdn_features(proj, conv_w, dims, L, tm=256):
    BL, N, B, D = dims
    M = proj.shape[0]
    C = GDN_HEADS * GDN_DK
    hb = tm // 16
    nhb = M // 16
    return pl.pallas_call(
        functools.partial(_gdn_feat_kernel, tm=tm, BL=BL, L=L, N=N),
        grid=(M // tm, 3),
        in_specs=[pl.BlockSpec((tm, C), lambda i, j: (i, j)),
                  pl.BlockSpec((16, C), lambda i, j: (jnp.maximum(i * hb - 1, 0), j)),
                  pl.BlockSpec((16, C), lambda i, j: (jnp.minimum((i + 1) * hb, nhb - 1), j)),
                  pl.BlockSpec((4, C), lambda i, j: (0, j))],
        out_specs=pl.BlockSpec((tm, C), lambda i, j: (i, j)),
        out_shape=jax.ShapeDtypeStruct((M, 3 * C), BF16),
        compiler_params=_cparams(("parallel", "arbitrary")),
        name="gdn_features",
    )(proj, proj, proj, conv_w)


def _gdn_intra_kernel(f_ref, ab_ref, pa_ref, pdt_ref, lm_ref, sm_ref,
                      w_ref, qd_ref, kt_ref, u_ref, aqk_ref, egl_ref, *, tm, cpg):
    d = pl.program_id(1)
    C = GDN_CHUNK
    H = GDN_HEADS
    DK = GDN_DK
    Lm = lm_ref[0]
    incl = Lm > 0.5
    strict = sm_ref[0] > 0.5
    eye = (lax.broadcasted_iota(jnp.int32, (C, C), 0) == lax.broadcasted_iota(jnp.int32, (C, C), 1)).astype(F32)
    lane = lax.broadcasted_iota(jnp.int32, (C, LANE), 1)
    pa = pa_ref[0]
    pdt = pdt_ref[0]

    def mm(a, b):
        return jnp.dot(a.astype(BF16), b.astype(BF16), preferred_element_type=F32)

    def nt(a, b):
        return lax.dot_general(a.astype(BF16), b.astype(BF16), NT, preferred_element_type=F32)

    def chunk_group(cg, carry):
        pairs = [(cc, h) for cc in range(cpg) for h in range(H)]
        rows = [pl.multiple_of((cg * cpg + cc) * C, C) for cc in range(cpg)]
        gc_all, gc_t, beta_all, g_last = [], [], [], []
        for cc in range(cpg):
            ab = ab_ref[pl.ds(rows[cc], C), :]
            sp = jnp.maximum(ab + pdt, 0.0) + jnp.log1p(jnp.exp(-jnp.abs(ab + pdt)))
            gval = jnp.where(lane < H, pa * sp, 0.0)
            ga = jnp.dot(Lm, gval, precision=HI, preferred_element_type=F32)
            gc_all.append(ga)
            gc_t.append(ga.T)
            beta_all.append(jax.nn.sigmoid(ab))
            g_last.append(jnp.where(d == 0, ga[C - 1:C], ga[0:1]))
        ld = lambda cc, col: f_ref[pl.ds(rows[cc], C), col * DK:(col + 1) * DK]
        q16 = [ld(cc, h) for cc, h in pairs]
        k16 = [ld(cc, H + h) for cc, h in pairs]
        k = [t.astype(F32) for t in k16]
        gc = [jnp.broadcast_to(gc_all[cc][:, h:h + 1], (C, DK)) for cc, h in pairs]
        beta = [jnp.broadcast_to(beta_all[cc][:, H + h:H + h + 1], (C, DK)) for cc, h in pairs]
        decay = []
        for i, (cc, h) in enumerate(pairs):
            diff = gc[i][:, :C] - gc_t[cc][h:h + 1, :]
            decay.append(jnp.where(incl, jnp.exp(jnp.where(incl, diff, 0.0)), 0.0))
        kb = [k[i] * beta[i] for i in range(len(pairs))]
        nm = [jnp.where(strict, nt(kb[i], k16[i]) * decay[i], 0.0) for i in range(len(pairs))]
        xinv = [eye - t for t in nm]
        pw = [mm(t, t) for t in nm]
        for it in range(5):
            if it < 4:
                z = [mm(jnp.concatenate([xinv[i], pw[i]], axis=0), pw[i]) for i in range(len(pairs))]
                xinv = [xinv[i] + z[i][:C] for i in range(len(pairs))]
                pw = [t[C:] for t in z]
            else:
                xinv = [xinv[i] + mm(xinv[i], pw[i]) for i in range(len(pairs))]
        eg = [jnp.exp(t) for t in gc]
        uw = [mm(xinv[i], jnp.concatenate([ld(cc, 2 * H + h).astype(F32) * beta[i], kb[i] * eg[i]], axis=1))
              for i, (cc, h) in enumerate(pairs)]
        u = [t[:, :DK] for t in uw]
        w = [t[:, DK:] for t in uw]
        aqk = [nt(q16[i], k16[i]) * decay[i] for i in range(len(pairs))]
        for i, (cc, h) in enumerate(pairs):
            sl = slice(h * DK, (h + 1) * DK)
            r = rows[cc]
            gl = jnp.broadcast_to(g_last[cc][:, h:h + 1], (1, DK))
            w_ref[0, pl.ds(r, C), sl] = w[i].astype(w_ref.dtype)
            u_ref[0, pl.ds(r, C), sl] = u[i]
            qd_ref[0, pl.ds(r, C), sl] = (q16[i].astype(F32) * eg[i]).astype(qd_ref.dtype)
            kt_ref[0, pl.ds(r, C), sl] = (k[i] * jnp.exp(gl - gc[i])).astype(kt_ref.dtype)
            aqk_ref[0, pl.ds(r, C), h * C:(h + 1) * C] = aqk[i].astype(aqk_ref.dtype)
            egl_ref[0, cg * cpg + cc, h:h + 1, :] = jnp.exp(gl)
        return carry

    lax.fori_loop(0, tm // (C * cpg), chunk_group, 0)


def gdn_intra(feats, ab, a_log, dt_bias, tm=256):
    M = feats.shape[0]
    H, DK, C = GDN_HEADS, GDN_DK, GDN_CHUNK
    HD = H * DK
    pa = jnp.zeros((2, 1, LANE), F32).at[:, 0, :H].set(-jnp.exp(a_log.astype(F32)))
    pdt = jnp.zeros((2, 1, LANE), F32).at[:, 0, :H].set(dt_bias.astype(F32))
    idx = np.arange(C)
    lower = idx[:, None] >= idx[None, :]
    lm = jnp.asarray(np.stack([lower, lower.T]).astype(np.float32))
    sm = jnp.asarray(np.stack([idx[:, None] > idx[None, :], idx[:, None] < idx[None, :]]).astype(np.float32))
    big = lambda dt: jax.ShapeDtypeStruct((2, M, HD), dt)
    dspec = pl.BlockSpec((1, tm, HD), lambda i, d: (d, i, 0))
    return pl.pallas_call(
        functools.partial(_gdn_intra_kernel, tm=tm, cpg=4),
        grid=(M // tm, 2),
        in_specs=[pl.BlockSpec((tm, 3 * HD), lambda i, d: (i, 0)),
                  pl.BlockSpec((tm, LANE), lambda i, d: (i, d)),
                  pl.BlockSpec((1, 1, LANE), lambda i, d: (d, 0, 0)),
                  pl.BlockSpec((1, 1, LANE), lambda i, d: (d, 0, 0)),
                  pl.BlockSpec((1, C, C), lambda i, d: (d, 0, 0)),
                  pl.BlockSpec((1, C, C), lambda i, d: (d, 0, 0))],
        out_specs=[dspec, dspec, dspec, dspec,
                   pl.BlockSpec((1, tm, H * C), lambda i, d: (d, i, 0)),
                   pl.BlockSpec((1, tm // C, H, LANE), lambda i, d: (d, i, 0, 0))],
        out_shape=[big(BF16), big(BF16), big(BF16), big(F32),
                   jax.ShapeDtypeStruct((2, M, H * C), BF16),
                   jax.ShapeDtypeStruct((2, M // C, H, LANE), F32)],
        compiler_params=_cparams(("parallel", "arbitrary")),
        name="gdn_intra",
    )(feats, ab, pa, pdt, lm, sm)


def _gdn_scan_kernel(*refs, nch):
    ins = (refs[0:6], refs[6:12])
    outs = refs[12:14]
    s_ref = refs[14]
    C = GDN_CHUNK
    DK = GDN_DK
    H = GDN_HEADS

    @pl.when(pl.program_id(1) == 0)
    def _():
        s_ref[...] = jnp.zeros_like(s_ref)

    def chunk(j, carry):
        cs = (j, nch - 1 - j)
        rs = tuple(pl.multiple_of(c * C, C) for c in cs)
        pairs = [(d, h) for d in range(2) for h in range(H)]
        sl = lambda h: slice(h * DK, (h + 1) * DK)
        ld = lambda k, d, h: ins[d][k][0, pl.ds(rs[d], C), sl(h)]
        S = [s_ref[d, h] for d, h in pairs]
        Sb = [t.astype(BF16) for t in S]
        ws = [jnp.dot(ld(0, d, h), Sb[i], preferred_element_type=F32) for i, (d, h) in enumerate(pairs)]
        qs = [jnp.dot(ld(1, d, h), Sb[i], preferred_element_type=F32) for i, (d, h) in enumerate(pairs)]
        vb = [(ld(3, d, h) - ws[i]).astype(BF16) for i, (d, h) in enumerate(pairs)]
        av = [jnp.dot(ins[d][4][0, pl.ds(rs[d], C), h * C:(h + 1) * C], vb[i], preferred_element_type=F32)
              for i, (d, h) in enumerate(pairs)]
        kv = [lax.dot_general(ld(2, d, h), vb[i], TN, preferred_element_type=F32)
              for i, (d, h) in enumerate(pairs)]
        for i, (d, h) in enumerate(pairs):
            outs[d][pl.ds(rs[d], C), sl(h)] = qs[i] + av[i]
            s_ref[d, h] = S[i] * ins[d][5][0, cs[d], h:h + 1, :] + kv[i]
        return carry

    lax.fori_loop(0, nch, chunk, 0)


def gdn_scan(w, qd, kt, u, aqk, egl, dims, L):
    BL, N, B, D = dims
    M = w.shape[1]
    H, DK, C = GDN_HEADS, GDN_DK, GDN_CHUNK
    HD = H * DK
    blk = L
    nlat = N // blk
    assert N % blk == 0 and blk % C == 0

    def rb(d):
        def index(b, s):
            lat = BL // blk + b * nlat + (s - 1 if d == 0 else nlat - s)
            return jnp.where(s == 0, b, lat)
        return index

    in_specs, args = [], []
    for d in range(2):
        r = rb(d)
        big = pl.BlockSpec((1, blk, HD), lambda b, s, r=r, d=d: (d, r(b, s), 0))
        in_specs += [big, big, big, big,
                     pl.BlockSpec((1, blk, H * C), lambda b, s, r=r, d=d: (d, r(b, s), 0)),
                     pl.BlockSpec((1, blk // C, H, LANE), lambda b, s, r=r, d=d: (d, r(b, s), 0, 0))]
        args += [w, qd, kt, u, aqk, egl]
    return pl.pallas_call(
        functools.partial(_gdn_scan_kernel, nch=blk // C),
        grid=(B, 1 + nlat),
        in_specs=in_specs,
        out_specs=[pl.BlockSpec((blk, HD), lambda b, s, r=rb(d): (r(b, s), 0)) for d in range(2)],
        out_shape=[jax.ShapeDtypeStruct((M, HD), F32)] * 2,
        scratch_shapes=[pltpu.VMEM((2, H, DK, DK), F32)],
        compiler_params=_cparams(("parallel", "arbitrary")),
        name="gdn_scan",
    )(*args)


def _gdn_out_kernel(of_ref, ob_ref, z_ref, ng_ref, w_ref, x_ref, gate_ref, o_ref, a_ref):
    DK = GDN_DK
    for h in range(GDN_HEADS):
        sl = slice(h * DK, (h + 1) * DK)
        o = of_ref[:, sl] + ob_ref[:, sl]
        y = o * lax.rsqrt(jnp.mean(o * o, axis=-1, keepdims=True) + NORM_EPS) * ng_ref[...]
        z = z_ref[:, sl].astype(F32)
        a_ref[:, sl] = (y * _silu(z)).astype(BF16)
    y = jnp.dot(a_ref[...], w_ref[...], preferred_element_type=F32)
    o_ref[...] = x_ref[...] + gate_ref[0] * y


def gdn_output(o_f, o_b, proj, norm_g, w_o, xs, gate, dims, row_start=0, tm=512):
    M, D = xs.shape
    HD = GDN_HEADS * GDN_DK
    off = row_start // tm
    return pl.pallas_call(
        _gdn_out_kernel,
        grid=(M // tm - off,),
        in_specs=[pl.BlockSpec((tm, HD), lambda i: (i + off, 0)),
                  pl.BlockSpec((tm, HD), lambda i: (i + off, 0)),
                  pl.BlockSpec((tm, HD), lambda i: (i + off, 3)),
                  pl.BlockSpec((1, GDN_DK), lambda i: (0, 0)),
                  pl.BlockSpec((HD, D), lambda i: (0, 0)),
                  pl.BlockSpec((tm, D), lambda i: (i + off, 0)),
                  _mod_spec(tm, off, dims)],
        out_specs=pl.BlockSpec((tm, D), lambda i: (i + off, 0)),
        out_shape=jax.ShapeDtypeStruct((M, D), F32),
        scratch_shapes=[pltpu.VMEM((tm, HD), BF16)],
        input_output_aliases={5: 0},
        compiler_params=_cparams(("parallel",)),
        name="gdn_output",
    )(o_f, o_b, proj, norm_g.reshape(1, GDN_DK), w_o, xs, gate)


def _pool_tables(tm, L):
    amats, invs = [], []
    t = np.arange(tm)
    for seg in (L, GRID_W):
        a_v, i_v = [], []
        tl = t % seg
        for win in POOL_WINDOWS:
            lo = np.clip(tl - win // 2, 0, seg)
            hi = np.clip(tl + win // 2, 0, seg)
            same = (t[:, None] // seg) == (t[None, :] // seg)
            a = same & (tl[None, :] >= lo[:, None]) & (tl[None, :] < hi[:, None])
            a_v.append(a.astype(np.float32))
            i_v.append(np.broadcast_to((1.0 / (hi - lo))[:, None], (tm, LANE)).astype(np.float32))
        amats.append(np.stack(a_v))
        invs.append(np.stack(i_v))
    return jnp.asarray(np.stack(amats), BF16), jnp.asarray(np.stack(invs), F32)


def _pool_kernel(x_ref, g_ref, sh_ref, sc_ref, gate_ref, a_ref, ic_ref, pw_ref, ls_ref, o_ref):
    x = x_ref[...]
    h = _prenorm(x, g_ref[...], sh_ref[0], sc_ref[0])
    G = pw_ref.shape[1]
    for gi in range(len(POOL_WINDOWS)):
        sl = slice(gi * G, (gi + 1) * G)
        hg = h[:, sl]
        hi = hg.astype(BF16)
        lo = (hg - hi.astype(F32)).astype(BF16)
        am = a_ref[0, gi]
        wsum = jnp.dot(am, hi, preferred_element_type=F32) + jnp.dot(am, lo, preferred_element_type=F32)
        ic = ic_ref[0, gi]
        mean = wsum * jnp.concatenate([ic] * (G // LANE), axis=-1)
        pooled = (mean - hg).astype(BF16)
        y = jnp.dot(pooled, pw_ref[gi], preferred_element_type=F32) * ls_ref[:, sl]
        o_ref[:, sl] = x[:, sl] + gate_ref[0][:, sl] * y


def pool_sublayer(xs, g, shift, scale, gate, pool_w, ls, dims, L, row_start=0, tm=256):
    BL, N, B, D = dims
    M = xs.shape[0]
    assert tm == L and tm % GRID_W == 0
    amat, inv = _pool_tables(tm, L)
    off = row_start // tm
    nct = BL // tm
    G = D // len(POOL_WINDOWS)
    variant = lambda i: jnp.where(i + off < nct, 0, 1)
    return pl.pallas_call(
        _pool_kernel,
        grid=(M // tm - off,),
        in_specs=[pl.BlockSpec((tm, D), lambda i: (i + off, 0)),
                  pl.BlockSpec((1, D), lambda i: (0, 0)),
                  _mod_spec(tm, off, dims), _mod_spec(tm, off, dims), _mod_spec(tm, off, dims),
                  pl.BlockSpec((1, 4, tm, tm), lambda i: (variant(i), 0, 0, 0)),
                  pl.BlockSpec((1, 4, tm, LANE), lambda i: (variant(i), 0, 0, 0)),
                  pl.BlockSpec((4, G, G), lambda i: (0, 0, 0)),
                  pl.BlockSpec((1, D), lambda i: (0, 0))],
        out_specs=pl.BlockSpec((tm, D), lambda i: (i + off, 0)),
        out_shape=jax.ShapeDtypeStruct((M, D), F32),
        input_output_aliases={0: 0},
        compiler_params=_cparams(("parallel",)),
        name="pool_sublayer",
    )(xs, g.reshape(1, D), shift, scale, gate, amat, inv, pool_w.astype(BF16), ls.reshape(1, D))


def _router_kernel(x_ref, g_ref, sh_ref, sc_ref, wr_ref, h_ref, ti_ref, tp_ref):
    h = _prenorm(x_ref[...], g_ref[...], sh_ref[0], sc_ref[0])
    h_ref[...] = h
    h_hi = h.astype(BF16)
    h_lo = (h - h_hi.astype(F32)).astype(BF16)
    wr = wr_ref[...]
    both = jnp.dot(h_hi, wr, preferred_element_type=F32)
    logits = both[:, :LANE] + both[:, LANE:] + jnp.dot(h_lo, wr[:, :LANE], preferred_element_type=F32)
    lane = lax.broadcasted_iota(jnp.int32, logits.shape, 1)
    valid = lane < N_EXPERTS
    lg = jnp.where(valid, logits, NEG_INF)
    e = jnp.where(valid, jnp.exp(lg - jnp.max(lg, axis=-1, keepdims=True)), 0.0)
    probs = jnp.where(valid, e / jnp.sum(e, axis=-1, keepdims=True), -1.0)
    p1 = jnp.max(probs, axis=-1, keepdims=True)
    i1 = jnp.min(jnp.where(probs == p1, lane, LANE), axis=-1, keepdims=True)
    rest = jnp.where(lane == i1, -1.0, probs)
    p2 = jnp.max(rest, axis=-1, keepdims=True)
    i2 = jnp.min(jnp.where(rest == p2, lane, LANE), axis=-1, keepdims=True)
    tot = p1 + p2
    tp_ref[...] = jnp.where(lane == 0, p1 / tot, jnp.where(lane == 1, p2 / tot, 0.0))
    ti_ref[...] = jnp.where(lane == 0, i1, jnp.where(lane == 1, i2, 0))


def moe_router(xs, g, shift, scale, w_router, dims, row_start=0, tm=1024):
    M, D = xs.shape
    off = row_start // tm
    wr = jnp.zeros((D, LANE), F32).at[:, :N_EXPERTS].set(w_router)
    wr_hi = wr.astype(BF16)
    wr = jnp.concatenate([wr_hi, (wr - wr_hi.astype(F32)).astype(BF16)], axis=1)
    Mo = M - row_start
    ospec = lambda w: pl.BlockSpec((tm, w), lambda i: (i, 0))
    return pl.pallas_call(
        _router_kernel,
        grid=(Mo // tm,),
        in_specs=[pl.BlockSpec((tm, D), lambda i: (i + off, 0)),
                  pl.BlockSpec((1, D), lambda i: (0, 0)),
                  _mod_spec(tm, off, dims), _mod_spec(tm, off, dims),
                  pl.BlockSpec((D, 2 * LANE), lambda i: (0, 0))],
        out_specs=[ospec(D), ospec(LANE), ospec(LANE)],
        out_shape=[jax.ShapeDtypeStruct((Mo, D), F32),
                   jax.ShapeDtypeStruct((Mo, LANE), jnp.int32),
                   jax.ShapeDtypeStruct((Mo, LANE), F32)],
        compiler_params=_cparams(("parallel",)),
        name="moe_router",
    )(xs, g.reshape(1, D), shift, scale, wr)


def _route_positions(top_e):
    n_tok = top_e.shape[0]
    e_flat = top_e.reshape(-1)
    onehot = (e_flat[:, None] == jnp.arange(N_EXPERTS, dtype=jnp.int32)[None, :]).astype(jnp.int32)
    csum = jnp.cumsum(onehot, axis=0)
    rank = jnp.sum(csum * onehot, axis=-1) - 1
    counts = csum[-1]
    padded = ((counts + MOE_BLOCK - 1) // MOE_BLOCK) * MOE_BLOCK
    ends = jnp.cumsum(padded)
    starts = ends - padded
    pos = (jnp.sum(onehot * starts[None, :], axis=-1) + rank).reshape(n_tok, TOP_K)
    n_blocks = -(-(n_tok * TOP_K) // MOE_BLOCK) + N_EXPERTS
    block_e = jnp.minimum(jnp.searchsorted(ends, jnp.arange(n_blocks) * MOE_BLOCK, side='right'),
                          N_EXPERTS - 1).astype(jnp.int32)
    pos = pos.astype(jnp.int32)
    cap = n_blocks * MOE_BLOCK
    n_pairs = n_tok * TOP_K
    pair_row = (jnp.arange(TOP_K, dtype=jnp.int32)[None, :] * n_tok
                + jnp.arange(n_tok, dtype=jnp.int32)[:, None]).reshape(-1)
    dst = (n_pairs + jnp.arange(cap, dtype=jnp.int32)).at[pos.reshape(-1)].set(pair_row, unique_indices=True)
    src = jnp.where(dst < n_pairs, dst % n_tok, 0)
    lead = n_pairs + cap + jnp.arange(MOE_BLOCK, dtype=jnp.int32)
    dst = jnp.concatenate([lead, dst])
    n_used = (ends[-1] // MOE_BLOCK).astype(jnp.int32).reshape(1)
    return src, dst, block_e, n_used, n_blocks


def _expert_kernel(be_ref, nu_ref, tokc_ref, tokn_ref, dstp_ref, dstc_ref, h_hbm, w1_ref, w3_ref, w2_ref,
                   y_hbm, xg_ref, ob_ref, a_ref, sem_g, sem_o, *, tf, n_blocks):
    del be_ref
    i = pl.program_id(0)
    n_used = nu_ref[0]
    slot = i % 2
    other = 1 - slot
    F = w1_ref.shape[2]
    nchunk = F // tf
    per = -(-MOE_BLOCK // nchunk)

    def gather_copy(tok_ref, s, t):
        return pltpu.make_async_copy(h_hbm.at[pl.ds(tok_ref[0, 0, t], 1), :],
                                     xg_ref.at[s, pl.ds(t, 1), :], sem_g.at[s])

    def scatter_copy(dst_ref, s, t):
        return pltpu.make_async_copy(ob_ref.at[s, pl.ds(t, 1), :],
                                     y_hbm.at[pl.ds(dst_ref[0, 0, t], 1), :], sem_o.at[s])

    def start_loop(make, ref, s):
        def body(t, carry):
            make(ref, s, t).start()
            return carry
        lax.fori_loop(0, MOE_BLOCK, body, 0, unroll=8)

    def wait_gather(s):
        pltpu.make_async_copy(h_hbm.at[pl.ds(0, MOE_BLOCK), :], xg_ref.at[s], sem_g.at[s]).wait()

    def wait_scatter(s):
        pltpu.make_async_copy(ob_ref.at[s], y_hbm.at[pl.ds(0, MOE_BLOCK), :], sem_o.at[s]).wait()

    @pl.when(i == 0)
    def _():
        ob_ref[...] = jnp.zeros_like(ob_ref)
        start_loop(gather_copy, tokc_ref, 0)

    wait_gather(slot)

    @pl.when(i >= 1)
    def _():
        wait_scatter(slot)

    @pl.when(i < n_used)
    def _():
        hb = xg_ref[slot].astype(BF16)
        for c in range(nchunk):
            cs = slice(c * tf, (c + 1) * tf)
            gg = jnp.dot(hb, w1_ref[0, :, cs], preferred_element_type=F32)
            uu = jnp.dot(hb, w3_ref[0, :, cs], preferred_element_type=F32)
            a_ref[:, cs] = (_silu(gg) * uu).astype(BF16)
            for t in range(c * per, min((c + 1) * per, MOE_BLOCK)):
                gather_copy(tokn_ref, other, t).start()
                scatter_copy(dstp_ref, other, t).start()
        ob_ref[slot] = jnp.dot(a_ref[...], w2_ref[0], preferred_element_type=F32)

    @pl.when(i >= n_used)
    def _():
        start_loop(gather_copy, tokn_ref, other)
        start_loop(scatter_copy, dstp_ref, other)
        ob_ref[slot] = jnp.zeros(ob_ref.shape[1:], F32)

    @pl.when(i == n_blocks - 1)
    def _():
        start_loop(scatter_copy, dstc_ref, slot)
        wait_scatter(other)
        wait_scatter(slot)
        wait_gather(other)


def moe_experts(h, src, dst, block_e, n_used, w1, w3, w2, n_blocks, tf=256):
    n_tok, D = h.shape
    F = w1.shape[2]
    last = n_blocks - 1
    smem = lambda index: pl.BlockSpec((1, 1, MOE_BLOCK), index, memory_space=pltpu.SMEM)
    resident = dict(pipeline_mode=pl.Buffered(1))
    grid_spec = pltpu.PrefetchScalarGridSpec(
        num_scalar_prefetch=2,
        grid=(n_blocks,),
        in_specs=[smem(lambda i, be, nu: (i, 0, 0)),
                  smem(lambda i, be, nu: (jnp.minimum(i + 1, last), 0, 0)),
                  smem(lambda i, be, nu: (i, 0, 0)),
                  smem(lambda i, be, nu: (i + 1, 0, 0)),
                  pl.BlockSpec(memory_space=pl.ANY),
                  pl.BlockSpec((1, D, F), lambda i, be, nu: (be[i], 0, 0), **resident),
                  pl.BlockSpec((1, D, F), lambda i, be, nu: (be[i], 0, 0), **resident),
                  pl.BlockSpec((1, F, D), lambda i, be, nu: (be[i], 0, 0), **resident)],
        out_specs=pl.BlockSpec(memory_space=pl.ANY),
        scratch_shapes=[pltpu.VMEM((2, MOE_BLOCK, D), F32), pltpu.VMEM((2, MOE_BLOCK, D), F32),
                        pltpu.VMEM((MOE_BLOCK, F), BF16),
                        pltpu.SemaphoreType.DMA((2,)), pltpu.SemaphoreType.DMA((2,))],
    )
    src3 = src.reshape(n_blocks, 1, MOE_BLOCK)
    dst3 = dst.reshape(n_blocks + 1, 1, MOE_BLOCK)
    n_rows = n_tok * TOP_K + (n_blocks + 1) * MOE_BLOCK
    return pl.pallas_call(
        functools.partial(_expert_kernel, tf=tf, n_blocks=n_blocks),
        grid_spec=grid_spec,
        out_shape=jax.ShapeDtypeStruct((n_rows, D), F32),
        compiler_params=_cparams(("arbitrary",)),
        name="moe_experts",
    )(block_e, n_used, src3, src3, dst3, dst3, h, w1, w3, w2)


def _combine_kernel(y0_ref, y1_ref, tp_ref, x_ref, gate_ref, o_ref):
    tp = tp_ref[...]
    y = tp[:, 0:1] * y0_ref[...] + tp[:, 1:2] * y1_ref[...]
    o_ref[...] = x_ref[...] + gate_ref[0] * y


def _combine_norm_kernel(y0_ref, y1_ref, tp_ref, x_ref, gate_ref, fg_ref, o_ref):
    tp = tp_ref[...]
    y = tp[:, 0:1] * y0_ref[...] + tp[:, 1:2] * y1_ref[...]
    x = x_ref[...] + gate_ref[0] * y
    o_ref[...] = x * lax.rsqrt(jnp.mean(x * x, axis=-1, keepdims=True) + NORM_EPS) * fg_ref[...]


def moe_combine(y, top_p, xs, gate, dims, row_start=0, final_g=None, tm=1024):
    M, D = xs.shape
    off = row_start // tm
    nt = (M - row_start) // tm
    in_specs = [pl.BlockSpec((tm, D), lambda i: (i, 0)),
                pl.BlockSpec((tm, D), lambda i: (nt + i, 0)),
                pl.BlockSpec((tm, LANE), lambda i: (i, 0)),
                pl.BlockSpec((tm, D), lambda i: (i + off, 0)),
                _mod_spec(tm, off, dims)]
    if final_g is None:
        return pl.pallas_call(
            _combine_kernel,
            grid=(nt,),
            in_specs=in_specs,
            out_specs=pl.BlockSpec((tm, D), lambda i: (i + off, 0)),
            out_shape=jax.ShapeDtypeStruct((M, D), F32),
            input_output_aliases={3: 0},
            compiler_params=_cparams(("parallel",)),
            name="moe_combine",
        )(y, y, top_p, xs, gate)
    return pl.pallas_call(
        _combine_norm_kernel,
        grid=(nt,),
        in_specs=in_specs + [pl.BlockSpec((1, D), lambda i: (0, 0))],
        out_specs=pl.BlockSpec((tm, D), lambda i: (i, 0)),
        out_shape=jax.ShapeDtypeStruct((M - row_start, D), F32),
        compiler_params=_cparams(("parallel",)),
        name="moe_combine_norm",
    )(y, y, top_p, xs, gate, final_g.reshape(1, D))


def moe_sublayer(xs, g, shift, scale, gate, w_router, w1, w3, w2, dims, row_start, final_g=None):
    h, top_i, top_p = moe_router(xs, g, shift, scale, w_router, dims, row_start)
    src, dst, block_e, n_used, n_blocks = _route_positions(top_i[:, :TOP_K])
    y = moe_experts(h, src, dst, block_e, n_used, w1, w3, w2, n_blocks)
    return moe_combine(y, top_p, xs, gate, dims, row_start, final_g)


def _final_norm_kernel(x_ref, g_ref, o_ref):
    x = x_ref[...]
    o_ref[...] = x * lax.rsqrt(jnp.mean(x * x, axis=-1, keepdims=True) + NORM_EPS) * g_ref[...]


def final_norm(xs, g, row_start, tm=512):
    M, D = xs.shape
    off = row_start // tm
    return pl.pallas_call(
        _final_norm_kernel,
        grid=((M - row_start) // tm,),
        in_specs=[pl.BlockSpec((tm, D), lambda i: (i + off, 0)),
                  pl.BlockSpec((1, D), lambda i: (0, 0))],
        out_specs=pl.BlockSpec((tm, D), lambda i: (i, 0)),
        out_shape=jax.ShapeDtypeStruct((M - row_start, D), F32),
        compiler_params=_cparams(("parallel",)),
        name="final_norm",
    )(xs, g.reshape(1, D))


def na_mixer(xs, g, m, w_qkv, w_o, rpb, dims, L, need_ctx):
    BL, N, B, D = dims
    col_scale = jnp.concatenate([jnp.full((D,), NA_HEAD_DIM ** -0.5 * LOG2E, F32), jnp.ones((2 * D,), F32)])
    qkv = prenorm_matmul(xs, g, m[0], m[1], w_qkv.astype(BF16), dims, col_scale=col_scale)
    bias_tab = _na_bias_table(rpb.astype(F32))
    w_o = w_o.astype(BF16)
    if need_ctx:
        xs = matmul_residual(ctx_attention(qkv, dims, L), w_o, xs, m[2], dims, row_start=0)
    return matmul_residual(na_attention(qkv, bias_tab, dims, L), w_o, xs, m[2], dims, row_start=BL)


def gdn_mixer(xs, g, m, w_in, conv_w, a_log, dt_bias, norm_g, w_o, dims, L, need_ctx):
    BL, N, B, D = dims
    H = GDN_HEADS
    HD = H * GDN_DK
    w_main = w_in[:, :4 * HD].astype(BF16)
    w_gate = w_in[:, 4 * HD:].astype(F32)
    w_ab = jnp.zeros((D, 2 * LANE), F32)
    for d in range(2):
        w_ab = w_ab.at[:, d * LANE:d * LANE + H].set(w_gate[:, d * H:(d + 1) * H])
        w_ab = w_ab.at[:, d * LANE + H:d * LANE + 2 * H].set(w_gate[:, (2 + d) * H:(3 + d) * H])
    proj, ab = prenorm_matmul(xs, g, m[0], m[1], w_main, dims, w_hi=w_ab)
    feats = gdn_features(proj, conv_w.astype(F32), dims, L)
    w, qd, kt, u, aqk, egl = gdn_intra(feats, ab, a_log, dt_bias)
    o_f, o_b = gdn_scan(w, qd, kt, u, aqk, egl, dims, L)
    return gdn_output(o_f, o_b, proj, norm_g.astype(F32), w_o.astype(BF16), xs, m[2], dims,
                      row_start=0 if need_ctx else BL)


def kernel(x, c, ctx, c_ctx, ada_w, ada_b, norm_g, final_g, na_w_qkv, na_w_o, na_rpb, gdn_w_in, gdn_conv, gdn_a_log, gdn_dt_bias, gdn_norm_g, gdn_w_o, pool_w, pool_scale, ffn_w1, ffn_w3, ffn_w2, moe_router, moe_w1, moe_w3, moe_w2):
    B, N, D = x.shape
    L = ctx.shape[1]
    depth = ada_w.shape[0]
    BL = B * L
    dims = (BL, N, B, D)
    xs = jnp.concatenate([ctx.reshape(BL, D), x.reshape(B * N, D)], axis=0)

    R = -(-(B + 1) // 8) * 8
    cvec = jnp.zeros((R, D), F32).at[:B].set(c).at[B].set(c_ctx)
    mods = ada_table(cvec, ada_w, ada_b)[:, :B + 1].reshape(depth, B + 1, 6, 1, D)

    for i in range(depth):
        last = i == depth - 1
        m = [mods[i, :, k] for k in range(6)]
        j = i // 3
        kind = i % 3
        if kind == 0:
            xs = na_mixer(xs, norm_g[i, 0], m, na_w_qkv[j], na_w_o[j], na_rpb[j], dims, L, not last)
        elif kind == 1:
            xs = gdn_mixer(xs, norm_g[i, 0], m, gdn_w_in[j], gdn_conv[j], gdn_a_log[j], gdn_dt_bias[j],
                           gdn_norm_g[j], gdn_w_o[j], dims, L, not last)
        else:
            xs = pool_sublayer(xs, norm_g[i, 0], m[0], m[1], m[2], pool_w[j], pool_scale[j], dims, L,
                               row_start=0 if not last else BL)
        row_start = BL if last else 0
        f = i // 2
        if i % 2 == 0:
            xs = ffn_sublayer(xs, norm_g[i, 1], m[3], m[4], m[5], ffn_w1[f].astype(BF16),
                              ffn_w3[f].astype(BF16), ffn_w2[f].astype(BF16), dims, row_start=row_start)
        else:
            xs = moe_sublayer(xs, norm_g[i, 1], m[3], m[4], m[5], moe_router[f], moe_w1[f].astype(BF16),
                              moe_w3[f].astype(BF16), moe_w2[f].astype(BF16), dims, row_start,
                              final_g=final_g if last else None)
            if last:
                return xs.reshape(B, N, D)
    return final_norm(xs, final_g, BL).reshape(B, N, D)
```
